```python
import math
import numpy as np
import jax
import jax.numpy as jnp
from jax import lax

D_MODEL = 2048
BATCH = 2
SEQ = 8192
DEPTH = 2

GRID_W = 64
CTX_LEN = 256
HEAD_DIM = 128
ROPE_THETA = 10000.0
EPS = 1e-6
Q_BLOCK = 128
BRANCH_W = D_MODEL // 2
N_BRANCH = 3
A_HEADS = BRANCH_W // HEAD_DIM
A_KV_HEADS = A_HEADS // 4
A_GROUP = A_HEADS // A_KV_HEADS
B_DK = 128
B_DV = 128
B_HEADS = BRANCH_W // B_DV
SCAN_CHUNK = 64
F_MIN = 1e-6
C_DV = 2 * HEAD_DIM
C_HEADS = BRANCH_W // C_DV
N_EXPERTS = 64
TOP_K = 8
N_GROUPS = 8
TOPK_GROUPS = 4
D_EXPERT = D_MODEL // 4
D_SHARED = D_MODEL // 4
ROUTE_SCALE = 2.5
MOE_BLOCK = 128
MASK_SCORE = -1e9
IN_SPLITS = (A_HEADS * HEAD_DIM, A_KV_HEADS * HEAD_DIM, A_KV_HEADS * HEAD_DIM,
             B_HEADS * B_DK, B_HEADS * B_DV, B_HEADS * B_DK, B_HEADS * B_DK, B_HEADS * B_DV,
             C_HEADS * 2 * HEAD_DIM, C_HEADS * 2 * HEAD_DIM, C_HEADS * C_DV,
             N_BRANCH * D_MODEL)
IN_WIDTH = sum(IN_SPLITS)

kernel_name = "hybrid_gqa_hgrn2_diffattn_moe_dit"


def rms_norm(x, w):
    xf = x.astype(jnp.float32)
    y = xf * lax.rsqrt(jnp.mean(xf * xf, axis=-1, keepdims=True) + EPS)
    return (y * w.astype(jnp.float32)).astype(x.dtype)


def modulate(h, shift, scale):
    return h * (1 + scale) + shift


def to_heads(a, n_heads):
    b, l, w = a.shape
    return a.reshape(b, l, n_heads, w // n_heads).transpose(0, 2, 1, 3)


def from_heads(a):
    b, n, l, d = a.shape
    return a.transpose(0, 2, 1, 3).reshape(b, l, n * d)


def axial_rope_tables(rows, dtype):
    t = jnp.arange(rows * GRID_W)
    pos = jnp.stack([t // GRID_W, t % GRID_W], axis=-1).astype(jnp.float32)
    n_freq = HEAD_DIM // 4
    inv_freq = ROPE_THETA ** (-jnp.arange(n_freq, dtype=jnp.float32) / n_freq)
    ang = pos[:, :, None] * inv_freq
    return jnp.cos(ang).astype(dtype), jnp.sin(ang).astype(dtype)


def axial_rope(x, cos, sin):
    xs = x.reshape(x.shape[:-1] + (2, 2, HEAD_DIM // 4))
    x1, x2 = xs[..., 0, :], xs[..., 1, :]
    out = jnp.stack([x1 * cos - x2 * sin, x2 * cos + x1 * sin], axis=-2)
    return out.reshape(x.shape)


def sweep_query_blocks(fn, *qs):
    n = qs[0].shape[-2]
    nb = n // Q_BLOCK

    def to_blocks(a):
        a = a.reshape(a.shape[:-2] + (nb, Q_BLOCK, a.shape[-1]))
        return jnp.moveaxis(a, -3, 0)

    out = lax.map(lambda blk: fn(*blk), tuple(to_blocks(a) for a in qs))
    out = jnp.moveaxis(out, 0, -3)
    return out.reshape(out.shape[:-3] + (n, out.shape[-1]))


def gqa_attend(q, k, v):
    s = jnp.einsum('bhgqd,bhkd->bhgqk', q, k, preferred_element_type=jnp.float32) * (q.shape[-1] ** -0.5)
    p = jax.nn.softmax(s, axis=-1).astype(v.dtype)
    return jnp.einsum('bhgqk,bhkd->bhgqd', p, v)


def diff_attend(q1, q2, k1, k2, v, lam):
    scale = q1.shape[-1] ** -0.5
    p1 = jax.nn.softmax(jnp.einsum('bhqd,bhkd->bhqk', q1, k1, preferred_element_type=jnp.float32) * scale, axis=-1)
    p2 = jax.nn.softmax(jnp.einsum('bhqd,bhkd->bhqk', q2, k2, preferred_element_type=jnp.float32) * scale, axis=-1)
    return jnp.einsum('bhqk,bhkd->bhqd', (p1 - lam * p2).astype(v.dtype), v)


def hgrn2_scan(q, k, v, g, s0):
    length = q.shape[-2]
    n_chunks = length // SCAN_CHUNK

    def chunks(a):
        a = a.reshape(a.shape[:-2] + (n_chunks, SCAN_CHUNK, a.shape[-1]))
        return jnp.moveaxis(a, -3, 0)

    causal = jnp.tril(jnp.ones((SCAN_CHUNK, SCAN_CHUNK), dtype=bool))[:, :, None]

    def step(state, inp):
        qc, kc, vc, gc = inp
        b = jnp.cumsum(gc, axis=-2)
        rel = b[..., :, None, :] - b[..., None, :, :]
        dec = jnp.where(causal, jnp.exp(jnp.where(causal, rel, 0.0)), 0.0)
        att = jnp.einsum('...td,...sd,...tsd->...ts', qc, kc, dec)
        o = att @ vc + jnp.einsum('...td,...de->...te', qc * jnp.exp(b), state)
        b_last = b[..., -1:, :]
        new_state = jnp.exp(b_last[..., 0, :])[..., None] * state + jnp.einsum(
            '...sd,...se->...de', kc * jnp.exp(b_last - b), vc)
        return new_state, o

    final, o = lax.scan(step, s0, (chunks(q), chunks(k), chunks(v), chunks(g)))
    o = jnp.moveaxis(o, 0, -3)
    return o.reshape(o.shape[:-3] + (length, o.shape[-1])), final


def token_mixers(h, n_ctx, cos, sin, w_in, q_norm_a, k_norm_a, lower_bound, hgrn_norm,
                 q_norm_c, k_norm_c, lam_q1, lam_k1, lam_q2, lam_k2, diff_norm, lam_init,
                 w_branch, w_out, need_ctx):
    bsz, m, _ = h.shape
    split_at = np.cumsum(IN_SPLITS)[:-1].tolist()
    (wq_a, wk_a, wv_a, wq_b, wi_b, wf_fwd, wf_bwd, wog_b,
     wq_c, wk_c, wv_c, w_gate) = jnp.split(w_in, split_at, axis=1)

    def ctx_part(a):
        return a[..., :n_ctx, :]

    def lat_part(a):
        return a[..., n_ctx:, :]

    qa = rms_norm(to_heads(h @ wq_a, A_HEADS), q_norm_a).reshape(bsz, A_KV_HEADS, A_GROUP, m, HEAD_DIM)
    ka = rms_norm(to_heads(h @ wk_a, A_KV_HEADS), k_norm_a)
    va = to_heads(h @ wv_a, A_KV_HEADS)
    ka_all = jnp.concatenate([ctx_part(ka), axial_rope(lat_part(ka), cos, sin)], axis=-2)
    oa_x = sweep_query_blocks(lambda qb: gqa_attend(qb, ka_all, va), axial_rope(lat_part(qa), cos, sin))

    def heads_a(o):
        return from_heads(o.reshape(bsz, A_HEADS, o.shape[-2], HEAD_DIM))

    qb = to_heads(jax.nn.silu(h @ wq_b), B_HEADS).astype(jnp.float32)
    vb = to_heads(h @ wi_b, B_HEADS).astype(jnp.float32)
    lb = lower_bound.reshape(B_HEADS, 1, B_DK)

    def forget(w_f):
        z = to_heads(h @ w_f, B_HEADS).astype(jnp.float32)
        f = lb + (1.0 - lb) * jax.nn.sigmoid(z)
        log_f = jnp.log(jnp.maximum(f, F_MIN))
        return (1.0 - lb) * jax.nn.sigmoid(-z), log_f

    kf, gf = forget(wf_fwd)
    kb, gb = forget(wf_bwd)

    def dir_inputs(part):
        return tuple(jnp.stack([part(a_f), jnp.flip(part(a_b), axis=-2)])
                     for a_f, a_b in ((qb, qb), (kf, kb), (vb, vb), (gf, gb)))

    s0 = jnp.zeros((2, bsz, B_HEADS, B_DK, B_DV), jnp.float32)
    o_ctx_b, s_ctx = hgrn2_scan(*dir_inputs(ctx_part), s0)
    o_lat_b, _ = hgrn2_scan(*dir_inputs(lat_part), s_ctx)
    og = h @ wog_b

    def readout_b(o, og_part):
        o = o[0] + jnp.flip(o[1], axis=-2)
        return from_heads(rms_norm(o, hgrn_norm)).astype(h.dtype) * jax.nn.silu(og_part)

    def pair_heads(a):
        return a.reshape(bsz, m, C_HEADS, 2, HEAD_DIM).transpose(0, 2, 3, 1, 4)

    qc = rms_norm(pair_heads(h @ wq_c), q_norm_c)
    kc = rms_norm(pair_heads(h @ wk_c), k_norm_c)
    vc = to_heads(h @ wv_c, C_HEADS)
    f32 = jnp.float32
    lam = (jnp.exp(jnp.sum(lam_q1.astype(f32) * lam_k1.astype(f32)))
           - jnp.exp(jnp.sum(lam_q2.astype(f32) * lam_k2.astype(f32))) + lam_init)
    kc_all = jnp.concatenate([ctx_part(kc), axial_rope(lat_part(kc), cos, sin)], axis=-2)
    k1_all, k2_all = kc_all[:, :, 0], kc_all[:, :, 1]
    qc_lat = axial_rope(lat_part(qc), cos, sin)
    oc_x = sweep_query_blocks(lambda q1, q2: diff_attend(q1, q2, k1_all, k2_all, vc, lam),
                              qc_lat[:, :, 0], qc_lat[:, :, 1])

    def readout_c(o):
        return from_heads(rms_norm(o, diff_norm) * (1.0 - lam_init))

    gate_logits = h @ w_gate

    def merge(o_a, o_b, o_c, gl):
        y = jnp.einsum('rbld,rde->rble', jnp.stack([o_a, o_b, o_c]), w_branch)
        g = jax.nn.sigmoid(gl.reshape(gl.shape[:-1] + (N_BRANCH, D_MODEL)))
        return jnp.einsum('blre,rble->ble', g, y) @ w_out

    out_x = merge(heads_a(oa_x), readout_b(o_lat_b, lat_part(og)), readout_c(oc_x), lat_part(gate_logits))
    if not need_ctx:
        return out_x, None
    oa_c = gqa_attend(ctx_part(qa), ctx_part(ka), ctx_part(va))
    qc_c, kc_c = ctx_part(qc), ctx_part(kc)
    oc_c = diff_attend(qc_c[:, :, 0], qc_c[:, :, 1], kc_c[:, :, 0], kc_c[:, :, 1], ctx_part(vc), lam)
    out_c = merge(heads_a(oa_c), readout_b(o_ctx_b, ctx_part(og)), readout_c(oc_c), ctx_part(gate_logits))
    return out_x, out_c


def route(h, w_router, b_router):
    n_tok = h.shape[0]
    scores = jax.nn.sigmoid((h @ w_router).astype(jnp.float32))
    biased = scores + b_router.astype(jnp.float32)
    grp = biased.reshape(n_tok, N_GROUPS, N_EXPERTS // N_GROUPS)
    grp_score = lax.top_k(grp, 2)[0].sum(-1)
    _, top_g = lax.top_k(grp_score, TOPK_GROUPS)
    gmask = jnp.sum(jax.nn.one_hot(top_g, N_GROUPS, dtype=jnp.float32), axis=-2)
    emask = jnp.repeat(gmask, N_EXPERTS // N_GROUPS, axis=-1) > 0
    _, top_e = lax.top_k(jnp.where(emask, biased, MASK_SCORE), TOP_K)
    w = jnp.take_along_axis(scores, top_e, axis=-1)
    return top_e, w / jnp.sum(w, axis=-1, keepdims=True) * ROUTE_SCALE


def moe_ffn(h, w_router, b_router, w_gate, w_up, w_down, ws_gate, ws_up, ws_down):
    lead = h.shape[:-1]
    hf = h.reshape(-1, D_MODEL)
    n_tok = hf.shape[0]
    top_e, top_w = route(hf, w_router, b_router)
    n_assign = n_tok * TOP_K
    flat_e = top_e.reshape(-1)
    flat_tok = jnp.arange(n_assign) // TOP_K
    flat_w = top_w.reshape(-1)
    order = jnp.argsort(flat_e)
    e_sorted = flat_e[order]
    counts = jnp.bincount(flat_e, length=N_EXPERTS)
    padded = (counts + MOE_BLOCK - 1) // MOE_BLOCK * MOE_BLOCK
    pad_end = jnp.cumsum(padded)
    pad_start = pad_end - padded
    grp_start = jnp.cumsum(counts) - counts
    dest = pad_start[e_sorted] + jnp.arange(n_assign) - grp_start[e_sorted]
    n_blocks = -(-n_assign // MOE_BLOCK) + N_EXPERTS
    n_slots = n_blocks * MOE_BLOCK
    slot_tok = jnp.zeros((n_slots,), jnp.int32).at[dest].set(flat_tok[order].astype(jnp.int32))
    slot_w = jnp.zeros((n_slots,), jnp.float32).at[dest].set(flat_w[order])
    block_e = jnp.minimum(jnp.searchsorted(pad_end, jnp.arange(n_blocks) * MOE_BLOCK, side='right'),
                          N_EXPERTS - 1)

    def expert_block(args):
        tok, e = args
        xb = hf[tok]
        return (jax.nn.silu(xb @ w_gate[e]) * (xb @ w_up[e])) @ w_down[e]

    yb = lax.map(expert_block, (slot_tok.reshape(n_blocks, MOE_BLOCK), block_e))
    routed = jnp.zeros_like(hf).at[slot_tok].add(
        yb.reshape(n_slots, D_MODEL) * slot_w[:, None].astype(yb.dtype))
    shared = (jax.nn.silu(hf @ ws_gate) * (hf @ ws_up)) @ ws_down
    return (routed + shared).reshape(lead + (D_MODEL,))


def setup_inputs(seed: int = 0) -> dict:
    key = jax.random.key(seed)
    keys = iter(jax.random.split(key, 32))

    def normal(shape, std):
        return std * jax.random.normal(next(keys), shape, jnp.float32)

    def gain(shape):
        return 1.0 + normal(shape, 0.05)

    d = D_MODEL
    return {
        "x": normal((BATCH, SEQ, d), 1.0),
        "c": normal((BATCH, d), 1.0),
        "ctx": normal((BATCH, CTX_LEN, d), 1.0),
        "c_ctx": normal((d,), 1.0),
        "w_ada": normal((DEPTH, d, 6 * d), 0.5 * d ** -0.5),
        "b_ada": normal((DEPTH, 6 * d), 0.02),
        "norm_mix": gain((DEPTH, d)),
        "norm_ffn": gain((DEPTH, d)),
        "w_in": normal((DEPTH, d, IN_WIDTH), d ** -0.5),
        "q_norm_a": gain((DEPTH, HEAD_DIM)),
        "k_norm_a": gain((DEPTH, HEAD_DIM)),
        "hgrn_lb_logits": normal((DEPTH, B_HEADS * B_DK), 0.5),
        "hgrn_out_norm": gain((DEPTH, B_DV)),
        "q_norm_c": gain((DEPTH, HEAD_DIM)),
        "k_norm_c": gain((DEPTH, HEAD_DIM)),
        "lambda_q1": normal((DEPTH, HEAD_DIM), 0.1),
        "lambda_k1": normal((DEPTH, HEAD_DIM), 0.1),
        "lambda_q2": normal((DEPTH, HEAD_DIM), 0.1),
        "lambda_k2": normal((DEPTH, HEAD_DIM), 0.1),
        "diff_sub_norm": gain((DEPTH, C_DV)),
        "w_branch": normal((DEPTH, N_BRANCH, BRANCH_W, d), BRANCH_W ** -0.5),
        "w_out": normal((DEPTH, d, d), d ** -0.5),
        "w_router": normal((DEPTH, d, N_EXPERTS), d ** -0.5),
        "b_router": normal((DEPTH, N_EXPERTS), 0.01),
        "w_exp_gate": normal((DEPTH, N_EXPERTS, d, D_EXPERT), d ** -0.5),
        "w_exp_up": normal((DEPTH, N_EXPERTS, d, D_EXPERT), d ** -0.5),
        "w_exp_down": normal((DEPTH, N_EXPERTS, D_EXPERT, d), D_EXPERT ** -0.5),
        "w_sh_gate": normal((DEPTH, d, D_SHARED), d ** -0.5),
        "w_sh_up": normal((DEPTH, d, D_SHARED), d ** -0.5),
        "w_sh_down": normal((DEPTH, D_SHARED, d), D_SHARED ** -0.5),
    }


def reference(x, c, ctx, c_ctx, w_ada, b_ada, norm_mix, norm_ffn, w_in, q_norm_a, k_norm_a,
              hgrn_lb_logits, hgrn_out_norm, q_norm_c, k_norm_c, lambda_q1, lambda_k1,
              lambda_q2, lambda_k2, diff_sub_norm, w_branch, w_out, w_router, b_router,
              w_exp_gate, w_exp_up, w_exp_down, w_sh_gate, w_sh_up, w_sh_down):
    bsz, n_lat, _ = x.shape
    n_ctx = ctx.shape[1]
    rows = n_lat // GRID_W
    cos, sin = axial_rope_tables(rows, x.dtype)
    p_lb = jax.nn.softmax(hgrn_lb_logits.astype(jnp.float32), axis=0)
    lower_bounds = jnp.cumsum(p_lb, axis=0) - p_lb[0]
    silu_c = jax.nn.silu(c)
    silu_cc = jax.nn.silu(c_ctx)
    xc = ctx
    for l in range(DEPTH):
        need_ctx = l < DEPTH - 1
        lam_init = 0.8 - 0.6 * math.exp(-0.3 * l)
        mod_x = (silu_c @ w_ada[l] + b_ada[l]).reshape(bsz, 6, 1, D_MODEL)
        mod_c = (silu_cc @ w_ada[l] + b_ada[l]).reshape(6, D_MODEL)
        hx = modulate(rms_norm(x, norm_mix[l]), mod_x[:, 0], mod_x[:, 1])
        hc = modulate(rms_norm(xc, norm_mix[l]), mod_c[0], mod_c[1])
        ox, oc = token_mixers(jnp.concatenate([hc, hx], axis=1), n_ctx, cos, sin, w_in[l],
                              q_norm_a[l], k_norm_a[l], lower_bounds[l], hgrn_out_norm[l],
                              q_norm_c[l], k_norm_c[l], lambda_q1[l], lambda_k1[l],
                              lambda_q2[l], lambda_k2[l], diff_sub_norm[l], lam_init,
                              w_branch[l], w_out[l], need_ctx)
        x = x + mod_x[:, 2] * ox
        hx = modulate(rms_norm(x, norm_ffn[l]), mod_x[:, 3], mod_x[:, 4])
        if need_ctx:
            xc = xc + mod_c[2] * oc
            hc = modulate(rms_norm(xc, norm_ffn[l]), mod_c[3], mod_c[4])
            f = moe_ffn(jnp.concatenate([hc, hx], axis=1), w_router[l], b_router[l], w_exp_gate[l],
                        w_exp_up[l], w_exp_down[l], w_sh_gate[l], w_sh_up[l], w_sh_down[l])
            xc = xc + mod_c[5] * f[:, :n_ctx]
            x = x + mod_x[:, 5] * f[:, n_ctx:]
        else:
            x = x + mod_x[:, 5] * moe_ffn(hx, w_router[l], b_router[l], w_exp_gate[l], w_exp_up[l],
                                          w_exp_down[l], w_sh_gate[l], w_sh_up[l], w_sh_down[l])
    return x
```

```python
import functools
import math

import jax
import jax.numpy as jnp
from jax import lax
from jax.experimental import pallas as pl
from jax.experimental.pallas import tpu as pltpu

f32 = jnp.float32
bf16 = jnp.bfloat16
i32 = jnp.int32

HEAD_DIM = 128
GRID_W = 64
ROPE_THETA = 10000.0
EPS = 1e-6
F_MIN = 1e-6
SCAN_CHUNK = 64
SCAN_SUB = 8
N_GROUPS = 8
TOPK_GROUPS = 4
TOP_K = 8
ROUTE_SCALE = 2.5
MASK_SCORE = -1e9
LOG2E = 1.4426950408889634

VMEM_LIMIT_BYTES = 48 * 1024 * 1024
ROW_TILE = 256
MM_TILE_M = 768
MOE_TILE = 256
SCAN_TILE = 256


def _params(*semantics):
    return pltpu.CompilerParams(dimension_semantics=semantics, vmem_limit_bytes=VMEM_LIMIT_BYTES)


def _silu(x):
    return x * jax.nn.sigmoid(x)


def _nt_dot(a, b):
    return lax.dot_general(a, b, (((1,), (1,)), ((), ())), preferred_element_type=f32)


def _tn_dot(a, b):
    return lax.dot_general(a, b, (((0,), (0,)), ((), ())), preferred_element_type=f32)


def _row_is_ctx(tile_index, tile_rows, rows_per_batch, n_ctx):
    row0 = (tile_index % (rows_per_batch // tile_rows)) * tile_rows
    rid = row0 + lax.broadcasted_iota(i32, (tile_rows, 1), 0)
    return rid < n_ctx


def _fm_kernel(*refs, n_pairs, n_extra, a_fn, epilogue):
    pair_refs = refs[:2 * n_pairs]
    extra = refs[2 * n_pairs:2 * n_pairs + n_extra]
    outs = refs[2 * n_pairs + n_extra:]
    accs = []
    for p in range(n_pairs):
        a = pair_refs[2 * p][...]
        if a_fn is not None:
            a = a_fn(a)
        accs.append(jnp.dot(a.astype(bf16), pair_refs[2 * p + 1][...].astype(bf16),
                            preferred_element_type=f32))
    epilogue(accs, extra, outs)


def fused_matmul(pairs, extras, epilogue, out_shapes, out_specs, tm, tn, a_fn=None, name=None):
    m = pairs[0][0].shape[0]
    n = pairs[0][1].shape[1]
    assert m % tm == 0 and n % tn == 0, (m, tm, n, tn)
    in_specs, args = [], []
    for a, w in pairs:
        k = a.shape[1]
        assert w.shape[0] == k
        in_specs += [pl.BlockSpec((tm, k), lambda i, j: (i, 0)), pl.BlockSpec((k, tn), lambda i, j: (0, j))]
        args += [a, w]
    for arr, spec in extras:
        in_specs.append(spec)
        args.append(arr)
    return pl.pallas_call(
        functools.partial(_fm_kernel, n_pairs=len(pairs), n_extra=len(extras), a_fn=a_fn, epilogue=epilogue),
        grid=(m // tm, n // tn),
        in_specs=in_specs,
        out_specs=out_specs,
        out_shape=out_shapes,
        compiler_params=_params("parallel", "arbitrary"),
        name=name,
    )(*args)


def _tile_spec(tm, tn):
    return pl.BlockSpec((tm, tn), lambda i, j: (i, j))


def _ep_store(dtype):
    def ep(accs, extra, outs):
        outs[0][...] = accs[0].astype(dtype)
    return ep


def _ep_bias(accs, extra, outs):
    outs[0][...] = accs[0] + extra[0][...]


def _ep_sigmoid(accs, extra, outs):
    outs[0][...] = jax.nn.sigmoid(accs[0]).astype(outs[0].dtype)


def _ep_qk(accs, extra, outs):
    gain_ref, cos_ref, sin_ref = extra
    acc = accs[0]
    cos = cos_ref[...]
    sin = sin_ref[...]
    lane = lax.broadcasted_iota(i32, cos.shape, 1)
    first_half = (lane % (HEAD_DIM // 2)) < (HEAD_DIM // 4)
    for c in range(acc.shape[1] // HEAD_DIM):
        x = acc[:, c * HEAD_DIM:(c + 1) * HEAD_DIM]
        y = x * lax.rsqrt(jnp.mean(x * x, axis=-1, keepdims=True) + EPS)
        y = y * gain_ref[:, c * HEAD_DIM:(c + 1) * HEAD_DIM]
        partner = jnp.where(first_half,
                            pltpu.roll(y, HEAD_DIM - HEAD_DIM // 4, 1),
                            pltpu.roll(y, HEAD_DIM // 4, 1))
        outs[0][c] = (y * cos + partner * sin).astype(outs[0].dtype)


def _ep_merge(accs, extra, outs):
    y = extra[0][...].astype(f32) * accs[0]
    y = y + extra[1][...].astype(f32) * accs[1]
    y = y + extra[2][...].astype(f32) * accs[2]
    outs[0][...] = y.astype(outs[0].dtype)


def _ep_glu(accs, extra, outs):
    outs[0][...] = (_silu(accs[0]) * accs[1]).astype(outs[0].dtype)


def _make_ep_residual(tm, rows_per_batch, n_ctx, with_routed):
    def ep(accs, extra, outs):
        x_ref, mod_ctx_ref, mod_b_ref = extra[:3]
        y = accs[0]
        if with_routed:
            y = y + extra[3][...]
        is_ctx = _row_is_ctx(pl.program_id(0), tm, rows_per_batch, n_ctx)
        mod = jnp.where(is_ctx, mod_ctx_ref[0], mod_b_ref[0])
        outs[0][...] = x_ref[...] + mod * y
    return ep


def _norm_kernel(x_ref, w_ref, shc_ref, shb_ref, scc_ref, scb_ref, *outs, tile, rows_per_batch, n_ctx):
    x = x_ref[...]
    y = x * lax.rsqrt(jnp.mean(x * x, axis=-1, keepdims=True) + EPS) * w_ref[...]
    is_ctx = _row_is_ctx(pl.program_id(0), tile, rows_per_batch, n_ctx)
    shift = jnp.where(is_ctx, shc_ref[0], shb_ref[0])
    scale = jnp.where(is_ctx, scc_ref[0], scb_ref[0])
    h = y * (1.0 + scale) + shift
    for o in outs:
        o[...] = h.astype(o.dtype)


def norm_modulate(x, w, shift, scale, rows_per_batch, n_ctx, out_dtypes):
    t, d = x.shape
    tile = ROW_TILE
    n_b = shift.shape[0] - 1
    tiles_per_batch = rows_per_batch // tile
    ctx_spec = pl.BlockSpec((1, 1, d), lambda i: (n_b, 0, 0))
    b_spec = pl.BlockSpec((1, 1, d), lambda i: (i // tiles_per_batch, 0, 0))
    row_spec = pl.BlockSpec((tile, d), lambda i: (i, 0))
    return pl.pallas_call(
        functools.partial(_norm_kernel, tile=tile, rows_per_batch=rows_per_batch, n_ctx=n_ctx),
        grid=(t // tile,),
        in_specs=[row_spec, pl.BlockSpec((1, d), lambda i: (0, 0)), ctx_spec, b_spec, ctx_spec, b_spec],
        out_specs=[row_spec for _ in out_dtypes],
        out_shape=[jax.ShapeDtypeStruct((t, d), dt) for dt in out_dtypes],
        compiler_params=_params("parallel"),
        name="norm_modulate",
    )(x, w.reshape(1, d), shift, shift, scale, scale)


def _attend(q, k_at, v_at, kv_len, tk, dv):
    rows = q.shape[0]

    def body(c, carry):
        m, l, acc = carry
        off = pl.multiple_of(c * tk, tk)
        k = k_at(off)
        v = v_at(off)
        s = _nt_dot(q, k)
        m_new = jnp.maximum(m, jnp.max(s, axis=1, keepdims=True))
        alpha = jnp.exp2(m - m_new)
        p = jnp.exp2(s - m_new)
        l = alpha * l + jnp.sum(p, axis=1, keepdims=True)
        acc = alpha * acc + jnp.dot(p.astype(bf16), v, preferred_element_type=f32)
        return m_new, l, acc

    init = (jnp.full((rows, 1), -jnp.inf, f32), jnp.zeros((rows, 1), f32), jnp.zeros((rows, dv), f32))
    _, l, acc = lax.fori_loop(0, kv_len // tk, body, init)
    return acc / l


def _attn_a_kernel(q_ref, k_ref, v_ref, *rest, kv_len, tk):
    o_ref = rest[-1]
    g, tq, d = q_ref.shape
    q = q_ref[...].reshape(g * tq, d)
    o = _attend(q, lambda off: k_ref[0, pl.ds(off, tk), :], lambda off: v_ref[pl.ds(off, tk), :], kv_len, tk,
                v_ref.shape[-1])
    for h in range(g):
        o_ref[:, h * d:(h + 1) * d] = o[h * tq:(h + 1) * tq].astype(o_ref.dtype)


def _attn_c_kernel(lam_ref, dn_ref, q_ref, k_ref, v_ref, *rest, kv_len, tk, lam_init):
    o_ref = rest[-1]
    dv = v_ref.shape[-1]
    o1 = _attend(q_ref[0], lambda off: k_ref[0, pl.ds(off, tk), :], lambda off: v_ref[pl.ds(off, tk), :],
                 kv_len, tk, dv)
    o2 = _attend(q_ref[1], lambda off: k_ref[1, pl.ds(off, tk), :], lambda off: v_ref[pl.ds(off, tk), :],
                 kv_len, tk, dv)
    lam_p = lam_ref[...]
    lam = (jnp.exp(jnp.sum(lam_p[0:1] * lam_p[1:2], axis=-1, keepdims=True))
           - jnp.exp(jnp.sum(lam_p[2:3] * lam_p[3:4], axis=-1, keepdims=True)) + lam_init)
    o = o1 - lam * o2
    y = o * lax.rsqrt(jnp.mean(o * o, axis=-1, keepdims=True) + EPS) * dn_ref[...]
    o_ref[...] = (y * (1.0 - lam_init)).astype(o_ref.dtype)


def _kv_chunk(kv_len):
    for tk in (768, 1408, 512, 256, 128):
        if kv_len % tk == 0:
            return tk
    raise ValueError(kv_len)


def attention_a(qk, va, prev, *, n_batch, rows_per_batch, n_ctx, a_heads, a_kv_heads, ctx_pass):
    t = qk.shape[1]
    group = a_heads // a_kv_heads
    tq = n_ctx if ctx_pass else ROW_TILE
    kv_len = n_ctx if ctx_pass else rows_per_batch
    kv_per_batch = rows_per_batch // kv_len
    q_tiles_per_batch = rows_per_batch // tq
    n_q = 1 if ctx_pass else (rows_per_batch - n_ctx) // tq
    q_off = 0 if ctx_pass else n_ctx // tq

    def q_tile(b, i):
        return b * q_tiles_per_batch + q_off + i

    in_specs = [
        pl.BlockSpec((group, tq, HEAD_DIM), lambda b, h, i: (h, q_tile(b, i), 0)),
        pl.BlockSpec((1, kv_len, HEAD_DIM), lambda b, h, i: (a_heads + h, b * kv_per_batch, 0)),
        pl.BlockSpec((kv_len, HEAD_DIM), lambda b, h, i: (b * kv_per_batch, h)),
    ]
    args = [qk, qk, va]
    aliases = {}
    if prev is not None:
        in_specs.append(pl.BlockSpec(memory_space=pl.ANY))
        args.append(prev)
        aliases = {3: 0}
    return pl.pallas_call(
        functools.partial(_attn_a_kernel, kv_len=kv_len, tk=_kv_chunk(kv_len)),
        grid=(n_batch, a_kv_heads, n_q),
        in_specs=in_specs,
        out_specs=pl.BlockSpec((tq, group * HEAD_DIM), lambda b, h, i: (q_tile(b, i), h)),
        out_shape=jax.ShapeDtypeStruct((t, a_heads * HEAD_DIM), bf16),
        input_output_aliases=aliases,
        compiler_params=_params("parallel", "parallel", "arbitrary"),
        name="attention_a_ctx" if ctx_pass else "attention_a",
    )(*args)


def attention_c(qk, vc, lam_params, diff_norm, prev, *, n_batch, rows_per_batch, n_ctx, head0_q, head0_k,
                c_heads, lam_init, ctx_pass):
    t = qk.shape[1]
    dv = 2 * HEAD_DIM
    tq = n_ctx if ctx_pass else ROW_TILE
    kv_len = n_ctx if ctx_pass else rows_per_batch
    kv_per_batch = rows_per_batch // kv_len
    q_tiles_per_batch = rows_per_batch // tq
    n_q = 1 if ctx_pass else (rows_per_batch - n_ctx) // tq
    q_off = 0 if ctx_pass else n_ctx // tq

    def q_tile(b, i):
        return b * q_tiles_per_batch + q_off + i

    in_specs = [
        pl.BlockSpec((4, HEAD_DIM), lambda b, h, i: (0, 0)),
        pl.BlockSpec((1, dv), lambda b, h, i: (0, 0)),
        pl.BlockSpec((2, tq, HEAD_DIM), lambda b, h, i: (head0_q // 2 + h, q_tile(b, i), 0)),
        pl.BlockSpec((2, kv_len, HEAD_DIM), lambda b, h, i: (head0_k // 2 + h, b * kv_per_batch, 0)),
        pl.BlockSpec((kv_len, dv), lambda b, h, i: (b * kv_per_batch, h)),
    ]
    args = [lam_params, diff_norm.reshape(1, dv), qk, qk, vc]
    aliases = {}
    if prev is not None:
        in_specs.append(pl.BlockSpec(memory_space=pl.ANY))
        args.append(prev)
        aliases = {5: 0}
    return pl.pallas_call(
        functools.partial(_attn_c_kernel, kv_len=kv_len, tk=_kv_chunk(kv_len), lam_init=lam_init),
        grid=(n_batch, c_heads, n_q),
        in_specs=in_specs,
        out_specs=pl.BlockSpec((tq, dv), lambda b, h, i: (q_tile(b, i), h)),
        out_shape=jax.ShapeDtypeStruct((t, c_heads * dv), bf16),
        input_output_aliases=aliases,
        compiler_params=_params("parallel", "parallel", "arbitrary"),
        name="attention_c_ctx" if ctx_pass else "attention_c",
    )(*args)


def _hgrn_chunk(qp, v, z, lb, st, rev):
    c = SCAN_CHUNK
    nb = c // SCAN_SUB
    q = _silu(qp.astype(f32))
    one_m = 1.0 - lb
    g = jnp.log(jnp.maximum(lb + one_m * jax.nn.sigmoid(z), F_MIN))
    k = one_m * jax.nn.sigmoid(-z)
    vf = v.astype(f32)

    r_i = lax.broadcasted_iota(i32, (c, c), 0)
    c_i = lax.broadcasted_iota(i32, (c, c), 1)
    tri = (c_i >= r_i) if rev else (c_i <= r_i)
    b = jnp.dot(tri.astype(f32), g, precision=lax.Precision.HIGHEST, preferred_element_type=f32)
    end = 0 if rev else c - 1
    b_end = b[end:end + 1, :]

    o = _nt_dot((q * jnp.exp(b)).astype(bf16), st.astype(bf16))
    k_hat = (k * jnp.exp(b_end - b)).astype(bf16)
    st_new = st * jnp.exp(b_end) + _tn_dot(v, k_hat)

    def edge_row(j):
        return j * SCAN_SUB if rev else j * SCAN_SUB + SCAN_SUB - 1

    row = lax.broadcasted_iota(i32, (c, HEAD_DIM), 0)
    rho_own = jnp.concatenate(
        [jnp.broadcast_to(b[edge_row(j):edge_row(j) + 1, :], (SCAN_SUB, HEAD_DIM)) for j in range(nb)], axis=0)
    k_til = k * jnp.exp(rho_own - b)
    qs, ks = [], []
    for j in (range(1, nb) if rev else range(nb - 1)):
        rho = b[edge_row(j):edge_row(j) + 1, :]
        q_rows = (row < j * SCAN_SUB) if rev else (row >= (j + 1) * SCAN_SUB)
        e = jnp.exp(jnp.minimum(b - rho, 0.0))
        qs.append(jnp.where(q_rows, q * e, 0.0).astype(bf16))
        k_rows = (row >= j * SCAN_SUB) & (row < (j + 1) * SCAN_SUB)
        ks.append(jnp.where(k_rows, k_til, 0.0).astype(bf16))
    a_off = _nt_dot(jnp.concatenate(qs, axis=1), jnp.concatenate(ks, axis=1))
    o = o + jnp.dot(a_off.astype(bf16), v, preferred_element_type=f32)

    sub_r = lax.broadcasted_iota(i32, (SCAN_SUB, HEAD_DIM), 0)
    o_diag = []
    for blk in range(nb):
        sl = slice(blk * SCAN_SUB, (blk + 1) * SCAN_SUB)
        b_i, q_i, k_i, v_i = b[sl], q[sl], k[sl], vf[sl]
        acc = jnp.zeros((SCAN_SUB, HEAD_DIM), f32)
        for s in range(SCAN_SUB):
            t_rows = (sub_r <= s) if rev else (sub_r >= s)
            e = jnp.exp(jnp.minimum(b_i - b_i[s:s + 1], 0.0))
            p = jnp.where(t_rows, q_i * e * k_i[s:s + 1], 0.0)
            acc = acc + jnp.sum(p, axis=1, keepdims=True) * v_i[s:s + 1]
        o_diag.append(acc)
    return o + jnp.concatenate(o_diag, axis=0), st_new


def _hgrn_kernel(lb_ref, qf_ref, vf_ref, zf_ref, qb_ref, vb_ref, zb_ref, of_ref, ob_ref, sf_ref, sb_ref):
    @pl.when(pl.program_id(2) == 0)
    def _():
        sf_ref[...] = jnp.zeros_like(sf_ref)
        sb_ref[...] = jnp.zeros_like(sb_ref)

    lb = lb_ref[...]
    n_chunks = qf_ref.shape[0] // SCAN_CHUNK

    def body(c, carry):
        rf = pl.ds(pl.multiple_of(c * SCAN_CHUNK, SCAN_CHUNK), SCAN_CHUNK)
        o, st = _hgrn_chunk(qf_ref[rf, :], vf_ref[rf, :], zf_ref[rf, :], lb, sf_ref[...], False)
        of_ref[rf, :] = o
        sf_ref[...] = st
        rb = pl.ds(pl.multiple_of((n_chunks - 1 - c) * SCAN_CHUNK, SCAN_CHUNK), SCAN_CHUNK)
        o, st = _hgrn_chunk(qb_ref[rb, :], vb_ref[rb, :], zb_ref[rb, :], lb, sb_ref[...], True)
        ob_ref[rb, :] = o
        sb_ref[...] = st
        return carry

    lax.fori_loop(0, n_chunks, body, 0)


def hgrn_scan(qio, zfb, lower_bound, *, n_batch, rows_per_batch, b_heads):
    t = qio.shape[0]
    tile = SCAN_TILE
    n_tiles = rows_per_batch // tile

    def fwd(b, h, j):
        return b * n_tiles + j

    def bwd(b, h, j):
        return b * n_tiles + jnp.where(j == 0, 0, n_tiles - j)

    def spec(tile_fn, col0):
        return pl.BlockSpec((tile, HEAD_DIM), lambda b, h, j: (tile_fn(b, h, j), col0 + h))

    return pl.pallas_call(
        _hgrn_kernel,
        grid=(n_batch, b_heads, n_tiles),
        in_specs=[
            pl.BlockSpec((1, HEAD_DIM), lambda b, h, j: (0, h)),
            spec(fwd, 0), spec(fwd, b_heads), spec(fwd, 0),
            spec(bwd, 0), spec(bwd, b_heads), spec(bwd, b_heads),
        ],
        out_specs=[spec(fwd, 0), spec(bwd, 0)],
        out_shape=[jax.ShapeDtypeStruct((t, b_heads * HEAD_DIM), f32)] * 2,
        scratch_shapes=[pltpu.VMEM((HEAD_DIM, HEAD_DIM), f32)] * 2,
        compiler_params=_params("parallel", "parallel", "arbitrary"),
        name="hgrn_scan",
    )(lower_bound.reshape(1, -1), qio, qio, zfb, qio, qio, zfb)


def _hgrn_out_kernel(of_ref, ob_ref, og_ref, w_ref, o_ref):
    o = of_ref[...] + ob_ref[...]
    w = w_ref[...]
    for c in range(o.shape[1] // HEAD_DIM):
        sl = slice(c * HEAD_DIM, (c + 1) * HEAD_DIM)
        x = o[:, sl]
        y = x * lax.rsqrt(jnp.mean(x * x, axis=-1, keepdims=True) + EPS) * w
        o_ref[:, sl] = (y * _silu(og_ref[:, sl].astype(f32))).astype(o_ref.dtype)


def hgrn_readout(o_f, o_b, qio, hgrn_norm, b_heads):
    t, w = o_f.shape
    tile = ROW_TILE
    row_spec = pl.BlockSpec((tile, w), lambda i: (i, 0))
    return pl.pallas_call(
        _hgrn_out_kernel,
        grid=(t // tile,),
        in_specs=[row_spec, row_spec, pl.BlockSpec((tile, w), lambda i: (i, 2)),
                  pl.BlockSpec((1, HEAD_DIM), lambda i: (0, 0))],
        out_specs=row_spec,
        out_shape=jax.ShapeDtypeStruct((t, w), bf16),
        compiler_params=_params("parallel"),
        name="hgrn_readout",
    )(o_f, o_b, qio, hgrn_norm.reshape(1, HEAD_DIM))


def _router_kernel(h_ref, w_ref, b_ref, e_ref, wt_ref):
    n_e = w_ref.shape[0]
    tm = h_ref.shape[0]
    per = n_e // N_GROUPS
    scores = jax.nn.sigmoid(_nt_dot(w_ref[...], h_ref[...]))
    biased = scores + b_ref[...]
    neg = -jnp.inf

    grp = biased.reshape(N_GROUPS, per, tm)
    idx_in = lax.broadcasted_iota(i32, grp.shape, 1)
    m1 = jnp.max(grp, axis=1, keepdims=True)
    first = jnp.min(jnp.where(grp == m1, idx_in, per), axis=1, keepdims=True)
    m2 = jnp.max(jnp.where(idx_in == first, neg, grp), axis=1, keepdims=True)
    cur = m1 + m2

    g_idx = lax.broadcasted_iota(i32, cur.shape, 0)
    g_sel = jnp.zeros(cur.shape, f32)
    for _ in range(TOPK_GROUPS):
        mx = jnp.max(cur, axis=0, keepdims=True)
        pick = g_idx == jnp.min(jnp.where(cur == mx, g_idx, N_GROUPS), axis=0, keepdims=True)
        g_sel = jnp.where(pick, 1.0, g_sel)
        cur = jnp.where(pick, neg, cur)
    e_mask = jnp.broadcast_to(g_sel, (N_GROUPS, per, tm)).reshape(n_e, tm) > 0.0

    e_idx = lax.broadcasted_iota(i32, (n_e, tm), 0)
    cur = jnp.where(e_mask, biased, MASK_SCORE)
    ws = []
    for r in range(TOP_K):
        mx = jnp.max(cur, axis=0, keepdims=True)
        first = jnp.min(jnp.where(cur == mx, e_idx, n_e), axis=0, keepdims=True)
        pick = e_idx == first
        e_ref[r:r + 1, :] = first
        ws.append(jnp.sum(jnp.where(pick, scores, 0.0), axis=0, keepdims=True))
        cur = jnp.where(pick, neg, cur)
    total = ws[0]
    for r in range(1, TOP_K):
        total = total + ws[r]
    for r in range(TOP_K):
        wt_ref[r:r + 1, :] = ws[r] / total * ROUTE_SCALE


def moe_route(h, w_router_t, b_router):
    t, d = h.shape
    n_e = w_router_t.shape[0]
    tm = 512
    assert t % tm == 0
    out_spec = pl.BlockSpec((TOP_K, tm), lambda i: (0, i))
    return pl.pallas_call(
        _router_kernel,
        grid=(t // tm,),
        in_specs=[pl.BlockSpec((tm, d), lambda i: (i, 0)), pl.BlockSpec((n_e, d), lambda i: (0, 0)),
                  pl.BlockSpec((n_e, 1), lambda i: (0, 0))],
        out_specs=[out_spec, out_spec],
        out_shape=[jax.ShapeDtypeStruct((TOP_K, t), i32), jax.ShapeDtypeStruct((TOP_K, t), f32)],
        compiler_params=_params("parallel"),
        name="moe_route",
    )(h, w_router_t, b_router.reshape(n_e, 1).astype(f32))


def _row_copy(src_ref, dst_ref, src_row, dst_row, sem):
    return pltpu.make_async_copy(src_ref.at[pl.ds(src_row, 1)], dst_ref.at[pl.ds(dst_row, 1)], sem)


def _dispatch_kernel(n_used_ref, idx_ref, src_ref, o_ref, buf, sem):
    rows = buf.shape[0]

    @pl.when(pl.program_id(0) < n_used_ref[0])
    def _():
        def issue(r, carry):
            _row_copy(src_ref, buf, idx_ref[0, 0, r], r, sem).start()
            return carry

        def wait(r, carry):
            _row_copy(src_ref, buf, 0, r, sem).wait()
            return carry

        lax.fori_loop(0, rows, issue, 0)
        lax.fori_loop(0, rows, wait, 0)
        o_ref[...] = buf[...].astype(o_ref.dtype)

    @pl.when(pl.program_id(0) >= n_used_ref[0])
    def _():
        o_ref[...] = jnp.zeros_like(o_ref)


def moe_dispatch(h_rows, slot_tok, n_used):
    t, d = h_rows.shape
    tm = MOE_TILE
    n_blocks = slot_tok.shape[0] // tm
    return pl.pallas_call(
        _dispatch_kernel,
        grid_spec=pltpu.PrefetchScalarGridSpec(
            num_scalar_prefetch=1,
            grid=(n_blocks,),
            in_specs=[pl.BlockSpec((1, 1, tm), lambda i, nu: (i, 0, 0), memory_space=pltpu.SMEM),
                      pl.BlockSpec(memory_space=pl.ANY)],
            out_specs=pl.BlockSpec((tm, d), lambda i, nu: (i, 0)),
            scratch_shapes=[pltpu.VMEM((tm, d), h_rows.dtype), pltpu.SemaphoreType.DMA(())],
        ),
        out_shape=jax.ShapeDtypeStruct((n_blocks * tm, d), bf16),
        compiler_params=_params("arbitrary"),
        name="moe_dispatch",
    )(n_used, slot_tok.reshape(n_blocks, 1, tm), h_rows)


def _expert_kernel(be_ref, n_used_ref, x_ref, wg_ref, wu_ref, wd_ref, sw_ref, y_ref):
    @pl.when(pl.program_id(0) < n_used_ref[0])
    def _():
        x = x_ref[...]
        g = jnp.dot(x, wg_ref[0], preferred_element_type=f32)
        u = jnp.dot(x, wu_ref[0], preferred_element_type=f32)
        a = (_silu(g) * u).astype(bf16)
        y_ref[...] = jnp.dot(a, wd_ref[0], preferred_element_type=f32) * sw_ref[...]

    @pl.when(pl.program_id(0) >= n_used_ref[0])
    def _():
        y_ref[...] = jnp.zeros_like(y_ref)


def moe_experts(xs, w_gate, w_up, w_down, slot_w, block_e, n_used):
    n_slots, d = xs.shape
    tm = MOE_TILE
    d_e = w_gate.shape[-1]
    return pl.pallas_call(
        _expert_kernel,
        grid_spec=pltpu.PrefetchScalarGridSpec(
            num_scalar_prefetch=2,
            grid=(n_slots // tm,),
            in_specs=[pl.BlockSpec((tm, d), lambda i, be, nu: (i, 0)),
                      pl.BlockSpec((1, d, d_e), lambda i, be, nu: (be[i], 0, 0)),
                      pl.BlockSpec((1, d, d_e), lambda i, be, nu: (be[i], 0, 0)),
                      pl.BlockSpec((1, d_e, d), lambda i, be, nu: (be[i], 0, 0)),
                      pl.BlockSpec((tm, 1), lambda i, be, nu: (i, 0))],
            out_specs=pl.BlockSpec((tm, d), lambda i, be, nu: (i, 0)),
        ),
        out_shape=jax.ShapeDtypeStruct((n_slots, d), f32),
        compiler_params=_params("arbitrary"),
        name="moe_experts",
    )(block_e, n_used, xs, w_gate, w_up, w_down, slot_w.reshape(n_slots, 1))


def _combine_kernel(idx_ref, y_ref, o_ref, buf, sem):
    n_k, rows, _ = buf.shape

    def issue(r, carry):
        for kk in range(n_k):
            _row_copy(y_ref, buf.at[kk], idx_ref[0, kk, r], r, sem).start()
        return carry

    def wait(r, carry):
        for kk in range(n_k):
            _row_copy(y_ref, buf.at[kk], 0, r, sem).wait()
        return carry

    lax.fori_loop(0, rows, issue, 0)
    lax.fori_loop(0, rows, wait, 0)
    acc = buf[0]
    for kk in range(1, n_k):
        acc = acc + buf[kk]
    o_ref[...] = acc


def moe_combine(y, dest):
    n_k, t = dest.shape
    d = y.shape[1]
    rows = 128
    idx = dest.reshape(n_k, t // rows, rows).transpose(1, 0, 2)
    return pl.pallas_call(
        _combine_kernel,
        grid=(t // rows,),
        in_specs=[pl.BlockSpec((1, n_k, rows), lambda i: (i, 0, 0), memory_space=pltpu.SMEM),
                  pl.BlockSpec(memory_space=pl.ANY)],
        out_specs=pl.BlockSpec((rows, d), lambda i: (i, 0)),
        out_shape=jax.ShapeDtypeStruct((t, d), f32),
        scratch_shapes=[pltpu.VMEM((n_k, rows, d), f32), pltpu.SemaphoreType.DMA(())],
        compiler_params=_params("arbitrary"),
        name="moe_combine",
    )(idx, y)


def _moe_plan(top_e, top_w, n_experts):
    n_k, t = top_e.shape
    tm = MOE_TILE
    n_blocks = -(-(n_k * t) // tm) + n_experts
    onehot = (top_e[None, :, :] == jnp.arange(n_experts, dtype=i32)[:, None, None])
    member = jnp.sum(onehot.astype(i32), axis=1)
    rank = jnp.cumsum(member, axis=1)
    counts = rank[:, -1]
    padded = (counts + tm - 1) // tm * tm
    pad_end = jnp.cumsum(padded)
    dest_dense = (pad_end - padded)[:, None] + rank - 1
    dest = jnp.take_along_axis(dest_dense, top_e, axis=0)
    tok = jnp.broadcast_to(jnp.arange(t, dtype=i32)[None, :], (n_k, t))
    flat = dest.reshape(-1)
    slot_tok = jnp.zeros((n_blocks * tm,), i32).at[flat].set(tok.reshape(-1), unique_indices=True)
    slot_w = jnp.zeros((n_blocks * tm,), f32).at[flat].set(top_w.reshape(-1), unique_indices=True)
    n_used = (pad_end[-1] // tm).astype(i32)
    blk = jnp.minimum(jnp.arange(n_blocks, dtype=i32), n_used - 1)
    block_e = jnp.minimum(jnp.searchsorted(pad_end, blk * tm, side='right'), n_experts - 1).astype(i32)
    return dest.astype(i32), slot_tok, slot_w, block_e, n_used.reshape(1)


def _rope_tables(n_ctx, n_lat):
    t = jnp.arange(n_lat)
    pos = jnp.stack([t // GRID_W, t % GRID_W], axis=-1).astype(f32)
    n_freq = HEAD_DIM // 4
    inv_freq = ROPE_THETA ** (-jnp.arange(n_freq, dtype=f32) / n_freq)
    ang = pos[:, :, None] * inv_freq
    cos, sin = jnp.cos(ang), jnp.sin(ang)
    cos_t = jnp.concatenate([cos[:, 0], cos[:, 0], cos[:, 1], cos[:, 1]], axis=-1)
    sin_t = jnp.concatenate([-sin[:, 0], sin[:, 0], -sin[:, 1], sin[:, 1]], axis=-1)
    cos_t = jnp.concatenate([jnp.ones((n_ctx, HEAD_DIM), f32), cos_t], axis=0)
    sin_t = jnp.concatenate([jnp.zeros((n_ctx, HEAD_DIM), f32), sin_t], axis=0)
    return cos_t, sin_t


def kernel(x, c, ctx, c_ctx, w_ada, b_ada, norm_mix, norm_ffn, w_in, q_norm_a, k_norm_a, hgrn_lb_logits, hgrn_out_norm, q_norm_c, k_norm_c, lambda_q1, lambda_k1, lambda_q2, lambda_k2, diff_sub_norm, w_branch, w_out, w_router, b_router, w_exp_gate, w_exp_up, w_exp_down, w_sh_gate, w_sh_up, w_sh_down):
    n_batch, n_lat, d = x.shape
    n_ctx = ctx.shape[1]
    depth = w_ada.shape[0]
    rows_b = n_ctx + n_lat
    t = n_batch * rows_b
    branch_w = d // 2
    a_heads = branch_w // HEAD_DIM
    a_kv = a_heads // 4
    b_heads = branch_w // HEAD_DIM
    c_heads = branch_w // (2 * HEAD_DIM)
    n_experts = w_router.shape[-1]
    tm = MM_TILE_M if rows_b % MM_TILE_M == 0 else ROW_TILE
    tiles_per_batch = rows_b // tm
    assert n_ctx % ROW_TILE == 0 and n_lat % ROW_TILE == 0 and rows_b % SCAN_TILE == 0

    cos_t, sin_t = _rope_tables(n_ctx, n_lat)
    p_lb = jax.nn.softmax(hgrn_lb_logits.astype(f32), axis=0)
    lower_bounds = jnp.cumsum(p_lb, axis=0) - p_lb[0]
    cond = jnp.zeros((8, d), f32).at[:n_batch].set(c).at[n_batch].set(c_ctx)

    xa = jnp.concatenate([ctx, x], axis=1).reshape(t, d)

    def mod_specs(tn):
        ctx_spec = pl.BlockSpec((1, 1, tn), lambda i, j: (n_batch, 0, j))
        b_spec = pl.BlockSpec((1, 1, tn), lambda i, j: (i // tiles_per_batch, 0, j))
        return ctx_spec, b_spec

    splits = (a_heads * HEAD_DIM, a_kv * HEAD_DIM, a_kv * HEAD_DIM,
              branch_w, branch_w, branch_w, branch_w, branch_w,
              branch_w, branch_w, branch_w, 3 * d)
    offs = [0]
    for s in splits:
        offs.append(offs[-1] + s)

    def cols(w, *ids):
        return jnp.concatenate([w[:, offs[i]:offs[i + 1]] for i in ids], axis=1).astype(bf16)

    n_qk_heads = a_heads + a_kv + 4 * c_heads
    qk_tn = (n_qk_heads // 2) * HEAD_DIM if n_qk_heads % 2 == 0 else n_qk_heads * HEAD_DIM
    sm_scale = HEAD_DIM ** -0.5 * LOG2E

    for l in range(depth):
        lam_init = 0.8 - 0.6 * math.exp(-0.3 * l)
        mod = fused_matmul([(cond, w_ada[l])], [(b_ada[l].reshape(1, -1), pl.BlockSpec((1, 1024), lambda i, j: (0, j)))],
                           _ep_bias, jax.ShapeDtypeStruct((8, 6 * d), f32), _tile_spec(8, 1024), 8, 1024,
                           a_fn=_silu, name="ada_ln")
        mod = mod[:n_batch + 1].reshape(n_batch + 1, 6, 1, d)
        mods = [mod[:, k] for k in range(6)]

        (h,) = norm_modulate(xa, norm_mix[l], mods[0], mods[1], rows_b, n_ctx, [bf16])
        w_l = w_in[l]
        gains = jnp.concatenate([jnp.tile(q_norm_a[l] * sm_scale, a_heads), jnp.tile(k_norm_a[l], a_kv),
                                 jnp.tile(q_norm_c[l] * sm_scale, 2 * c_heads), jnp.tile(k_norm_c[l], 2 * c_heads)])
        qk = fused_matmul(
            [(h, cols(w_l, 0, 1, 8, 9))],
            [(gains.reshape(1, -1).astype(f32), pl.BlockSpec((1, qk_tn), lambda i, j: (0, j))),
             (cos_t, pl.BlockSpec((tm, HEAD_DIM), lambda i, j: (i % tiles_per_batch, 0))),
             (sin_t, pl.BlockSpec((tm, HEAD_DIM), lambda i, j: (i % tiles_per_batch, 0)))],
            _ep_qk, jax.ShapeDtypeStruct((n_qk_heads, t, HEAD_DIM), bf16),
            pl.BlockSpec((qk_tn // HEAD_DIM, tm, HEAD_DIM), lambda i, j: (j, i, 0)), tm, qk_tn, name="in_proj_qk")
        va = fused_matmul([(h, cols(w_l, 2))], [], _ep_store(bf16), jax.ShapeDtypeStruct((t, a_kv * HEAD_DIM), bf16),
                          _tile_spec(tm, a_kv * HEAD_DIM), tm, a_kv * HEAD_DIM, name="in_proj_va")
        vc = fused_matmul([(h, cols(w_l, 10))], [], _ep_store(bf16), jax.ShapeDtypeStruct((t, branch_w), bf16),
                          _tile_spec(tm, 512), tm, 512, name="in_proj_vc")
        qio = fused_matmul([(h, cols(w_l, 3, 4, 7))], [], _ep_store(bf16), jax.ShapeDtypeStruct((t, 3 * branch_w), bf16),
                           _tile_spec(tm, 512), tm, 512, name="in_proj_hgrn")
        zfb = fused_matmul([(h, cols(w_l, 5, 6))], [], _ep_store(f32), jax.ShapeDtypeStruct((t, 2 * branch_w), f32),
                           _tile_spec(tm, 512), tm, 512, name="in_proj_forget")
        gates = fused_matmul([(h, cols(w_l, 11))], [], _ep_sigmoid, jax.ShapeDtypeStruct((t, 3 * d), bf16),
                             _tile_spec(tm, 512), tm, 512, name="in_proj_gates")

        dims = dict(n_batch=n_batch, rows_per_batch=rows_b, n_ctx=n_ctx)
        o_a = attention_a(qk, va, None, a_heads=a_heads, a_kv_heads=a_kv, ctx_pass=False, **dims)
        o_a = attention_a(qk, va, o_a, a_heads=a_heads, a_kv_heads=a_kv, ctx_pass=True, **dims)
        lam_params = jnp.stack([lambda_q1[l], lambda_k1[l], lambda_q2[l], lambda_k2[l]]).astype(f32)
        c_args = dict(head0_q=a_heads + a_kv, head0_k=a_heads + a_kv + 2 * c_heads, c_heads=c_heads,
                      lam_init=lam_init, **dims)
        o_c = attention_c(qk, vc, lam_params, diff_sub_norm[l], None, ctx_pass=False, **c_args)
        o_c = attention_c(qk, vc, lam_params, diff_sub_norm[l], o_c, ctx_pass=True, **c_args)
        o_f, o_r = hgrn_scan(qio, zfb, lower_bounds[l], n_batch=n_batch, rows_per_batch=rows_b, b_heads=b_heads)
        o_b = hgrn_readout(o_f, o_r, qio, hgrn_out_norm[l], b_heads)

        wb = w_branch[l].astype(bf16)
        n_col = d // 512
        y = fused_matmul(
            [(o_a, wb[0]), (o_b, wb[1]), (o_c, wb[2])],
            [(gates, pl.BlockSpec((tm, 512), lambda i, j, r=r: (i, r * n_col + j))) for r in range(3)],
            _ep_merge, jax.ShapeDtypeStruct((t, d), bf16), _tile_spec(tm, 512), tm, 512, name="merge")
        ctx_spec, b_spec = mod_specs(512)
        xa = fused_matmul(
            [(y, w_out[l].astype(bf16))],
            [(xa, _tile_spec(tm, 512)), (mods[2], ctx_spec), (mods[2], b_spec)],
            _make_ep_residual(tm, rows_b, n_ctx, False), jax.ShapeDtypeStruct((t, d), f32),
            _tile_spec(tm, 512), tm, 512, name="out_proj")

        h2, h2_rows = norm_modulate(xa, norm_ffn[l], mods[3], mods[4], rows_b, n_ctx, [bf16, f32])
        top_e, top_w = moe_route(h2, w_router[l].T.astype(bf16), b_router[l])
        dest, slot_tok, slot_w, block_e, n_used = _moe_plan(top_e, top_w, n_experts)
        xs = moe_dispatch(h2_rows, slot_tok, n_used)
        ys = moe_experts(xs, w_exp_gate[l].astype(bf16), w_exp_up[l].astype(bf16), w_exp_down[l].astype(bf16),
                         slot_w, block_e, n_used)
        routed = moe_combine(ys, dest)
        d_sh = w_sh_gate.shape[-1]
        act = fused_matmul([(h2, w_sh_gate[l].astype(bf16)), (h2, w_sh_up[l].astype(bf16))], [], _ep_glu,
                           jax.ShapeDtypeStruct((t, d_sh), bf16), _tile_spec(tm, d_sh), tm, d_sh, name="shared_glu")
        xa = fused_matmul(
            [(act, w_sh_down[l].astype(bf16))],
            [(xa, _tile_spec(tm, 512)), (mods[5], ctx_spec), (mods[5], b_spec), (routed, _tile_spec(tm, 512))],
            _make_ep_residual(tm, rows_b, n_ctx, True), jax.ShapeDtypeStruct((t, d), f32),
            _tile_spec(tm, 512), tm, 512, name="moe_out")

    return xa.reshape(n_batch, rows_b, d)[:, n_ctx:]
```

```python
import functools
import math

import jax
import jax.numpy as jnp
from jax import lax
from jax.experimental import pallas as pl
from jax.experimental.pallas import tpu as pltpu

f32 = jnp.float32
bf16 = jnp.bfloat16
i32 = jnp.int32

HEAD_DIM = 128
GRID_W = 64
ROPE_THETA = 10000.0
EPS = 1e-6
F_MIN = 1e-6
SCAN_CHUNK = 64
SCAN_SUB = 8
N_GROUPS = 8
TOPK_GROUPS = 4
TOP_K = 8
ROUTE_SCALE = 2.5
MASK_SCORE = -1e9
LOG2E = 1.4426950408889634

VMEM_LIMIT_BYTES = 48 * 1024 * 1024
ROW_TILE = 256
MM_TILE_M = 768
MOE_TILE = 256
SCAN_TILE = 256


def _params(*semantics):
    return pltpu.CompilerParams(dimension_semantics=semantics, vmem_limit_bytes=VMEM_LIMIT_BYTES)


def _silu(x):
    return x * jax.nn.sigmoid(x)


def _nt_dot(a, b):
    return lax.dot_general(a, b, (((1,), (1,)), ((), ())), preferred_element_type=f32)


def _tn_dot(a, b):
    return lax.dot_general(a, b, (((0,), (0,)), ((), ())), preferred_element_type=f32)


def _row_is_ctx(tile_index, tile_rows, rows_per_batch, n_ctx):
    row0 = (tile_index % (rows_per_batch // tile_rows)) * tile_rows
    rid = row0 + lax.broadcasted_iota(i32, (tile_rows, 1), 0)
    return rid < n_ctx


def _fm_kernel(*refs, n_pairs, n_extra, a_fn, epilogue):
    pair_refs = refs[:2 * n_pairs]
    extra = refs[2 * n_pairs:2 * n_pairs + n_extra]
    outs = refs[2 * n_pairs + n_extra:]
    accs = []
    for p in range(n_pairs):
        a = pair_refs[2 * p][...]
        if a_fn is not None:
            a = a_fn(a)
        accs.append(jnp.dot(a.astype(bf16), pair_refs[2 * p + 1][...].astype(bf16),
                            preferred_element_type=f32))
    epilogue(accs, extra, outs)


def fused_matmul(pairs, extras, epilogue, out_shapes, out_specs, tm, tn, a_fn=None, name=None):
    m = pairs[0][0].shape[0]
    n = pairs[0][1].shape[1]
    assert m % tm == 0 and n % tn == 0, (m, tm, n, tn)
    in_specs, args = [], []
    for a, w in pairs:
        k = a.shape[1]
        assert w.shape[0] == k
        in_specs += [pl.BlockSpec((tm, k), lambda i, j: (i, 0)), pl.BlockSpec((k, tn), lambda i, j: (0, j))]
        args += [a, w]
    for arr, spec in extras:
        in_specs.append(spec)
        args.append(arr)
    return pl.pallas_call(
        functools.partial(_fm_kernel, n_pairs=len(pairs), n_extra=len(extras), a_fn=a_fn, epilogue=epilogue),
        grid=(m // tm, n // tn),
        in_specs=in_specs,
        out_specs=out_specs,
        out_shape=out_shapes,
        compiler_params=_params("parallel", "arbitrary"),
        name=name,
    )(*args)


def _tile_spec(tm, tn):
    return pl.BlockSpec((tm, tn), lambda i, j: (i, j))


def _ep_store(dtype):
    def ep(accs, extra, outs):
        outs[0][...] = accs[0].astype(dtype)
    return ep


def _ep_bias(accs, extra, outs):
    outs[0][...] = accs[0] + extra[0][...]


def _ep_sigmoid(accs, extra, outs):
    outs[0][...] = jax.nn.sigmoid(accs[0]).astype(outs[0].dtype)


def _ep_qk(accs, extra, outs):
    gain_ref, cos_ref, sin_ref = extra
    acc = accs[0]
    cos = cos_ref[...]
    sin = sin_ref[...]
    lane = lax.broadcasted_iota(i32, cos.shape, 1)
    first_half = (lane % (HEAD_DIM // 2)) < (HEAD_DIM // 4)
    for c in range(acc.shape[1] // HEAD_DIM):
        x = acc[:, c * HEAD_DIM:(c + 1) * HEAD_DIM]
        y = x * lax.rsqrt(jnp.mean(x * x, axis=-1, keepdims=True) + EPS)
        y = y * gain_ref[:, c * HEAD_DIM:(c + 1) * HEAD_DIM]
        partner = jnp.where(first_half,
                            pltpu.roll(y, HEAD_DIM - HEAD_DIM // 4, 1),
                            pltpu.roll(y, HEAD_DIM // 4, 1))
        outs[0][c] = (y * cos + partner * sin).astype(outs[0].dtype)


def _ep_merge(accs, extra, outs):
    y = extra[0][...].astype(f32) * accs[0]
    y = y + extra[1][...].astype(f32) * accs[1]
    y = y + extra[2][...].astype(f32) * accs[2]
    outs[0][...] = y.astype(outs[0].dtype)


def _ep_glu(accs, extra, outs):
    outs[0][...] = (_silu(accs[0]) * accs[1]).astype(outs[0].dtype)


def _make_ep_residual(tm, rows_per_batch, n_ctx, with_routed):
    def ep(accs, extra, outs):
        x_ref, mod_ctx_ref, mod_b_ref = extra[:3]
        y = accs[0]
        if with_routed:
            y = y + extra[3][...]
        is_ctx = _row_is_ctx(pl.program_id(0), tm, rows_per_batch, n_ctx)
        mod = jnp.where(is_ctx, mod_ctx_ref[0], mod_b_ref[0])
        outs[0][...] = x_ref[...] + mod * y
    return ep


def _norm_kernel(x_ref, w_ref, shc_ref, shb_ref, scc_ref, scb_ref, *outs, tile, rows_per_batch, n_ctx):
    x = x_ref[...]
    y = x * lax.rsqrt(jnp.mean(x * x, axis=-1, keepdims=True) + EPS) * w_ref[...]
    is_ctx = _row_is_ctx(pl.program_id(0), tile, rows_per_batch, n_ctx)
    shift = jnp.where(is_ctx, shc_ref[0], shb_ref[0])
    scale = jnp.where(is_ctx, scc_ref[0], scb_ref[0])
    h = y * (1.0 + scale) + shift
    for o in outs:
        o[...] = h.astype(o.dtype)


def norm_modulate(x, w, shift, scale, rows_per_batch, n_ctx, out_dtypes):
    t, d = x.shape
    tile = ROW_TILE
    n_b = shift.shape[0] - 1
    tiles_per_batch = rows_per_batch // tile
    ctx_spec = pl.BlockSpec((1, 1, d), lambda i: (n_b, 0, 0))
    b_spec = pl.BlockSpec((1, 1, d), lambda i: (i // tiles_per_batch, 0, 0))
    row_spec = pl.BlockSpec((tile, d), lambda i: (i, 0))
    return pl.pallas_call(
        functools.partial(_norm_kernel, tile=tile, rows_per_batch=rows_per_batch, n_ctx=n_ctx),
        grid=(t // tile,),
        in_specs=[row_spec, pl.BlockSpec((1, d), lambda i: (0, 0)), ctx_spec, b_spec, ctx_spec, b_spec],
        out_specs=[row_spec for _ in out_dtypes],
        out_shape=[jax.ShapeDtypeStruct((t, d), dt) for dt in out_dtypes],
        compiler_params=_params("parallel"),
        name="norm_modulate",
    )(x, w.reshape(1, d), shift, shift, scale, scale)


def _attend(q, k_at, v_at, kv_len, tk, dv):
    m = l = acc = None
    for c in range(kv_len // tk):
        s = _nt_dot(q, k_at(c * tk))
        s_max = jnp.max(s, axis=1, keepdims=True)
        if c == 0:
            m = s_max
            p = jnp.exp2(s - m)
            l = jnp.sum(p, axis=1, keepdims=True)
            acc = jnp.dot(p.astype(bf16), v_at(c * tk), preferred_element_type=f32)
        else:
            m_new = jnp.maximum(m, s_max)
            alpha = jnp.exp2(m - m_new)
            p = jnp.exp2(s - m_new)
            l = alpha * l + jnp.sum(p, axis=1, keepdims=True)
            acc = alpha * acc + jnp.dot(p.astype(bf16), v_at(c * tk), preferred_element_type=f32)
            m = m_new
    return acc / l


def _kv_chunk(kv_len):
    for tk in (768, 1408, 512, 256, 128):
        if kv_len % tk == 0:
            return tk
    raise ValueError(kv_len)


def _ctx_or_all(n_ctx_tiles, n_ctx, kv_all, run):
    is_ctx = pl.program_id(2) < n_ctx_tiles

    @pl.when(is_ctx)
    def _():
        run(n_ctx)

    @pl.when(jnp.logical_not(is_ctx))
    def _():
        run(kv_all)


def _attn_a_kernel(q_ref, k_ref, v_ref, o_ref, *, n_ctx):
    g, tq, d = q_ref.shape

    def run(kv_len):
        tk = _kv_chunk(kv_len)
        q = q_ref[...].reshape(g * tq, d)
        o = _attend(q, lambda off: k_ref[0, off:off + tk, :], lambda off: v_ref[off:off + tk, :], kv_len, tk, d)
        for h in range(g):
            o_ref[:, h * d:(h + 1) * d] = o[h * tq:(h + 1) * tq].astype(o_ref.dtype)

    _ctx_or_all(n_ctx // tq, n_ctx, k_ref.shape[1], run)


def _attn_c_kernel(lam_ref, dn_ref, q_ref, k_ref, v_ref, o_ref, *, n_ctx, lam_init):
    tq = q_ref.shape[1]
    dv = v_ref.shape[-1]

    def run(kv_len):
        tk = _kv_chunk(kv_len)
        o1 = _attend(q_ref[0], lambda off: k_ref[0, off:off + tk, :], lambda off: v_ref[off:off + tk, :],
                     kv_len, tk, dv)
        o2 = _attend(q_ref[1], lambda off: k_ref[1, off:off + tk, :], lambda off: v_ref[off:off + tk, :],
                     kv_len, tk, dv)
        lam_p = lam_ref[...]
        lam = (jnp.exp(jnp.sum(lam_p[0:1] * lam_p[1:2], axis=-1, keepdims=True))
               - jnp.exp(jnp.sum(lam_p[2:3] * lam_p[3:4], axis=-1, keepdims=True)) + lam_init)
        o = o1 - lam * o2
        y = o * lax.rsqrt(jnp.mean(o * o, axis=-1, keepdims=True) + EPS) * dn_ref[...]
        o_ref[...] = (y * (1.0 - lam_init)).astype(o_ref.dtype)

    _ctx_or_all(n_ctx // tq, n_ctx, k_ref.shape[1], run)


def attention_a(qk, va, *, n_batch, rows_per_batch, n_ctx, a_heads, a_kv_heads):
    t = qk.shape[1]
    group = a_heads // a_kv_heads
    tq = ROW_TILE
    n_q = rows_per_batch // tq
    return pl.pallas_call(
        functools.partial(_attn_a_kernel, n_ctx=n_ctx),
        grid=(n_batch, a_kv_heads, n_q),
        in_specs=[
            pl.BlockSpec((group, tq, HEAD_DIM), lambda b, h, i: (h, b * n_q + i, 0)),
            pl.BlockSpec((1, rows_per_batch, HEAD_DIM), lambda b, h, i: (a_heads + h, b, 0)),
            pl.BlockSpec((rows_per_batch, HEAD_DIM), lambda b, h, i: (b, h)),
        ],
        out_specs=pl.BlockSpec((tq, group * HEAD_DIM), lambda b, h, i: (b * n_q + i, h)),
        out_shape=jax.ShapeDtypeStruct((t, a_heads * HEAD_DIM), bf16),
        compiler_params=_params("parallel", "parallel", "arbitrary"),
        name="attention_a",
    )(qk, qk, va)


def attention_c(qk, vc, lam_params, diff_norm, *, n_batch, rows_per_batch, n_ctx, head0_q, head0_k,
                c_heads, lam_init):
    t = qk.shape[1]
    dv = 2 * HEAD_DIM
    tq = ROW_TILE
    n_q = rows_per_batch // tq
    return pl.pallas_call(
        functools.partial(_attn_c_kernel, n_ctx=n_ctx, lam_init=lam_init),
        grid=(n_batch, c_heads, n_q),
        in_specs=[
            pl.BlockSpec((4, HEAD_DIM), lambda b, h, i: (0, 0)),
            pl.BlockSpec((1, dv), lambda b, h, i: (0, 0)),
            pl.BlockSpec((2, tq, HEAD_DIM), lambda b, h, i: (head0_q // 2 + h, b * n_q + i, 0)),
            pl.BlockSpec((2, rows_per_batch, HEAD_DIM), lambda b, h, i: (head0_k // 2 + h, b, 0)),
            pl.BlockSpec((rows_per_batch, dv), lambda b, h, i: (b, h)),
        ],
        out_specs=pl.BlockSpec((tq, dv), lambda b, h, i: (b * n_q + i, h)),
        out_shape=jax.ShapeDtypeStruct((t, c_heads * dv), bf16),
        compiler_params=_params("parallel", "parallel", "arbitrary"),
        name="attention_c",
    )(lam_params, diff_norm.reshape(1, dv), qk, qk, vc)


def _hgrn_chunk(qp, v, z, lb, st, rev):
    c = SCAN_CHUNK
    nb = c // SCAN_SUB
    q = _silu(qp.astype(f32))
    one_m = 1.0 - lb
    g = jnp.log(jnp.maximum(lb + one_m * jax.nn.sigmoid(z), F_MIN))
    k = one_m * jax.nn.sigmoid(-z)
    vf = v.astype(f32)

    r_i = lax.broadcasted_iota(i32, (c, c), 0)
    c_i = lax.broadcasted_iota(i32, (c, c), 1)
    tri = (c_i >= r_i) if rev else (c_i <= r_i)
    b = jnp.dot(tri.astype(f32), g, precision=lax.Precision.HIGHEST, preferred_element_type=f32)
    end = 0 if rev else c - 1
    b_end = b[end:end + 1, :]

    o = _nt_dot((q * jnp.exp(b)).astype(bf16), st.astype(bf16))
    k_hat = (k * jnp.exp(b_end - b)).astype(bf16)
    st_new = st * jnp.exp(b_end) + _tn_dot(v, k_hat)

    def edge_row(j):
        return j * SCAN_SUB if rev else j * SCAN_SUB + SCAN_SUB - 1

    row = lax.broadcasted_iota(i32, (c, HEAD_DIM), 0)
    rho_own = jnp.concatenate(
        [jnp.broadcast_to(b[edge_row(j):edge_row(j) + 1, :], (SCAN_SUB, HEAD_DIM)) for j in range(nb)], axis=0)
    k_til = k * jnp.exp(rho_own - b)
    qs, ks = [], []
    for j in (range(1, nb) if rev else range(nb - 1)):
        rho = b[edge_row(j):edge_row(j) + 1, :]
        q_rows = (row < j * SCAN_SUB) if rev else (row >= (j + 1) * SCAN_SUB)
        qs.append((q * jnp.exp(jnp.where(q_rows, b - rho, -jnp.inf))).astype(bf16))
        k_rows = (row >= j * SCAN_SUB) & (row < (j + 1) * SCAN_SUB)
        ks.append(jnp.where(k_rows, k_til, 0.0).astype(bf16))
    a_off = _nt_dot(jnp.concatenate(qs, axis=1), jnp.concatenate(ks, axis=1))
    o = o + jnp.dot(a_off.astype(bf16), v, preferred_element_type=f32)

    sub_r = lax.broadcasted_iota(i32, (SCAN_SUB, HEAD_DIM), 0)
    o_diag = []
    for blk in range(nb):
        sl = slice(blk * SCAN_SUB, (blk + 1) * SCAN_SUB)
        b_i, q_i, k_i, v_i = b[sl], q[sl], k[sl], vf[sl]
        acc = jnp.zeros((SCAN_SUB, HEAD_DIM), f32)
        for s in range(SCAN_SUB):
            t_rows = (sub_r <= s) if rev else (sub_r >= s)
            e = jnp.exp(jnp.where(t_rows, b_i - b_i[s:s + 1], -jnp.inf))
            p = q_i * e * k_i[s:s + 1]
            acc = acc + jnp.sum(p, axis=1, keepdims=True) * v_i[s:s + 1]
        o_diag.append(acc)
    return o + jnp.concatenate(o_diag, axis=0), st_new


def _hgrn_kernel(lb_ref, qf_ref, vf_ref, zf_ref, qb_ref, vb_ref, zb_ref, of_ref, ob_ref, sf_ref, sb_ref):
    @pl.when(pl.program_id(2) == 0)
    def _():
        sf_ref[...] = jnp.zeros_like(sf_ref)
        sb_ref[...] = jnp.zeros_like(sb_ref)

    lb = lb_ref[...]
    n_chunks = qf_ref.shape[0] // SCAN_CHUNK

    st_f = sf_ref[...]
    st_b = sb_ref[...]
    for c in range(n_chunks):
        rf = slice(c * SCAN_CHUNK, (c + 1) * SCAN_CHUNK)
        o, st_f = _hgrn_chunk(qf_ref[rf, :], vf_ref[rf, :], zf_ref[rf, :], lb, st_f, False)
        of_ref[rf, :] = o
        rb = slice((n_chunks - 1 - c) * SCAN_CHUNK, (n_chunks - c) * SCAN_CHUNK)
        o, st_b = _hgrn_chunk(qb_ref[rb, :], vb_ref[rb, :], zb_ref[rb, :], lb, st_b, True)
        ob_ref[rb, :] = o
    sf_ref[...] = st_f
    sb_ref[...] = st_b


def hgrn_scan(qio, zfb, lower_bound, *, n_batch, rows_per_batch, b_heads):
    t = qio.shape[0]
    tile = SCAN_TILE
    n_tiles = rows_per_batch // tile

    def fwd(b, h, j):
        return b * n_tiles + j

    def bwd(b, h, j):
        return b * n_tiles + jnp.where(j == 0, 0, n_tiles - j)

    def spec(tile_fn, col0):
        return pl.BlockSpec((tile, HEAD_DIM), lambda b, h, j: (tile_fn(b, h, j), col0 + h))

    return pl.pallas_call(
        _hgrn_kernel,
        grid=(n_batch, b_heads, n_tiles),
        in_specs=[
            pl.BlockSpec((1, HEAD_DIM), lambda b, h, j: (0, h)),
            spec(fwd, 0), spec(fwd, b_heads), spec(fwd, 0),
            spec(bwd, 0), spec(bwd, b_heads), spec(bwd, b_heads),
        ],
        out_specs=[spec(fwd, 0), spec(bwd, 0)],
        out_shape=[jax.ShapeDtypeStruct((t, b_heads * HEAD_DIM), f32)] * 2,
        scratch_shapes=[pltpu.VMEM((HEAD_DIM, HEAD_DIM), f32)] * 2,
        compiler_params=_params("parallel", "parallel", "arbitrary"),
        name="hgrn_scan",
    )(lower_bound.reshape(1, -1), qio, qio, zfb, qio, qio, zfb)


def _hgrn_out_kernel(of_ref, ob_ref, og_ref, w_ref, o_ref):
    o = of_ref[...] + ob_ref[...]
    w = w_ref[...]
    for c in range(o.shape[1] // HEAD_DIM):
        sl = slice(c * HEAD_DIM, (c + 1) * HEAD_DIM)
        x = o[:, sl]
        y = x * lax.rsqrt(jnp.mean(x * x, axis=-1, keepdims=True) + EPS) * w
        o_ref[:, sl] = (y * _silu(og_ref[:, sl].astype(f32))).astype(o_ref.dtype)


def hgrn_readout(o_f, o_b, qio, hgrn_norm, b_heads):
    t, w = o_f.shape
    tile = ROW_TILE
    row_spec = pl.BlockSpec((tile, w), lambda i: (i, 0))
    return pl.pallas_call(
        _hgrn_out_kernel,
        grid=(t // tile,),
        in_specs=[row_spec, row_spec, pl.BlockSpec((tile, w), lambda i: (i, 2)),
                  pl.BlockSpec((1, HEAD_DIM), lambda i: (0, 0))],
        out_specs=row_spec,
        out_shape=jax.ShapeDtypeStruct((t, w), bf16),
        compiler_params=_params("parallel"),
        name="hgrn_readout",
    )(o_f, o_b, qio, hgrn_norm.reshape(1, HEAD_DIM))


def _router_kernel(h_ref, w_ref, b_ref, e_ref, wt_ref, rank_ref, cnt_ref, run_ref):
    n_e = w_ref.shape[0]
    tm = h_ref.shape[0]
    per = n_e // N_GROUPS

    @pl.when(pl.program_id(0) == 0)
    def _():
        run_ref[...] = jnp.zeros_like(run_ref)

    scores = jax.nn.sigmoid(_nt_dot(w_ref[...], h_ref[...]))
    biased = scores + b_ref[...]
    neg = -jnp.inf

    grp = biased.reshape(N_GROUPS, per, tm)
    idx_in = lax.broadcasted_iota(i32, grp.shape, 1)
    m1 = jnp.max(grp, axis=1, keepdims=True)
    first = jnp.min(jnp.where(grp == m1, idx_in, per), axis=1, keepdims=True)
    m2 = jnp.max(jnp.where(idx_in == first, neg, grp), axis=1, keepdims=True)
    cur = m1 + m2

    g_idx = lax.broadcasted_iota(i32, cur.shape, 0)
    g_sel = jnp.zeros(cur.shape, f32)
    for _ in range(TOPK_GROUPS):
        mx = jnp.max(cur, axis=0, keepdims=True)
        pick = g_idx == jnp.min(jnp.where(cur == mx, g_idx, N_GROUPS), axis=0, keepdims=True)
        g_sel = jnp.where(pick, 1.0, g_sel)
        cur = jnp.where(pick, neg, cur)
    e_mask = jnp.broadcast_to(g_sel, (N_GROUPS, per, tm)).reshape(n_e, tm) > 0.0

    e_idx = lax.broadcasted_iota(i32, (n_e, tm), 0)
    cur = jnp.where(e_mask, biased, MASK_SCORE)
    ws, picks = [], []
    member = jnp.zeros((n_e, tm), f32)
    for r in range(TOP_K):
        mx = jnp.max(cur, axis=0, keepdims=True)
        first = jnp.min(jnp.where(cur == mx, e_idx, n_e), axis=0, keepdims=True)
        pick = e_idx == first
        e_ref[r:r + 1, :] = first
        ws.append(jnp.sum(jnp.where(pick, scores, 0.0), axis=0, keepdims=True))
        cur = jnp.where(pick, neg, cur)
        member = jnp.where(pick, 1.0, member)
        picks.append(pick)
    total = ws[0]
    for r in range(1, TOP_K):
        total = total + ws[r]
    for r in range(TOP_K):
        wt_ref[r:r + 1, :] = ws[r] / total * ROUTE_SCALE

    u_i = lax.broadcasted_iota(i32, (tm, tm), 0)
    t_i = lax.broadcasted_iota(i32, (tm, tm), 1)
    prefix = jnp.dot(member.astype(bf16), (u_i <= t_i).astype(bf16), preferred_element_type=f32)
    rank = run_ref[...] + prefix
    for r in range(TOP_K):
        rank_ref[r:r + 1, :] = jnp.sum(jnp.where(picks[r], rank, 0.0), axis=0, keepdims=True).astype(i32)
    run_ref[...] = run_ref[...] + jnp.sum(member, axis=1, keepdims=True)
    cnt_ref[...] = run_ref[...].astype(i32)


def moe_route(h, w_router_t, b_router):
    t, d = h.shape
    n_e = w_router_t.shape[0]
    tm = 512
    assert t % tm == 0
    out_spec = pl.BlockSpec((TOP_K, tm), lambda i: (0, i))
    return pl.pallas_call(
        _router_kernel,
        grid=(t // tm,),
        in_specs=[pl.BlockSpec((tm, d), lambda i: (i, 0)), pl.BlockSpec((n_e, d), lambda i: (0, 0)),
                  pl.BlockSpec((n_e, 1), lambda i: (0, 0))],
        out_specs=[out_spec, out_spec, out_spec, pl.BlockSpec((n_e, 1), lambda i: (0, 0))],
        out_shape=[jax.ShapeDtypeStruct((TOP_K, t), i32), jax.ShapeDtypeStruct((TOP_K, t), f32),
                   jax.ShapeDtypeStruct((TOP_K, t), i32), jax.ShapeDtypeStruct((n_e, 1), i32)],
        scratch_shapes=[pltpu.VMEM((n_e, 1), f32)],
        compiler_params=_params("arbitrary"),
        name="moe_route",
    )(h, w_router_t, b_router.reshape(n_e, 1).astype(f32))


def _row_copy(src_ref, dst_ref, src_row, dst_row, sem):
    return pltpu.make_async_copy(src_ref.at[pl.ds(src_row, 1)], dst_ref.at[pl.ds(dst_row, 1)], sem)


def _expert_kernel(be_ref, idx0_ref, idx_next_ref, h_ref, wg_ref, wu_ref, wd_ref, y_ref,
                   xbuf, sem, wg_b, wu_b, wd_b):
    i = pl.program_id(0)
    last = pl.num_programs(0) - 1
    tm = xbuf.shape[1]
    slot = i % 2

    def gather_start(idx_ref, dst_slot):
        for r in range(tm):
            _row_copy(h_ref, xbuf.at[dst_slot], idx_ref[0, 0, r], r, sem.at[dst_slot]).start()

    def gather_wait(dst_slot):
        pltpu.make_async_copy(h_ref.at[pl.ds(0, tm)], xbuf.at[dst_slot], sem.at[dst_slot]).wait()

    @pl.when(i == 0)
    def _():
        gather_start(idx0_ref, 0)

    @pl.when((i == 0) | (be_ref[i] != be_ref[jnp.maximum(i - 1, 0)]))
    def _():
        wg_b[...] = wg_ref[0].astype(bf16)
        wu_b[...] = wu_ref[0].astype(bf16)
        wd_b[...] = wd_ref[0].astype(bf16)

    gather_wait(slot)
    gather_start(idx_next_ref, 1 - slot)
    x = xbuf[slot].astype(bf16)
    g = jnp.dot(x, wg_b[...], preferred_element_type=f32)
    u = jnp.dot(x, wu_b[...], preferred_element_type=f32)
    a = (_silu(g) * u).astype(bf16)
    y_ref[...] = jnp.dot(a, wd_b[...], preferred_element_type=f32)

    @pl.when(i == last)
    def _():
        gather_wait(1 - slot)


def moe_experts(h_rows, slot_tok, w_gate, w_up, w_down, block_e):
    t, d = h_rows.shape
    tm = MOE_TILE
    n_blocks = slot_tok.shape[0] // tm
    d_e = w_gate.shape[-1]
    idx = slot_tok.reshape(n_blocks, 1, tm)
    return pl.pallas_call(
        _expert_kernel,
        grid_spec=pltpu.PrefetchScalarGridSpec(
            num_scalar_prefetch=1,
            grid=(n_blocks,),
            in_specs=[pl.BlockSpec((1, 1, tm), lambda i, be: (0, 0, 0), memory_space=pltpu.SMEM),
                      pl.BlockSpec((1, 1, tm), lambda i, be: (jnp.minimum(i + 1, n_blocks - 1), 0, 0),
                                   memory_space=pltpu.SMEM),
                      pl.BlockSpec(memory_space=pl.ANY),
                      pl.BlockSpec((1, d, d_e), lambda i, be: (be[i], 0, 0)),
                      pl.BlockSpec((1, d, d_e), lambda i, be: (be[i], 0, 0)),
                      pl.BlockSpec((1, d_e, d), lambda i, be: (be[i], 0, 0))],
            out_specs=pl.BlockSpec((tm, d), lambda i, be: (i, 0)),
            scratch_shapes=[pltpu.VMEM((2, tm, d), h_rows.dtype), pltpu.SemaphoreType.DMA((2,)),
                            pltpu.VMEM((d, d_e), bf16), pltpu.VMEM((d, d_e), bf16), pltpu.VMEM((d_e, d), bf16)],
        ),
        out_shape=jax.ShapeDtypeStruct((n_blocks * tm, d), f32),
        compiler_params=_params("arbitrary"),
        name="moe_experts",
    )(block_e, idx, idx, h_rows, w_gate, w_up, w_down)


def _combine_kernel(idx_ref, w_ref, y_ref, o_ref, buf, sem):
    n_k, rows, _ = buf.shape

    def issue(r, carry):
        for kk in range(n_k):
            _row_copy(y_ref, buf.at[kk], idx_ref[0, kk, r], r, sem).start()
        return carry

    lax.fori_loop(0, rows, issue, 0, unroll=8)
    for kk in range(n_k):
        pltpu.make_async_copy(y_ref.at[pl.ds(0, rows)], buf.at[kk], sem).wait()
    w = w_ref[...]
    acc = buf[0] * w[:, 0:1]
    for kk in range(1, n_k):
        acc = acc + buf[kk] * w[:, kk:kk + 1]
    o_ref[...] = acc


def moe_combine(y, dest, top_w):
    n_k, t = dest.shape
    d = y.shape[1]
    rows = 128
    idx = dest.reshape(n_k, t // rows, rows).transpose(1, 0, 2)
    return pl.pallas_call(
        _combine_kernel,
        grid=(t // rows,),
        in_specs=[pl.BlockSpec((1, n_k, rows), lambda i: (i, 0, 0), memory_space=pltpu.SMEM),
                  pl.BlockSpec((rows, n_k), lambda i: (i, 0)),
                  pl.BlockSpec(memory_space=pl.ANY)],
        out_specs=pl.BlockSpec((rows, d), lambda i: (i, 0)),
        out_shape=jax.ShapeDtypeStruct((t, d), f32),
        scratch_shapes=[pltpu.VMEM((n_k, rows, d), f32), pltpu.SemaphoreType.DMA(())],
        compiler_params=_params("arbitrary"),
        name="moe_combine",
    )(idx, top_w.T, y)


def _moe_plan(top_e, rank, counts):
    n_k, t = top_e.shape
    n_experts = counts.shape[0]
    tm = MOE_TILE
    n_blocks = -(-(n_k * t) // tm) + n_experts
    padded = (counts + tm - 1) // tm * tm
    pad_end = jnp.cumsum(padded)
    pad_start = pad_end - padded
    onehot = top_e[:, :, None] == jnp.arange(n_experts, dtype=i32)
    dest = jnp.sum(jnp.where(onehot, pad_start, 0), axis=-1) + rank - 1
    tok = jnp.broadcast_to(jnp.arange(t, dtype=i32)[None, :], (n_k, t))
    slot_tok = jnp.zeros((n_blocks * tm,), i32).at[dest.reshape(-1)].set(tok.reshape(-1), unique_indices=True)
    n_used = pad_end[-1] // tm
    blk = jnp.minimum(jnp.arange(n_blocks, dtype=i32), n_used - 1)
    block_e = jnp.minimum(jnp.searchsorted(pad_end, blk * tm, side='right'), n_experts - 1).astype(i32)
    return dest.astype(i32), slot_tok, block_e


def _rope_tables(n_ctx, n_lat):
    t = jnp.arange(n_lat)
    pos = jnp.stack([t // GRID_W, t % GRID_W], axis=-1).astype(f32)
    n_freq = HEAD_DIM // 4
    inv_freq = ROPE_THETA ** (-jnp.arange(n_freq, dtype=f32) / n_freq)
    ang = pos[:, :, None] * inv_freq
    cos, sin = jnp.cos(ang), jnp.sin(ang)
    cos_t = jnp.concatenate([cos[:, 0], cos[:, 0], cos[:, 1], cos[:, 1]], axis=-1)
    sin_t = jnp.concatenate([-sin[:, 0], sin[:, 0], -sin[:, 1], sin[:, 1]], axis=-1)
    cos_t = jnp.concatenate([jnp.ones((n_ctx, HEAD_DIM), f32), cos_t], axis=0)
    sin_t = jnp.concatenate([jnp.zeros((n_ctx, HEAD_DIM), f32), sin_t], axis=0)
    return cos_t, sin_t


def kernel(x, c, ctx, c_ctx, w_ada, b_ada, norm_mix, norm_ffn, w_in, q_norm_a, k_norm_a, hgrn_lb_logits, hgrn_out_norm, q_norm_c, k_norm_c, lambda_q1, lambda_k1, lambda_q2, lambda_k2, diff_sub_norm, w_branch, w_out, w_router, b_router, w_exp_gate, w_exp_up, w_exp_down, w_sh_gate, w_sh_up, w_sh_down):
    n_batch, n_lat, d = x.shape
    n_ctx = ctx.shape[1]
    depth = w_ada.shape[0]
    rows_b = n_ctx + n_lat
    t = n_batch * rows_b
    branch_w = d // 2
    a_heads = branch_w // HEAD_DIM
    a_kv = a_heads // 4
    b_heads = branch_w // HEAD_DIM
    c_heads = branch_w // (2 * HEAD_DIM)
    n_experts = w_router.shape[-1]
    tm = MM_TILE_M if rows_b % MM_TILE_M == 0 else ROW_TILE
    tiles_per_batch = rows_b // tm
    assert n_ctx % ROW_TILE == 0 and n_lat % ROW_TILE == 0 and rows_b % SCAN_TILE == 0

    cos_t, sin_t = _rope_tables(n_ctx, n_lat)
    p_lb = jax.nn.softmax(hgrn_lb_logits.astype(f32), axis=0)
    lower_bounds = jnp.cumsum(p_lb, axis=0) - p_lb[0]
    cond = jnp.zeros((8, d), f32).at[:n_batch].set(c).at[n_batch].set(c_ctx)

    xa = jnp.concatenate([ctx, x], axis=1).reshape(t, d)

    def mod_specs(tn):
        ctx_spec = pl.BlockSpec((1, 1, tn), lambda i, j: (n_batch, 0, j))
        b_spec = pl.BlockSpec((1, 1, tn), lambda i, j: (i // tiles_per_batch, 0, j))
        return ctx_spec, b_spec

    splits = (a_heads * HEAD_DIM, a_kv * HEAD_DIM, a_kv * HEAD_DIM,
              branch_w, branch_w, branch_w, branch_w, branch_w,
              branch_w, branch_w, branch_w, 3 * d)
    offs = [0]
    for s in splits:
        offs.append(offs[-1] + s)

    def cols(w, *ids):
        return jnp.concatenate([w[:, offs[i]:offs[i + 1]] for i in ids], axis=1).astype(bf16)

    n_qk_heads = a_heads + a_kv + 4 * c_heads
    qk_tn = (n_qk_heads // 2) * HEAD_DIM if n_qk_heads % 2 == 0 else n_qk_heads * HEAD_DIM
    sm_scale = HEAD_DIM ** -0.5 * LOG2E

    for l in range(depth):
        lam_init = 0.8 - 0.6 * math.exp(-0.3 * l)
        mod = fused_matmul([(cond, w_ada[l])], [(b_ada[l].reshape(1, -1), pl.BlockSpec((1, 1024), lambda i, j: (0, j)))],
                           _ep_bias, jax.ShapeDtypeStruct((8, 6 * d), f32), _tile_spec(8, 1024), 8, 1024,
                           a_fn=_silu, name="ada_ln")
        mod = mod[:n_batch + 1].reshape(n_batch + 1, 6, 1, d)
        mods = [mod[:, k] for k in range(6)]

        (h,) = norm_modulate(xa, norm_mix[l], mods[0], mods[1], rows_b, n_ctx, [bf16])
        w_l = w_in[l]
        gains = jnp.concatenate([jnp.tile(q_norm_a[l] * sm_scale, a_heads), jnp.tile(k_norm_a[l], a_kv),
                                 jnp.tile(q_norm_c[l] * sm_scale, 2 * c_heads), jnp.tile(k_norm_c[l], 2 * c_heads)])
        qk = fused_matmul(
            [(h, cols(w_l, 0, 1, 8, 9))],
            [(gains.reshape(1, -1).astype(f32), pl.BlockSpec((1, qk_tn), lambda i, j: (0, j))),
             (cos_t, pl.BlockSpec((tm, HEAD_DIM), lambda i, j: (i % tiles_per_batch, 0))),
             (sin_t, pl.BlockSpec((tm, HEAD_DIM), lambda i, j: (i % tiles_per_batch, 0)))],
            _ep_qk, jax.ShapeDtypeStruct((n_qk_heads, t, HEAD_DIM), bf16),
            pl.BlockSpec((qk_tn // HEAD_DIM, tm, HEAD_DIM), lambda i, j: (j, i, 0)), tm, qk_tn, name="in_proj_qk")
        va = fused_matmul([(h, cols(w_l, 2))], [], _ep_store(bf16), jax.ShapeDtypeStruct((t, a_kv * HEAD_DIM), bf16),
                          _tile_spec(tm, a_kv * HEAD_DIM), tm, a_kv * HEAD_DIM, name="in_proj_va")
        vc = fused_matmul([(h, cols(w_l, 10))], [], _ep_store(bf16), jax.ShapeDtypeStruct((t, branch_w), bf16),
                          _tile_spec(tm, 512), tm, 512, name="in_proj_vc")
        qio = fused_matmul([(h, cols(w_l, 3, 4, 7))], [], _ep_store(bf16), jax.ShapeDtypeStruct((t, 3 * branch_w), bf16),
                           _tile_spec(tm, 512), tm, 512, name="in_proj_hgrn")
        zfb = fused_matmul([(h, cols(w_l, 5, 6))], [], _ep_store(f32), jax.ShapeDtypeStruct((t, 2 * branch_w), f32),
                           _tile_spec(tm, 512), tm, 512, name="in_proj_forget")
        gates = fused_matmul([(h, cols(w_l, 11))], [], _ep_sigmoid, jax.ShapeDtypeStruct((t, 3 * d), bf16),
                             _tile_spec(tm, 512), tm, 512, name="in_proj_gates")

        dims = dict(n_batch=n_batch, rows_per_batch=rows_b, n_ctx=n_ctx)
        o_a = attention_a(qk, va, a_heads=a_heads, a_kv_heads=a_kv, **dims)
        lam_params = jnp.stack([lambda_q1[l], lambda_k1[l], lambda_q2[l], lambda_k2[l]]).astype(f32)
        o_c = attention_c(qk, vc, lam_params, diff_sub_norm[l], head0_q=a_heads + a_kv,
                          head0_k=a_heads + a_kv + 2 * c_heads, c_heads=c_heads, lam_init=lam_init, **dims)
        o_f, o_r = hgrn_scan(qio, zfb, lower_bounds[l], n_batch=n_batch, rows_per_batch=rows_b, b_heads=b_heads)
        o_b = hgrn_readout(o_f, o_r, qio, hgrn_out_norm[l], b_heads)

        wb = w_branch[l].astype(bf16)
        n_col = d // 512
        y = fused_matmul(
            [(o_a, wb[0]), (o_b, wb[1]), (o_c, wb[2])],
            [(gates, pl.BlockSpec((tm, 512), lambda i, j, r=r: (i, r * n_col + j))) for r in range(3)],
            _ep_merge, jax.ShapeDtypeStruct((t, d), bf16), _tile_spec(tm, 512), tm, 512, name="merge")
        ctx_spec, b_spec = mod_specs(512)
        xa = fused_matmul(
            [(y, w_out[l].astype(bf16))],
            [(xa, _tile_spec(tm, 512)), (mods[2], ctx_spec), (mods[2], b_spec)],
            _make_ep_residual(tm, rows_b, n_ctx, False), jax.ShapeDtypeStruct((t, d), f32),
            _tile_spec(tm, 512), tm, 512, name="out_proj")

        h2, h2_rows = norm_modulate(xa, norm_ffn[l], mods[3], mods[4], rows_b, n_ctx, [bf16, f32])
        top_e, top_w, rank, counts = moe_route(h2, w_router[l].T.astype(bf16), b_router[l])
        dest, slot_tok, block_e = _moe_plan(top_e, rank, counts.reshape(-1))
        ys = moe_experts(h2_rows, slot_tok, w_exp_gate[l], w_exp_up[l], w_exp_down[l], block_e)
        routed = moe_combine(ys, dest, top_w)
        d_sh = w_sh_gate.shape[-1]
        act = fused_matmul([(h2, w_sh_gate[l].astype(bf16)), (h2, w_sh_up[l].astype(bf16))], [], _ep_glu,
                           jax.ShapeDtypeStruct((t, d_sh), bf16), _tile_spec(tm, d_sh), tm, d_sh, name="shared_glu")
        xa = fused_matmul(
            [(act, w_sh_down[l].astype(bf16))],
            [(xa, _tile_spec(tm, 512)), (mods[5], ctx_spec), (mods[5], b_spec), (routed, _tile_spec(tm, 512))],
            _make_ep_residual(tm, rows_b, n_ctx, True), jax.ShapeDtypeStruct((t, d), f32),
            _tile_spec(tm, 512), tm, 512, name="moe_out")

    return xa.reshape(n_batch, rows_b, d)[:, n_ctx:]
```

```python
import functools
import math

import jax
import jax.numpy as jnp
from jax import lax
from jax.experimental import pallas as pl
from jax.experimental.pallas import tpu as pltpu

f32 = jnp.float32
bf16 = jnp.bfloat16
i32 = jnp.int32

HEAD_DIM = 128
GRID_W = 64
ROPE_THETA = 10000.0
EPS = 1e-6
F_MIN = 1e-6
SCAN_CHUNK = 64
SCAN_SUB = 8
N_GROUPS = 8
TOPK_GROUPS = 4
TOP_K = 8
ROUTE_SCALE = 2.5
MASK_SCORE = -1e9
LOG2E = 1.4426950408889634

VMEM_LIMIT_BYTES = 48 * 1024 * 1024
ROW_TILE = 256
MM_TILE_M = 768
MOE_TILE = 256
SCAN_TILE = 256


def _params(*semantics):
    return pltpu.CompilerParams(dimension_semantics=semantics, vmem_limit_bytes=VMEM_LIMIT_BYTES)


def _silu(x):
    return x * jax.nn.sigmoid(x)


def _nt_dot(a, b):
    return lax.dot_general(a, b, (((1,), (1,)), ((), ())), preferred_element_type=f32)


def _tn_dot(a, b):
    return lax.dot_general(a, b, (((0,), (0,)), ((), ())), preferred_element_type=f32)


def _row_is_ctx(tile_index, tile_rows, rows_per_batch, n_ctx):
    row0 = (tile_index % (rows_per_batch // tile_rows)) * tile_rows
    rid = row0 + lax.broadcasted_iota(i32, (tile_rows, 1), 0)
    return rid < n_ctx


def _fm_kernel(*refs, n_pairs, n_extra, a_fn, epilogue):
    pair_refs = refs[:2 * n_pairs]
    extra = refs[2 * n_pairs:2 * n_pairs + n_extra]
    outs = refs[2 * n_pairs + n_extra:]
    accs = []
    for p in range(n_pairs):
        a = pair_refs[2 * p][...]
        if a_fn is not None:
            a = a_fn(a)
        w_ref = pair_refs[2 * p + 1]
        w = w_ref[0] if len(w_ref.shape) == 3 else w_ref[...]
        accs.append(jnp.dot(a.astype(bf16), w.astype(bf16), preferred_element_type=f32))
    epilogue(accs, extra, outs)


def fused_matmul(pairs, extras, epilogue, out_shapes, out_specs, tm, tn, a_fn=None, name=None, layer=None):
    m = pairs[0][0].shape[0]
    n = pairs[0][1].shape[-1]
    assert m % tm == 0 and n % tn == 0, (m, tm, n, tn)
    in_specs, args = [], []
    for a, w in pairs:
        k = a.shape[1]
        assert w.shape[-2] == k
        if layer is None:
            w_spec = pl.BlockSpec((k, tn), lambda i, j: (0, j))
        else:
            w_spec = pl.BlockSpec((1, k, tn), lambda i, j: (layer, 0, j))
        in_specs += [pl.BlockSpec((tm, k), lambda i, j: (i, 0)), w_spec]
        args += [a, w]
    for arr, spec in extras:
        in_specs.append(spec)
        args.append(arr)
    return pl.pallas_call(
        functools.partial(_fm_kernel, n_pairs=len(pairs), n_extra=len(extras), a_fn=a_fn, epilogue=epilogue),
        grid=(m // tm, n // tn),
        in_specs=in_specs,
        out_specs=out_specs,
        out_shape=out_shapes,
        compiler_params=_params("parallel", "arbitrary"),
        name=name,
    )(*args)


def _tile_spec(tm, tn):
    return pl.BlockSpec((tm, tn), lambda i, j: (i, j))


def _ep_store(dtype):
    def ep(accs, extra, outs):
        outs[0][...] = accs[0].astype(dtype)
    return ep


def _ep_bias(accs, extra, outs):
    outs[0][...] = accs[0] + extra[0][...]


def _ep_sigmoid(accs, extra, outs):
    outs[0][...] = jax.nn.sigmoid(accs[0]).astype(outs[0].dtype)


def _ep_qk(accs, extra, outs):
    gain_ref, cos_ref, sin_ref = extra
    acc = accs[0]
    cos = cos_ref[...]
    sin = sin_ref[...]
    lane = lax.broadcasted_iota(i32, cos.shape, 1)
    first_half = (lane % (HEAD_DIM // 2)) < (HEAD_DIM // 4)
    for c in range(acc.shape[1] // HEAD_DIM):
        x = acc[:, c * HEAD_DIM:(c + 1) * HEAD_DIM]
        y = x * lax.rsqrt(jnp.mean(x * x, axis=-1, keepdims=True) + EPS)
        y = y * gain_ref[:, c * HEAD_DIM:(c + 1) * HEAD_DIM]
        partner = jnp.where(first_half,
                            pltpu.roll(y, HEAD_DIM - HEAD_DIM // 4, 1),
                            pltpu.roll(y, HEAD_DIM // 4, 1))
        outs[0][c] = (y * cos + partner * sin).astype(outs[0].dtype)


def _ep_merge(accs, extra, outs):
    y = extra[0][...].astype(f32) * accs[0]
    y = y + extra[1][...].astype(f32) * accs[1]
    y = y + extra[2][...].astype(f32) * accs[2]
    outs[0][...] = y.astype(outs[0].dtype)


def _ep_glu(accs, extra, outs):
    outs[0][...] = (_silu(accs[0]) * accs[1]).astype(outs[0].dtype)


def _make_ep_residual(tm, rows_per_batch, n_ctx, with_routed):
    def ep(accs, extra, outs):
        x_ref, mod_ctx_ref, mod_b_ref = extra[:3]
        y = accs[0]
        if with_routed:
            y = y + extra[3][...]
        is_ctx = _row_is_ctx(pl.program_id(0), tm, rows_per_batch, n_ctx)
        mod = jnp.where(is_ctx, mod_ctx_ref[0], mod_b_ref[0])
        outs[0][...] = x_ref[...] + mod * y
    return ep


PACK_ROWS = 8


def _norm_kernel(x_ref, w_ref, shc_ref, shb_ref, scc_ref, scb_ref, o_ref, *packed, tile, rows_per_batch, n_ctx):
    x = x_ref[...]
    y = x * lax.rsqrt(jnp.mean(x * x, axis=-1, keepdims=True) + EPS) * w_ref[...]
    is_ctx = _row_is_ctx(pl.program_id(0), tile, rows_per_batch, n_ctx)
    shift = jnp.where(is_ctx, shc_ref[0], shb_ref[0])
    scale = jnp.where(is_ctx, scc_ref[0], scb_ref[0])
    hb = (y * (1.0 + scale) + shift).astype(bf16)
    o_ref[...] = hb
    if packed:
        bits = lax.bitcast_convert_type(hb.astype(f32), jnp.uint32)
        half = bits.shape[1] // 2
        for s in range(PACK_ROWS):
            lo = bits[:, s * HEAD_DIM:(s + 1) * HEAD_DIM] >> 16
            hi = bits[:, half + s * HEAD_DIM:half + (s + 1) * HEAD_DIM]
            packed[0][pl.ds(s, tile, stride=PACK_ROWS), :] = hi | lo


def norm_modulate(x, w, shift, scale, rows_per_batch, n_ctx, packed_rows=False):
    t, d = x.shape
    tile = ROW_TILE
    n_b = shift.shape[0] - 1
    tiles_per_batch = rows_per_batch // tile
    ctx_spec = pl.BlockSpec((1, 1, d), lambda i: (n_b, 0, 0))
    b_spec = pl.BlockSpec((1, 1, d), lambda i: (i // tiles_per_batch, 0, 0))
    row_spec = pl.BlockSpec((tile, d), lambda i: (i, 0))
    out_specs, out_shape = [row_spec], [jax.ShapeDtypeStruct((t, d), bf16)]
    if packed_rows:
        assert d == 2 * PACK_ROWS * HEAD_DIM
        out_specs.append(pl.BlockSpec((tile * PACK_ROWS, HEAD_DIM), lambda i: (i, 0)))
        out_shape.append(jax.ShapeDtypeStruct((t * PACK_ROWS, HEAD_DIM), jnp.uint32))
    return pl.pallas_call(
        functools.partial(_norm_kernel, tile=tile, rows_per_batch=rows_per_batch, n_ctx=n_ctx),
        grid=(t // tile,),
        in_specs=[row_spec, pl.BlockSpec((1, d), lambda i: (0, 0)), ctx_spec, b_spec, ctx_spec, b_spec],
        out_specs=out_specs,
        out_shape=out_shape,
        compiler_params=_params("parallel"),
        name="norm_modulate",
    )(x, w.reshape(1, d), shift, shift, scale, scale)


def _attend(q, k_at, v_at, kv_len, tk, dv):
    m = l = acc = None
    for c in range(kv_len // tk):
        s = _nt_dot(q, k_at(c * tk))
        s_max = jnp.max(s, axis=1, keepdims=True)
        if c == 0:
            m = s_max
            p = jnp.exp2(s - m)
            l = jnp.sum(p, axis=1, keepdims=True)
            acc = jnp.dot(p.astype(bf16), v_at(c * tk), preferred_element_type=f32)
        else:
            m_new = jnp.maximum(m, s_max)
            alpha = jnp.exp2(m - m_new)
            p = jnp.exp2(s - m_new)
            l = alpha * l + jnp.sum(p, axis=1, keepdims=True)
            acc = alpha * acc + jnp.dot(p.astype(bf16), v_at(c * tk), preferred_element_type=f32)
            m = m_new
    return acc / l


def _kv_chunk(kv_len):
    for tk in (768, 1408, 512, 256, 128):
        if kv_len % tk == 0:
            return tk
    raise ValueError(kv_len)


def _ctx_or_all(n_ctx_tiles, n_ctx, kv_all, run):
    is_ctx = pl.program_id(2) < n_ctx_tiles

    @pl.when(is_ctx)
    def _():
        run(n_ctx)

    @pl.when(jnp.logical_not(is_ctx))
    def _():
        run(kv_all)


def _attn_a_kernel(q_ref, k_ref, v_ref, o_ref, *, n_ctx):
    g, tq, d = q_ref.shape

    def run(kv_len):
        tk = _kv_chunk(kv_len)
        q = q_ref[...].reshape(g * tq, d)
        o = _attend(q, lambda off: k_ref[0, off:off + tk, :], lambda off: v_ref[off:off + tk, :], kv_len, tk, d)
        for h in range(g):
            o_ref[:, h * d:(h + 1) * d] = o[h * tq:(h + 1) * tq].astype(o_ref.dtype)

    _ctx_or_all(n_ctx // tq, n_ctx, k_ref.shape[1], run)


def _attn_c_kernel(lam_ref, dn_ref, q_ref, k_ref, v_ref, o_ref, *, n_ctx, lam_init):
    tq = q_ref.shape[1]
    dv = v_ref.shape[-1]

    def run(kv_len):
        tk = _kv_chunk(kv_len)
        o1 = _attend(q_ref[0], lambda off: k_ref[0, off:off + tk, :], lambda off: v_ref[off:off + tk, :],
                     kv_len, tk, dv)
        o2 = _attend(q_ref[1], lambda off: k_ref[1, off:off + tk, :], lambda off: v_ref[off:off + tk, :],
                     kv_len, tk, dv)
        lam_p = lam_ref[...]
        lam = (jnp.exp(jnp.sum(lam_p[0:1] * lam_p[1:2], axis=-1, keepdims=True))
               - jnp.exp(jnp.sum(lam_p[2:3] * lam_p[3:4], axis=-1, keepdims=True)) + lam_init)
        o = o1 - lam * o2
        y = o * lax.rsqrt(jnp.mean(o * o, axis=-1, keepdims=True) + EPS) * dn_ref[...]
        o_ref[...] = (y * (1.0 - lam_init)).astype(o_ref.dtype)

    _ctx_or_all(n_ctx // tq, n_ctx, k_ref.shape[1], run)


def attention_a(qk, va, *, n_batch, rows_per_batch, n_ctx, a_heads, a_kv_heads):
    t = qk.shape[1]
    group = a_heads // a_kv_heads
    tq = ROW_TILE
    n_q = rows_per_batch // tq
    return pl.pallas_call(
        functools.partial(_attn_a_kernel, n_ctx=n_ctx),
        grid=(n_batch, a_kv_heads, n_q),
        in_specs=[
            pl.BlockSpec((group, tq, HEAD_DIM), lambda b, h, i: (h, b * n_q + i, 0)),
            pl.BlockSpec((1, rows_per_batch, HEAD_DIM), lambda b, h, i: (a_heads + h, b, 0)),
            pl.BlockSpec((rows_per_batch, HEAD_DIM), lambda b, h, i: (b, h)),
        ],
        out_specs=pl.BlockSpec((tq, group * HEAD_DIM), lambda b, h, i: (b * n_q + i, h)),
        out_shape=jax.ShapeDtypeStruct((t, a_heads * HEAD_DIM), bf16),
        compiler_params=_params("parallel", "parallel", "arbitrary"),
        name="attention_a",
    )(qk, qk, va)


def attention_c(qk, vc, lam_params, diff_norm, *, n_batch, rows_per_batch, n_ctx, head0_q, head0_k,
                c_heads, lam_init):
    t = qk.shape[1]
    dv = 2 * HEAD_DIM
    tq = ROW_TILE
    n_q = rows_per_batch // tq
    return pl.pallas_call(
        functools.partial(_attn_c_kernel, n_ctx=n_ctx, lam_init=lam_init),
        grid=(n_batch, c_heads, n_q),
        in_specs=[
            pl.BlockSpec((4, HEAD_DIM), lambda b, h, i: (0, 0)),
            pl.BlockSpec((1, dv), lambda b, h, i: (0, 0)),
            pl.BlockSpec((2, tq, HEAD_DIM), lambda b, h, i: (head0_q // 2 + h, b * n_q + i, 0)),
            pl.BlockSpec((2, rows_per_batch, HEAD_DIM), lambda b, h, i: (head0_k // 2 + h, b, 0)),
            pl.BlockSpec((rows_per_batch, dv), lambda b, h, i: (b, h)),
        ],
        out_specs=pl.BlockSpec((tq, dv), lambda b, h, i: (b * n_q + i, h)),
        out_shape=jax.ShapeDtypeStruct((t, c_heads * dv), bf16),
        compiler_params=_params("parallel", "parallel", "arbitrary"),
        name="attention_c",
    )(lam_params, diff_norm.reshape(1, dv), qk, qk, vc)


def _hgrn_chunk(qp, v, z, lb, st, rev):
    c = SCAN_CHUNK
    nb = c // SCAN_SUB
    q = _silu(qp.astype(f32))
    one_m = 1.0 - lb
    g = jnp.log(jnp.maximum(lb + one_m * jax.nn.sigmoid(z), F_MIN))
    k = one_m * jax.nn.sigmoid(-z)
    vf = v.astype(f32)

    r_i = lax.broadcasted_iota(i32, (c, c), 0)
    c_i = lax.broadcasted_iota(i32, (c, c), 1)
    tri = (c_i >= r_i) if rev else (c_i <= r_i)
    b = jnp.dot(tri.astype(f32), g, precision=lax.Precision.HIGHEST, preferred_element_type=f32)
    end = 0 if rev else c - 1
    b_end = b[end:end + 1, :]

    o = _nt_dot((q * jnp.exp(b)).astype(bf16), st.astype(bf16))
    k_hat = (k * jnp.exp(b_end - b)).astype(bf16)
    st_new = st * jnp.exp(b_end) + _tn_dot(v, k_hat)

    def edge_row(j):
        return j * SCAN_SUB if rev else j * SCAN_SUB + SCAN_SUB - 1

    row = lax.broadcasted_iota(i32, (c, HEAD_DIM), 0)
    rho_own = jnp.concatenate(
        [jnp.broadcast_to(b[edge_row(j):edge_row(j) + 1, :], (SCAN_SUB, HEAD_DIM)) for j in range(nb)], axis=0)
    k_til = k * jnp.exp(rho_own - b)
    qs, ks = [], []
    for j in (range(1, nb) if rev else range(nb - 1)):
        rho = b[edge_row(j):edge_row(j) + 1, :]
        q_rows = (row < j * SCAN_SUB) if rev else (row >= (j + 1) * SCAN_SUB)
        qs.append((q * jnp.exp(jnp.where(q_rows, b - rho, -jnp.inf))).astype(bf16))
        k_rows = (row >= j * SCAN_SUB) & (row < (j + 1) * SCAN_SUB)
        ks.append(jnp.where(k_rows, k_til, 0.0).astype(bf16))
    a_off = _nt_dot(jnp.concatenate(qs, axis=1), jnp.concatenate(ks, axis=1))
    o = o + jnp.dot(a_off.astype(bf16), v, preferred_element_type=f32)

    sub_r = lax.broadcasted_iota(i32, (SCAN_SUB, HEAD_DIM), 0)
    o_diag = []
    for blk in range(nb):
        sl = slice(blk * SCAN_SUB, (blk + 1) * SCAN_SUB)
        b_i, q_i, k_i, v_i = b[sl], q[sl], k[sl], vf[sl]
        acc = jnp.zeros((SCAN_SUB, HEAD_DIM), f32)
        for s in range(SCAN_SUB):
            t_rows = (sub_r <= s) if rev else (sub_r >= s)
            e = jnp.exp(jnp.where(t_rows, b_i - b_i[s:s + 1], -jnp.inf))
            p = q_i * e * k_i[s:s + 1]
            acc = acc + jnp.sum(p, axis=1, keepdims=True) * v_i[s:s + 1]
        o_diag.append(acc)
    return o + jnp.concatenate(o_diag, axis=0), st_new


def _hgrn_kernel(lb_ref, qf_ref, vf_ref, zf_ref, qb_ref, vb_ref, zb_ref, of_ref, ob_ref, sf_ref, sb_ref):
    @pl.when(pl.program_id(2) == 0)
    def _():
        sf_ref[...] = jnp.zeros_like(sf_ref)
        sb_ref[...] = jnp.zeros_like(sb_ref)

    lb = lb_ref[...]
    n_chunks = qf_ref.shape[0] // SCAN_CHUNK

    st_f = sf_ref[...]
    st_b = sb_ref[...]
    for c in range(n_chunks):
        rf = slice(c * SCAN_CHUNK, (c + 1) * SCAN_CHUNK)
        o, st_f = _hgrn_chunk(qf_ref[rf, :], vf_ref[rf, :], zf_ref[rf, :], lb, st_f, False)
        of_ref[rf, :] = o
        rb = slice((n_chunks - 1 - c) * SCAN_CHUNK, (n_chunks - c) * SCAN_CHUNK)
        o, st_b = _hgrn_chunk(qb_ref[rb, :], vb_ref[rb, :], zb_ref[rb, :], lb, st_b, True)
        ob_ref[rb, :] = o
    sf_ref[...] = st_f
    sb_ref[...] = st_b


def hgrn_scan(qio, zfb, lower_bound, *, n_batch, rows_per_batch, b_heads):
    t = qio.shape[0]
    tile = SCAN_TILE
    n_tiles = rows_per_batch // tile

    def fwd(b, h, j):
        return b * n_tiles + j

    def bwd(b, h, j):
        return b * n_tiles + jnp.where(j == 0, 0, n_tiles - j)

    def spec(tile_fn, col0):
        return pl.BlockSpec((tile, HEAD_DIM), lambda b, h, j: (tile_fn(b, h, j), col0 + h))

    return pl.pallas_call(
        _hgrn_kernel,
        grid=(n_batch, b_heads, n_tiles),
        in_specs=[
            pl.BlockSpec((1, HEAD_DIM), lambda b, h, j: (0, h)),
            spec(fwd, 0), spec(fwd, b_heads), spec(fwd, 0),
            spec(bwd, 0), spec(bwd, b_heads), spec(bwd, b_heads),
        ],
        out_specs=[spec(fwd, 0), spec(bwd, 0)],
        out_shape=[jax.ShapeDtypeStruct((t, b_heads * HEAD_DIM), f32)] * 2,
        scratch_shapes=[pltpu.VMEM((HEAD_DIM, HEAD_DIM), f32)] * 2,
        compiler_params=_params("parallel", "parallel", "arbitrary"),
        name="hgrn_scan",
    )(lower_bound.reshape(1, -1), qio, qio, zfb, qio, qio, zfb)


def _hgrn_out_kernel(of_ref, ob_ref, og_ref, w_ref, o_ref):
    o = of_ref[...] + ob_ref[...]
    w = w_ref[...]
    for c in range(o.shape[1] // HEAD_DIM):
        sl = slice(c * HEAD_DIM, (c + 1) * HEAD_DIM)
        x = o[:, sl]
        y = x * lax.rsqrt(jnp.mean(x * x, axis=-1, keepdims=True) + EPS) * w
        o_ref[:, sl] = (y * _silu(og_ref[:, sl].astype(f32))).astype(o_ref.dtype)


def hgrn_readout(o_f, o_b, qio, hgrn_norm, b_heads):
    t, w = o_f.shape
    tile = ROW_TILE
    row_spec = pl.BlockSpec((tile, w), lambda i: (i, 0))
    return pl.pallas_call(
        _hgrn_out_kernel,
        grid=(t // tile,),
        in_specs=[row_spec, row_spec, pl.BlockSpec((tile, w), lambda i: (i, 2)),
                  pl.BlockSpec((1, HEAD_DIM), lambda i: (0, 0))],
        out_specs=row_spec,
        out_shape=jax.ShapeDtypeStruct((t, w), bf16),
        compiler_params=_params("parallel"),
        name="hgrn_readout",
    )(o_f, o_b, qio, hgrn_norm.reshape(1, HEAD_DIM))


def _router_kernel(h_ref, w_ref, b_ref, e_ref, wt_ref, rank_ref, cnt_ref, run_ref):
    n_e = w_ref.shape[0]
    tm = h_ref.shape[0]
    per = n_e // N_GROUPS

    @pl.when(pl.program_id(0) == 0)
    def _():
        run_ref[...] = jnp.zeros_like(run_ref)

    scores = jax.nn.sigmoid(_nt_dot(w_ref[...], h_ref[...]))
    biased = scores + b_ref[...]
    neg = -jnp.inf

    grp = biased.reshape(N_GROUPS, per, tm)
    idx_in = lax.broadcasted_iota(i32, grp.shape, 1)
    m1 = jnp.max(grp, axis=1, keepdims=True)
    first = jnp.min(jnp.where(grp == m1, idx_in, per), axis=1, keepdims=True)
    m2 = jnp.max(jnp.where(idx_in == first, neg, grp), axis=1, keepdims=True)
    cur = m1 + m2

    g_idx = lax.broadcasted_iota(i32, cur.shape, 0)
    g_sel = jnp.zeros(cur.shape, f32)
    for _ in range(TOPK_GROUPS):
        mx = jnp.max(cur, axis=0, keepdims=True)
        pick = g_idx == jnp.min(jnp.where(cur == mx, g_idx, N_GROUPS), axis=0, keepdims=True)
        g_sel = jnp.where(pick, 1.0, g_sel)
        cur = jnp.where(pick, neg, cur)
    e_mask = jnp.broadcast_to(g_sel, (N_GROUPS, per, tm)).reshape(n_e, tm) > 0.0

    e_idx = lax.broadcasted_iota(i32, (n_e, tm), 0)
    cur = jnp.where(e_mask, biased, MASK_SCORE)
    ws, picks = [], []
    member = jnp.zeros((n_e, tm), f32)
    for r in range(TOP_K):
        mx = jnp.max(cur, axis=0, keepdims=True)
        first = jnp.min(jnp.where(cur == mx, e_idx, n_e), axis=0, keepdims=True)
        pick = e_idx == first
        e_ref[r:r + 1, :] = first
        ws.append(jnp.sum(jnp.where(pick, scores, 0.0), axis=0, keepdims=True))
        cur = jnp.where(pick, neg, cur)
        member = jnp.where(pick, 1.0, member)
        picks.append(pick)
    total = ws[0]
    for r in range(1, TOP_K):
        total = total + ws[r]
    for r in range(TOP_K):
        wt_ref[r:r + 1, :] = ws[r] / total * ROUTE_SCALE

    u_i = lax.broadcasted_iota(i32, (tm, tm), 0)
    t_i = lax.broadcasted_iota(i32, (tm, tm), 1)
    prefix = jnp.dot(member.astype(bf16), (u_i <= t_i).astype(bf16), preferred_element_type=f32)
    rank = run_ref[...] + prefix
    for r in range(TOP_K):
        rank_ref[r:r + 1, :] = jnp.sum(jnp.where(picks[r], rank, 0.0), axis=0, keepdims=True).astype(i32)
    run_ref[...] = run_ref[...] + jnp.sum(member, axis=1, keepdims=True)
    cnt_ref[...] = run_ref[...].astype(i32)


def moe_route(h, w_router_t, b_router):
    t, d = h.shape
    n_e = w_router_t.shape[0]
    tm = 512
    assert t % tm == 0
    out_spec = pl.BlockSpec((TOP_K, tm), lambda i: (0, i))
    return pl.pallas_call(
        _router_kernel,
        grid=(t // tm,),
        in_specs=[pl.BlockSpec((tm, d), lambda i: (i, 0)), pl.BlockSpec((n_e, d), lambda i: (0, 0)),
                  pl.BlockSpec((n_e, 1), lambda i: (0, 0))],
        out_specs=[out_spec, out_spec, out_spec, pl.BlockSpec((n_e, 1), lambda i: (0, 0))],
        out_shape=[jax.ShapeDtypeStruct((TOP_K, t), i32), jax.ShapeDtypeStruct((TOP_K, t), f32),
                   jax.ShapeDtypeStruct((TOP_K, t), i32), jax.ShapeDtypeStruct((n_e, 1), i32)],
        scratch_shapes=[pltpu.VMEM((n_e, 1), f32)],
        compiler_params=_params("arbitrary"),
        name="moe_route",
    )(h, w_router_t, b_router.reshape(n_e, 1).astype(f32))


def _row_copy(src_ref, dst_ref, src_row, dst_row, sem):
    return pltpu.make_async_copy(src_ref.at[pl.ds(src_row, 1)], dst_ref.at[pl.ds(dst_row, 1)], sem)


GATHER_DEPTH = 3


def _expert_kernel(be_ref, idx0_ref, idx1_ref, idx_ahead_ref, h_ref, wg_ref, wu_ref, wd_ref, y_ref,
                   xbuf, sem, wg_b, wu_b, wd_b):
    i = pl.program_id(0)
    last = pl.num_programs(0) - 1
    tm = xbuf.shape[1] // PACK_ROWS
    slot = i % GATHER_DEPTH
    ahead = (i + GATHER_DEPTH - 1) % GATHER_DEPTH

    def gather_start(idx_ref, dst_slot):
        for r in range(tm):
            src_row = pl.multiple_of(idx_ref[0, 0, r] * PACK_ROWS, PACK_ROWS)
            pltpu.make_async_copy(h_ref.at[pl.ds(src_row, PACK_ROWS)],
                                  xbuf.at[dst_slot, pl.ds(r * PACK_ROWS, PACK_ROWS)], sem.at[dst_slot]).start()

    def gather_wait(dst_slot):
        pltpu.make_async_copy(h_ref.at[pl.ds(0, tm * PACK_ROWS)], xbuf.at[dst_slot], sem.at[dst_slot]).wait()

    @pl.when(i == 0)
    def _():
        gather_start(idx0_ref, 0)
        gather_start(idx1_ref, 1)

    @pl.when((i == 0) | (be_ref[i] != be_ref[jnp.maximum(i - 1, 0)]))
    def _():
        wg_b[...] = wg_ref[0, 0].astype(bf16)
        wu_b[...] = wu_ref[0, 0].astype(bf16)
        wd_b[...] = wd_ref[0, 0].astype(bf16)

    gather_wait(slot)
    gather_start(idx_ahead_ref, ahead)
    lo, hi = [], []
    for s in range(PACK_ROWS):
        word = xbuf[slot, pl.ds(s, tm, stride=PACK_ROWS), :]
        lo.append(lax.bitcast_convert_type(word << 16, f32).astype(bf16))
        hi.append(lax.bitcast_convert_type(word & jnp.uint32(0xFFFF0000), f32).astype(bf16))
    x = jnp.concatenate(lo + hi, axis=1)
    g = jnp.dot(x, wg_b[...], preferred_element_type=f32)
    u = jnp.dot(x, wu_b[...], preferred_element_type=f32)
    a = (_silu(g) * u).astype(bf16)
    y_ref[...] = jnp.dot(a, wd_b[...], preferred_element_type=f32)

    @pl.when(i == last)
    def _():
        for k in range(1, GATHER_DEPTH):
            gather_wait((i + k) % GATHER_DEPTH)


def moe_experts(h_packed, slot_tok, w_gate, w_up, w_down, layer, block_e):
    tm = MOE_TILE
    n_blocks = slot_tok.shape[0] // tm
    assert n_blocks >= GATHER_DEPTH
    d, d_e = w_gate.shape[-2:]
    idx = slot_tok.reshape(n_blocks, 1, tm)

    def idx_spec(block_fn):
        return pl.BlockSpec((1, 1, tm), lambda i, be: (block_fn(i), 0, 0), memory_space=pltpu.SMEM)

    def w_spec(shape):
        return pl.BlockSpec((1, 1) + shape, lambda i, be: (layer, be[i], 0, 0))

    return pl.pallas_call(
        _expert_kernel,
        grid_spec=pltpu.PrefetchScalarGridSpec(
            num_scalar_prefetch=1,
            grid=(n_blocks,),
            in_specs=[idx_spec(lambda i: 0), idx_spec(lambda i: 1),
                      idx_spec(lambda i: jnp.minimum(i + GATHER_DEPTH - 1, n_blocks - 1)),
                      pl.BlockSpec(memory_space=pl.ANY),
                      w_spec((d, d_e)), w_spec((d, d_e)), w_spec((d_e, d))],
            out_specs=pl.BlockSpec((tm, d), lambda i, be: (i, 0)),
            scratch_shapes=[pltpu.VMEM((GATHER_DEPTH, tm * PACK_ROWS, HEAD_DIM), jnp.uint32),
                            pltpu.SemaphoreType.DMA((GATHER_DEPTH,)),
                            pltpu.VMEM((d, d_e), bf16), pltpu.VMEM((d, d_e), bf16), pltpu.VMEM((d_e, d), bf16)],
        ),
        out_shape=jax.ShapeDtypeStruct((n_blocks * tm, d), f32),
        compiler_params=_params("arbitrary"),
        name="moe_experts",
    )(block_e, idx, idx, idx, h_packed, w_gate, w_up, w_down)


def _combine_kernel(idx_ref, idx_next_ref, w_ref, y_ref, o_ref, buf, sem):
    i = pl.program_id(0)
    last = pl.num_programs(0) - 1
    _, n_k, rows, _ = buf.shape
    slot = i % 2

    def gather_start(src_idx_ref, dst_slot):
        def issue(r, carry):
            for kk in range(n_k):
                _row_copy(y_ref, buf.at[dst_slot, kk], src_idx_ref[0, kk, r], r, sem.at[dst_slot]).start()
            return carry
        lax.fori_loop(0, rows, issue, 0, unroll=8)

    def gather_wait(dst_slot):
        for kk in range(n_k):
            pltpu.make_async_copy(y_ref.at[pl.ds(0, rows)], buf.at[dst_slot, kk], sem.at[dst_slot]).wait()

    @pl.when(i == 0)
    def _():
        gather_start(idx_ref, 0)

    gather_start(idx_next_ref, 1 - slot)
    gather_wait(slot)
    w = w_ref[...]
    acc = buf[slot, 0] * w[:, 0:1]
    for kk in range(1, n_k):
        acc = acc + buf[slot, kk] * w[:, kk:kk + 1]
    o_ref[...] = acc

    @pl.when(i == last)
    def _():
        gather_wait(1 - slot)


def moe_combine(y, dest, top_w):
    n_k, t = dest.shape
    d = y.shape[1]
    rows = 128
    n_tiles = t // rows
    idx = dest.reshape(n_k, n_tiles, rows).transpose(1, 0, 2)

    def idx_spec(tile_fn):
        return pl.BlockSpec((1, n_k, rows), lambda i: (tile_fn(i), 0, 0), memory_space=pltpu.SMEM)

    return pl.pallas_call(
        _combine_kernel,
        grid=(n_tiles,),
        in_specs=[idx_spec(lambda i: i), idx_spec(lambda i: jnp.minimum(i + 1, n_tiles - 1)),
                  pl.BlockSpec((rows, n_k), lambda i: (i, 0)),
                  pl.BlockSpec(memory_space=pl.ANY)],
        out_specs=pl.BlockSpec((rows, d), lambda i: (i, 0)),
        out_shape=jax.ShapeDtypeStruct((t, d), f32),
        scratch_shapes=[pltpu.VMEM((2, n_k, rows, d), f32), pltpu.SemaphoreType.DMA((2,))],
        compiler_params=_params("arbitrary"),
        name="moe_combine",
    )(idx, idx, top_w.T, y)


def _moe_plan(top_e, rank, counts):
    n_k, t = top_e.shape
    n_experts = counts.shape[0]
    tm = MOE_TILE
    n_blocks = -(-(n_k * t) // tm) + n_experts
    padded = (counts + tm - 1) // tm * tm
    pad_end = jnp.cumsum(padded)
    pad_start = pad_end - padded
    onehot = top_e[:, :, None] == jnp.arange(n_experts, dtype=i32)
    dest = jnp.sum(jnp.where(onehot, pad_start, 0), axis=-1) + rank - 1
    tok = jnp.broadcast_to(jnp.arange(t, dtype=i32)[None, :], (n_k, t))
    slot_tok = jnp.zeros((n_blocks * tm,), i32).at[dest.reshape(-1)].set(tok.reshape(-1), unique_indices=True)
    n_used = pad_end[-1] // tm
    blk = jnp.minimum(jnp.arange(n_blocks, dtype=i32), n_used - 1)
    block_e = jnp.sum((pad_end[None, :] <= (blk * tm)[:, None]).astype(i32), axis=1)
    block_e = jnp.minimum(block_e, n_experts - 1)
    return dest.astype(i32), slot_tok, block_e


def _rope_tables(n_ctx, n_lat):
    t = jnp.arange(n_lat)
    pos = jnp.stack([t // GRID_W, t % GRID_W], axis=-1).astype(f32)
    n_freq = HEAD_DIM // 4
    inv_freq = ROPE_THETA ** (-jnp.arange(n_freq, dtype=f32) / n_freq)
    ang = pos[:, :, None] * inv_freq
    cos, sin = jnp.cos(ang), jnp.sin(ang)
    cos_t = jnp.concatenate([cos[:, 0], cos[:, 0], cos[:, 1], cos[:, 1]], axis=-1)
    sin_t = jnp.concatenate([-sin[:, 0], sin[:, 0], -sin[:, 1], sin[:, 1]], axis=-1)
    cos_t = jnp.concatenate([jnp.ones((n_ctx, HEAD_DIM), f32), cos_t], axis=0)
    sin_t = jnp.concatenate([jnp.zeros((n_ctx, HEAD_DIM), f32), sin_t], axis=0)
    return cos_t, sin_t


def kernel(x, c, ctx, c_ctx, w_ada, b_ada, norm_mix, norm_ffn, w_in, q_norm_a, k_norm_a, hgrn_lb_logits, hgrn_out_norm, q_norm_c, k_norm_c, lambda_q1, lambda_k1, lambda_q2, lambda_k2, diff_sub_norm, w_branch, w_out, w_router, b_router, w_exp_gate, w_exp_up, w_exp_down, w_sh_gate, w_sh_up, w_sh_down):
    n_batch, n_lat, d = x.shape
    n_ctx = ctx.shape[1]
    depth = w_ada.shape[0]
    rows_b = n_ctx + n_lat
    t = n_batch * rows_b
    branch_w = d // 2
    a_heads = branch_w // HEAD_DIM
    a_kv = a_heads // 4
    b_heads = branch_w // HEAD_DIM
    c_heads = branch_w // (2 * HEAD_DIM)
    n_experts = w_router.shape[-1]
    tm = MM_TILE_M if rows_b % MM_TILE_M == 0 else ROW_TILE
    tiles_per_batch = rows_b // tm
    assert n_ctx % ROW_TILE == 0 and n_lat % ROW_TILE == 0 and rows_b % SCAN_TILE == 0

    cos_t, sin_t = _rope_tables(n_ctx, n_lat)
    p_lb = jax.nn.softmax(hgrn_lb_logits.astype(f32), axis=0)
    lower_bounds = jnp.cumsum(p_lb, axis=0) - p_lb[0]
    cond = jnp.zeros((8, d), f32).at[:n_batch].set(c).at[n_batch].set(c_ctx)

    xa = jnp.concatenate([ctx, x], axis=1).reshape(t, d)

    def mod_specs(tn):
        ctx_spec = pl.BlockSpec((1, 1, tn), lambda i, j: (n_batch, 0, j))
        b_spec = pl.BlockSpec((1, 1, tn), lambda i, j: (i // tiles_per_batch, 0, j))
        return ctx_spec, b_spec

    splits = (a_heads * HEAD_DIM, a_kv * HEAD_DIM, a_kv * HEAD_DIM,
              branch_w, branch_w, branch_w, branch_w, branch_w,
              branch_w, branch_w, branch_w, 3 * d)
    offs = [0]
    for s in splits:
        offs.append(offs[-1] + s)

    def cols(w, *ids):
        return jnp.concatenate([w[:, offs[i]:offs[i + 1]] for i in ids], axis=1).astype(bf16)

    n_qk_heads = a_heads + a_kv + 4 * c_heads
    qk_tn = (n_qk_heads // 2) * HEAD_DIM if n_qk_heads % 2 == 0 else n_qk_heads * HEAD_DIM
    sm_scale = HEAD_DIM ** -0.5 * LOG2E

    for l in range(depth):
        lam_init = 0.8 - 0.6 * math.exp(-0.3 * l)
        mod = fused_matmul([(cond, w_ada)], [(b_ada[l].reshape(1, -1), pl.BlockSpec((1, 1024), lambda i, j: (0, j)))],
                           _ep_bias, jax.ShapeDtypeStruct((8, 6 * d), f32), _tile_spec(8, 1024), 8, 1024,
                           a_fn=_silu, name="ada_ln", layer=l)
        mod = mod[:n_batch + 1].reshape(n_batch + 1, 6, 1, d)
        mods = [mod[:, k] for k in range(6)]

        (h,) = norm_modulate(xa, norm_mix[l], mods[0], mods[1], rows_b, n_ctx)
        w_l = w_in[l]
        gains = jnp.concatenate([jnp.tile(q_norm_a[l] * sm_scale, a_heads), jnp.tile(k_norm_a[l], a_kv),
                                 jnp.tile(q_norm_c[l] * sm_scale, 2 * c_heads), jnp.tile(k_norm_c[l], 2 * c_heads)])
        qk = fused_matmul(
            [(h, cols(w_l, 0, 1, 8, 9))],
            [(gains.reshape(1, -1).astype(f32), pl.BlockSpec((1, qk_tn), lambda i, j: (0, j))),
             (cos_t, pl.BlockSpec((tm, HEAD_DIM), lambda i, j: (i % tiles_per_batch, 0))),
             (sin_t, pl.BlockSpec((tm, HEAD_DIM), lambda i, j: (i % tiles_per_batch, 0)))],
            _ep_qk, jax.ShapeDtypeStruct((n_qk_heads, t, HEAD_DIM), bf16),
            pl.BlockSpec((qk_tn // HEAD_DIM, tm, HEAD_DIM), lambda i, j: (j, i, 0)), tm, qk_tn, name="in_proj_qk")
        va = fused_matmul([(h, cols(w_l, 2))], [], _ep_store(bf16), jax.ShapeDtypeStruct((t, a_kv * HEAD_DIM), bf16),
                          _tile_spec(tm, a_kv * HEAD_DIM), tm, a_kv * HEAD_DIM, name="in_proj_va")
        vc = fused_matmul([(h, cols(w_l, 10))], [], _ep_store(bf16), jax.ShapeDtypeStruct((t, branch_w), bf16),
                          _tile_spec(tm, 512), tm, 512, name="in_proj_vc")
        qio = fused_matmul([(h, cols(w_l, 3, 4, 7))], [], _ep_store(bf16), jax.ShapeDtypeStruct((t, 3 * branch_w), bf16),
                           _tile_spec(tm, 512), tm, 512, name="in_proj_hgrn")
        zfb = fused_matmul([(h, cols(w_l, 5, 6))], [], _ep_store(f32), jax.ShapeDtypeStruct((t, 2 * branch_w), f32),
                           _tile_spec(tm, 512), tm, 512, name="in_proj_forget")
        gates = fused_matmul([(h, cols(w_l, 11))], [], _ep_sigmoid, jax.ShapeDtypeStruct((t, 3 * d), bf16),
                             _tile_spec(tm, 512), tm, 512, name="in_proj_gates")

        dims = dict(n_batch=n_batch, rows_per_batch=rows_b, n_ctx=n_ctx)
        o_a = attention_a(qk, va, a_heads=a_heads, a_kv_heads=a_kv, **dims)
        lam_params = jnp.stack([lambda_q1[l], lambda_k1[l], lambda_q2[l], lambda_k2[l]]).astype(f32)
        o_c = attention_c(qk, vc, lam_params, diff_sub_norm[l], head0_q=a_heads + a_kv,
                          head0_k=a_heads + a_kv + 2 * c_heads, c_heads=c_heads, lam_init=lam_init, **dims)
        o_f, o_r = hgrn_scan(qio, zfb, lower_bounds[l], n_batch=n_batch, rows_per_batch=rows_b, b_heads=b_heads)
        o_b = hgrn_readout(o_f, o_r, qio, hgrn_out_norm[l], b_heads)

        wb = w_branch[l].astype(bf16)
        n_col = d // 512
        y = fused_matmul(
            [(o_a, wb[0]), (o_b, wb[1]), (o_c, wb[2])],
            [(gates, pl.BlockSpec((tm, 512), lambda i, j, r=r: (i, r * n_col + j))) for r in range(3)],
            _ep_merge, jax.ShapeDtypeStruct((t, d), bf16), _tile_spec(tm, 512), tm, 512, name="merge")
        ctx_spec, b_spec = mod_specs(512)
        xa = fused_matmul(
            [(y, w_out[l].astype(bf16))],
            [(xa, _tile_spec(tm, 512)), (mods[2], ctx_spec), (mods[2], b_spec)],
            _make_ep_residual(tm, rows_b, n_ctx, False), jax.ShapeDtypeStruct((t, d), f32),
            _tile_spec(tm, 512), tm, 512, name="out_proj")

        h2, h2_packed = norm_modulate(xa, norm_ffn[l], mods[3], mods[4], rows_b, n_ctx, packed_rows=True)
        top_e, top_w, rank, counts = moe_route(h2, w_router[l].T.astype(bf16), b_router[l])
        dest, slot_tok, block_e = _moe_plan(top_e, rank, counts.reshape(-1))
        ys = moe_experts(h2_packed, slot_tok, w_exp_gate, w_exp_up, w_exp_down, l, block_e)
        routed = moe_combine(ys, dest, top_w)
        d_sh = w_sh_gate.shape[-1]
        act = fused_matmul([(h2, w_sh_gate[l].astype(bf16)), (h2, w_sh_up[l].astype(bf16))], [], _ep_glu,
                           jax.ShapeDtypeStruct((t, d_sh), bf16), _tile_spec(tm, d_sh), tm, d_sh, name="shared_glu")
        xa = fused_matmul(
            [(act, w_sh_down[l].astype(bf16))],
            [(xa, _tile_spec(tm, 512)), (mods[5], ctx_spec), (mods[5], b_spec), (routed, _tile_spec(tm, 512))],
            _make_ep_residual(tm, rows_b, n_ctx, True), jax.ShapeDtypeStruct((t, d), f32),
            _tile_spec(tm, 512), tm, 512, name="moe_out")

    return xa.reshape(n_batch, rows_b, d)[:, n_ctx:]
```

```python
import functools
import math

import jax
import jax.numpy as jnp
from jax import lax
from jax.experimental import pallas as pl
from jax.experimental.pallas import tpu as pltpu

f32 = jnp.float32
bf16 = jnp.bfloat16
i32 = jnp.int32

HEAD_DIM = 128
GRID_W = 64
ROPE_THETA = 10000.0
EPS = 1e-6
F_MIN = 1e-6
SCAN_CHUNK = 64
SCAN_SUB = 8
N_GROUPS = 8
TOPK_GROUPS = 4
TOP_K = 8
ROUTE_SCALE = 2.5
MASK_SCORE = -1e9
LOG2E = 1.4426950408889634

VMEM_LIMIT_BYTES = 48 * 1024 * 1024
ROW_TILE = 256
MM_TILE_M = 768
MOE_TILE = 256
SCAN_TILE = 256


def _params(*semantics, flags=None):
    return pltpu.CompilerParams(dimension_semantics=semantics, vmem_limit_bytes=VMEM_LIMIT_BYTES, flags=flags)


def _silu(x):
    return x * jax.nn.sigmoid(x)


def _nt_dot(a, b):
    return lax.dot_general(a, b, (((1,), (1,)), ((), ())), preferred_element_type=f32)


def _tn_dot(a, b):
    return lax.dot_general(a, b, (((0,), (0,)), ((), ())), preferred_element_type=f32)


def _row_is_ctx(tile_index, tile_rows, rows_per_batch, n_ctx):
    row0 = (tile_index % (rows_per_batch // tile_rows)) * tile_rows
    rid = row0 + lax.broadcasted_iota(i32, (tile_rows, 1), 0)
    return rid < n_ctx


def _fm_kernel(*refs, n_pairs, n_extra, a_fn, epilogue):
    pair_refs = refs[:2 * n_pairs]
    extra = refs[2 * n_pairs:2 * n_pairs + n_extra]
    outs = refs[2 * n_pairs + n_extra:]
    accs = []
    for p in range(n_pairs):
        a = pair_refs[2 * p][...]
        if a_fn is not None:
            a = a_fn(a)
        w_ref = pair_refs[2 * p + 1]
        w = w_ref[0] if len(w_ref.shape) == 3 else w_ref[...]
        accs.append(jnp.dot(a.astype(bf16), w.astype(bf16), preferred_element_type=f32))
    epilogue(accs, extra, outs)


def fused_matmul(pairs, extras, epilogue, out_shapes, out_specs, tm, tn, a_fn=None, name=None, layer=None):
    m = pairs[0][0].shape[0]
    n = pairs[0][1].shape[-1]
    assert m % tm == 0 and n % tn == 0, (m, tm, n, tn)
    in_specs, args = [], []
    for a, w in pairs:
        k = a.shape[1]
        assert w.shape[-2] == k
        if layer is None:
            w_spec = pl.BlockSpec((k, tn), lambda i, j: (0, j))
        else:
            w_spec = pl.BlockSpec((1, k, tn), lambda i, j: (layer, 0, j))
        in_specs += [pl.BlockSpec((tm, k), lambda i, j: (i, 0)), w_spec]
        args += [a, w]
    for arr, spec in extras:
        in_specs.append(spec)
        args.append(arr)
    return pl.pallas_call(
        functools.partial(_fm_kernel, n_pairs=len(pairs), n_extra=len(extras), a_fn=a_fn, epilogue=epilogue),
        grid=(m // tm, n // tn),
        in_specs=in_specs,
        out_specs=out_specs,
        out_shape=out_shapes,
        compiler_params=_params("parallel", "arbitrary"),
        name=name,
    )(*args)


def _tile_spec(tm, tn):
    return pl.BlockSpec((tm, tn), lambda i, j: (i, j))


def _ep_store(dtype):
    def ep(accs, extra, outs):
        outs[0][...] = accs[0].astype(dtype)
    return ep


def _ep_bias(accs, extra, outs):
    outs[0][...] = accs[0] + extra[0][...]


def _ep_sigmoid(accs, extra, outs):
    outs[0][...] = jax.nn.sigmoid(accs[0]).astype(outs[0].dtype)


def _ep_qk(accs, extra, outs):
    gain_ref, cos_ref, sin_ref = extra
    acc = accs[0]
    cos = cos_ref[...]
    sin = sin_ref[...]
    lane = lax.broadcasted_iota(i32, cos.shape, 1)
    first_half = (lane % (HEAD_DIM // 2)) < (HEAD_DIM // 4)
    for c in range(acc.shape[1] // HEAD_DIM):
        x = acc[:, c * HEAD_DIM:(c + 1) * HEAD_DIM]
        y = x * lax.rsqrt(jnp.mean(x * x, axis=-1, keepdims=True) + EPS)
        y = y * gain_ref[:, c * HEAD_DIM:(c + 1) * HEAD_DIM]
        partner = jnp.where(first_half,
                            pltpu.roll(y, HEAD_DIM - HEAD_DIM // 4, 1),
                            pltpu.roll(y, HEAD_DIM // 4, 1))
        outs[0][c] = (y * cos + partner * sin).astype(outs[0].dtype)


def _ep_merge(accs, extra, outs):
    y = extra[0][...].astype(f32) * accs[0]
    y = y + extra[1][...].astype(f32) * accs[1]
    y = y + extra[2][...].astype(f32) * accs[2]
    outs[0][...] = y.astype(outs[0].dtype)


def _ep_glu(accs, extra, outs):
    outs[0][...] = (_silu(accs[0]) * accs[1]).astype(outs[0].dtype)


def _make_ep_residual(tm, rows_per_batch, n_ctx, with_routed):
    def ep(accs, extra, outs):
        x_ref, mod_ctx_ref, mod_b_ref = extra[:3]
        y = accs[0]
        if with_routed:
            y = y + extra[3][...]
        is_ctx = _row_is_ctx(pl.program_id(0), tm, rows_per_batch, n_ctx)
        mod = jnp.where(is_ctx, mod_ctx_ref[0], mod_b_ref[0])
        outs[0][...] = x_ref[...] + mod * y
    return ep


PACK_ROWS = 8


def _norm_kernel(x_ref, w_ref, shc_ref, shb_ref, scc_ref, scb_ref, o_ref, *packed, tile, rows_per_batch, n_ctx):
    x = x_ref[...]
    y = x * lax.rsqrt(jnp.mean(x * x, axis=-1, keepdims=True) + EPS) * w_ref[...]
    is_ctx = _row_is_ctx(pl.program_id(0), tile, rows_per_batch, n_ctx)
    shift = jnp.where(is_ctx, shc_ref[0], shb_ref[0])
    scale = jnp.where(is_ctx, scc_ref[0], scb_ref[0])
    hb = (y * (1.0 + scale) + shift).astype(bf16)
    o_ref[...] = hb
    if packed:
        bits = lax.bitcast_convert_type(hb.astype(f32), jnp.uint32)
        half = bits.shape[1] // 2
        for s in range(PACK_ROWS):
            lo = bits[:, s * HEAD_DIM:(s + 1) * HEAD_DIM] >> 16
            hi = bits[:, half + s * HEAD_DIM:half + (s + 1) * HEAD_DIM]
            packed[0][pl.ds(s, tile, stride=PACK_ROWS), :] = hi | lo


def norm_modulate(x, w, shift, scale, rows_per_batch, n_ctx, packed_rows=False):
    t, d = x.shape
    tile = ROW_TILE
    n_b = shift.shape[0] - 1
    tiles_per_batch = rows_per_batch // tile
    ctx_spec = pl.BlockSpec((1, 1, d), lambda i: (n_b, 0, 0))
    b_spec = pl.BlockSpec((1, 1, d), lambda i: (i // tiles_per_batch, 0, 0))
    row_spec = pl.BlockSpec((tile, d), lambda i: (i, 0))
    out_specs, out_shape = [row_spec], [jax.ShapeDtypeStruct((t, d), bf16)]
    if packed_rows:
        assert d == 2 * PACK_ROWS * HEAD_DIM
        out_specs.append(pl.BlockSpec((tile * PACK_ROWS, HEAD_DIM), lambda i: (i, 0)))
        out_shape.append(jax.ShapeDtypeStruct((t * PACK_ROWS, HEAD_DIM), jnp.uint32))
    return pl.pallas_call(
        functools.partial(_norm_kernel, tile=tile, rows_per_batch=rows_per_batch, n_ctx=n_ctx),
        grid=(t // tile,),
        in_specs=[row_spec, pl.BlockSpec((1, d), lambda i: (0, 0)), ctx_spec, b_spec, ctx_spec, b_spec],
        out_specs=out_specs,
        out_shape=out_shape,
        compiler_params=_params("parallel"),
        name="norm_modulate",
    )(x, w.reshape(1, d), shift, shift, scale, scale)


def _attend(q, k_at, v_at, kv_len, tk, dv):
    m = l = acc = None
    n = kv_len // tk
    s_next = _nt_dot(q, k_at(0))
    for c in range(n):
        s = s_next
        if c + 1 < n:
            s_next = _nt_dot(q, k_at((c + 1) * tk))
        s_max = jnp.max(s, axis=1, keepdims=True)
        if c == 0:
            m = s_max
            p = jnp.exp2(s - m)
            l = jnp.sum(p, axis=1, keepdims=True)
            acc = jnp.dot(p.astype(bf16), v_at(c * tk), preferred_element_type=f32)
        else:
            m_new = jnp.maximum(m, s_max)
            alpha = jnp.exp2(m - m_new)
            p = jnp.exp2(s - m_new)
            l = alpha * l + jnp.sum(p, axis=1, keepdims=True)
            acc = alpha * acc + jnp.dot(p.astype(bf16), v_at(c * tk), preferred_element_type=f32)
            m = m_new
    return acc / l


def _kv_chunk(kv_len):
    for tk in (768, 1408, 512, 256, 128):
        if kv_len % tk == 0:
            return tk
    raise ValueError(kv_len)


def _ctx_or_all(n_ctx_tiles, n_ctx, kv_all, run):
    is_ctx = pl.program_id(2) < n_ctx_tiles

    @pl.when(is_ctx)
    def _():
        run(n_ctx)

    @pl.when(jnp.logical_not(is_ctx))
    def _():
        run(kv_all)


def _attn_a_kernel(q_ref, k_ref, v_ref, o_ref, *, n_ctx):
    g, tq, d = q_ref.shape

    def run(kv_len):
        tk = _kv_chunk(kv_len)
        q = q_ref[...].reshape(g * tq, d)
        o = _attend(q, lambda off: k_ref[0, off:off + tk, :], lambda off: v_ref[off:off + tk, :], kv_len, tk, d)
        for h in range(g):
            o_ref[:, h * d:(h + 1) * d] = o[h * tq:(h + 1) * tq].astype(o_ref.dtype)

    _ctx_or_all(n_ctx // tq, n_ctx, k_ref.shape[1], run)


def _attn_c_kernel(lam_ref, dn_ref, q_ref, k_ref, v_ref, o_ref, *, n_ctx, lam_init):
    tq = q_ref.shape[1]
    dv = v_ref.shape[-1]

    def run(kv_len):
        tk = _kv_chunk(kv_len)
        o1 = _attend(q_ref[0], lambda off: k_ref[0, off:off + tk, :], lambda off: v_ref[off:off + tk, :],
                     kv_len, tk, dv)
        o2 = _attend(q_ref[1], lambda off: k_ref[1, off:off + tk, :], lambda off: v_ref[off:off + tk, :],
                     kv_len, tk, dv)
        lam_p = lam_ref[...]
        lam = (jnp.exp(jnp.sum(lam_p[0:1] * lam_p[1:2], axis=-1, keepdims=True))
               - jnp.exp(jnp.sum(lam_p[2:3] * lam_p[3:4], axis=-1, keepdims=True)) + lam_init)
        o = o1 - lam * o2
        y = o * lax.rsqrt(jnp.mean(o * o, axis=-1, keepdims=True) + EPS) * dn_ref[...]
        o_ref[...] = (y * (1.0 - lam_init)).astype(o_ref.dtype)

    _ctx_or_all(n_ctx // tq, n_ctx, k_ref.shape[1], run)


def attention_a(qk, va, *, n_batch, rows_per_batch, n_ctx, a_heads, a_kv_heads):
    t = qk.shape[1]
    group = a_heads // a_kv_heads
    tq = ROW_TILE
    n_q = rows_per_batch // tq
    return pl.pallas_call(
        functools.partial(_attn_a_kernel, n_ctx=n_ctx),
        grid=(n_batch, a_kv_heads, n_q),
        in_specs=[
            pl.BlockSpec((group, tq, HEAD_DIM), lambda b, h, i: (h, b * n_q + i, 0)),
            pl.BlockSpec((1, rows_per_batch, HEAD_DIM), lambda b, h, i: (a_heads + h, b, 0)),
            pl.BlockSpec((rows_per_batch, HEAD_DIM), lambda b, h, i: (b, h)),
        ],
        out_specs=pl.BlockSpec((tq, group * HEAD_DIM), lambda b, h, i: (b * n_q + i, h)),
        out_shape=jax.ShapeDtypeStruct((t, a_heads * HEAD_DIM), bf16),
        compiler_params=_params("parallel", "parallel", "arbitrary"),
        name="attention_a",
    )(qk, qk, va)


def attention_c(qk, vc, lam_params, diff_norm, *, n_batch, rows_per_batch, n_ctx, head0_q, head0_k,
                c_heads, lam_init):
    t = qk.shape[1]
    dv = 2 * HEAD_DIM
    tq = ROW_TILE
    n_q = rows_per_batch // tq
    return pl.pallas_call(
        functools.partial(_attn_c_kernel, n_ctx=n_ctx, lam_init=lam_init),
        grid=(n_batch, c_heads, n_q),
        in_specs=[
            pl.BlockSpec((4, HEAD_DIM), lambda b, h, i: (0, 0)),
            pl.BlockSpec((1, dv), lambda b, h, i: (0, 0)),
            pl.BlockSpec((2, tq, HEAD_DIM), lambda b, h, i: (head0_q // 2 + h, b * n_q + i, 0)),
            pl.BlockSpec((2, rows_per_batch, HEAD_DIM), lambda b, h, i: (head0_k // 2 + h, b, 0)),
            pl.BlockSpec((rows_per_batch, dv), lambda b, h, i: (b, h)),
        ],
        out_specs=pl.BlockSpec((tq, dv), lambda b, h, i: (b * n_q + i, h)),
        out_shape=jax.ShapeDtypeStruct((t, c_heads * dv), bf16),
        compiler_params=_params("parallel", "parallel", "arbitrary"),
        name="attention_c",
    )(lam_params, diff_norm.reshape(1, dv), qk, qk, vc)


def _hgrn_tables(rev):
    c = SCAN_CHUNK
    half = SCAN_SUB // 2
    t_i = lax.broadcasted_iota(i32, (c, c), 0)
    u_i = lax.broadcasted_iota(i32, (c, c), 1)
    blk8 = t_i & ~(SCAN_SUB - 1)
    blk4 = t_i & ~(half - 1)
    if rev:
        rows_upto = [t_i, blk4 + half - 1, blk8 + half, blk8]
        sums = jnp.concatenate([(u_i >= r).astype(f32) for r in rows_upto], axis=0)
    else:
        rows_upto = [t_i, blk4, blk8 + half - 1, blk8 + SCAN_SUB - 1]
        sums = jnp.concatenate([(u_i <= r).astype(f32) for r in rows_upto], axis=0)
    causal = (u_i >= t_i) if rev else (u_i <= t_i)
    same8 = (u_i & ~(SCAN_SUB - 1)) == blk8
    same4 = (u_i & ~(half - 1)) == blk4
    return sums, same8 & jnp.logical_not(same4) & causal, same4 & causal


def _hgrn_gates(qp, z, lb):
    q = _silu(qp.astype(f32))
    one_m = 1.0 - lb
    g = jnp.log(jnp.maximum(lb + one_m * jax.nn.sigmoid(z), F_MIN))
    k = one_m * jax.nn.sigmoid(-z)
    return q, k, g


def _hgrn_local(q, k, b_all, v, tables, rev):
    c = SCAN_CHUNK
    nb = c // SCAN_SUB
    _, across, inner = tables
    b, rho4, rho_mid, rho_edge = (b_all[i * c:(i + 1) * c] for i in range(4))
    end = 0 if rev else c - 1
    b_end = b[end:end + 1, :]
    q_in = (q * jnp.exp(b)).astype(bf16)
    d_st = _tn_dot(v, (k * jnp.exp(b_end - b)).astype(bf16))

    def edge_row(j):
        return j * SCAN_SUB if rev else j * SCAN_SUB + SCAN_SUB - 1

    row = lax.broadcasted_iota(i32, (c, HEAD_DIM), 0)
    k_til = k * jnp.exp(rho_edge - b)
    qs, ks = [], []
    for j in (range(1, nb) if rev else range(nb - 1)):
        rho = b[edge_row(j):edge_row(j) + 1, :]
        q_rows = (row < j * SCAN_SUB) if rev else (row >= (j + 1) * SCAN_SUB)
        qs.append((q * jnp.exp(jnp.where(q_rows, b - rho, -jnp.inf))).astype(bf16))
        k_rows = (row >= j * SCAN_SUB) & (row < (j + 1) * SCAN_SUB)
        ks.append(jnp.where(k_rows, k_til, 0.0).astype(bf16))
    att = _nt_dot(jnp.concatenate(qs, axis=1), jnp.concatenate(ks, axis=1))

    att_mid = _nt_dot((q * jnp.exp(jnp.minimum(b - rho_mid, 0.0))).astype(bf16),
                      (k * jnp.exp(jnp.minimum(rho_mid - b, 0.0))).astype(bf16))
    att_in = _nt_dot((q * jnp.exp(b - rho4)).astype(bf16), (k * jnp.exp(rho4 - b)).astype(bf16))
    att = att + jnp.where(across, att_mid, 0.0) + jnp.where(inner, att_in, 0.0)
    return jnp.dot(att.astype(bf16), v, preferred_element_type=f32), q_in, d_st, jnp.exp(b_end)


def _hgrn_kernel(lb_ref, qf_ref, vf_ref, zf_ref, qb_ref, vb_ref, zb_ref, of_ref, ob_ref, sf_ref, sb_ref):
    @pl.when(pl.program_id(2) == 0)
    def _():
        sf_ref[...] = jnp.zeros_like(sf_ref)
        sb_ref[...] = jnp.zeros_like(sb_ref)

    lb = lb_ref[...]
    n_chunks = qf_ref.shape[0] // SCAN_CHUNK
    work = []
    for c in range(n_chunks):
        work.append((slice(c * SCAN_CHUNK, (c + 1) * SCAN_CHUNK), False))
        work.append((slice((n_chunks - 1 - c) * SCAN_CHUNK, (n_chunks - c) * SCAN_CHUNK), True))
    refs = {False: (qf_ref, vf_ref, zf_ref, of_ref), True: (qb_ref, vb_ref, zb_ref, ob_ref)}
    tables = {False: _hgrn_tables(False), True: _hgrn_tables(True)}

    gates = [_hgrn_gates(refs[rev][0][rows, :], refs[rev][2][rows, :], lb) for rows, rev in work]
    sums = [jnp.dot(tables[rev][0], g, precision=lax.Precision.HIGHEST, preferred_element_type=f32)
            for (_, rev), (_, _, g) in zip(work, gates)]
    local = [_hgrn_local(q, k, b_all, refs[rev][1][rows, :], tables[rev], rev)
             for (rows, rev), (q, k, _), b_all in zip(work, gates, sums)]
    state = {False: sf_ref[...], True: sb_ref[...]}
    for (rows, rev), (o_in, q_in, d_st, decay) in zip(work, local):
        st = state[rev]
        refs[rev][3][rows, :] = o_in + _nt_dot(q_in, st.astype(bf16))
        state[rev] = st * decay + d_st
    sf_ref[...] = state[False]
    sb_ref[...] = state[True]


def hgrn_scan(qio, zfb, lower_bound, *, n_batch, rows_per_batch, b_heads):
    t = qio.shape[0]
    tile = SCAN_TILE
    n_tiles = rows_per_batch // tile

    def fwd(b, h, j):
        return b * n_tiles + j

    def bwd(b, h, j):
        return b * n_tiles + jnp.where(j == 0, 0, n_tiles - j)

    def spec(tile_fn, col0):
        return pl.BlockSpec((tile, HEAD_DIM), lambda b, h, j: (tile_fn(b, h, j), col0 + h))

    return pl.pallas_call(
        _hgrn_kernel,
        grid=(n_batch, b_heads, n_tiles),
        in_specs=[
            pl.BlockSpec((1, HEAD_DIM), lambda b, h, j: (0, h)),
            spec(fwd, 0), spec(fwd, b_heads), spec(fwd, 0),
            spec(bwd, 0), spec(bwd, b_heads), spec(bwd, b_heads),
        ],
        out_specs=[spec(fwd, 0), spec(bwd, 0)],
        out_shape=[jax.ShapeDtypeStruct((t, b_heads * HEAD_DIM), f32)] * 2,
        scratch_shapes=[pltpu.VMEM((HEAD_DIM, HEAD_DIM), f32)] * 2,
        compiler_params=_params("parallel", "parallel", "arbitrary"),
        name="hgrn_scan",
    )(lower_bound.reshape(1, -1), qio, qio, zfb, qio, qio, zfb)


def _hgrn_out_kernel(of_ref, ob_ref, og_ref, w_ref, o_ref):
    o = of_ref[...] + ob_ref[...]
    w = w_ref[...]
    for c in range(o.shape[1] // HEAD_DIM):
        sl = slice(c * HEAD_DIM, (c + 1) * HEAD_DIM)
        x = o[:, sl]
        y = x * lax.rsqrt(jnp.mean(x * x, axis=-1, keepdims=True) + EPS) * w
        o_ref[:, sl] = (y * _silu(og_ref[:, sl].astype(f32))).astype(o_ref.dtype)


def hgrn_readout(o_f, o_b, qio, hgrn_norm, b_heads):
    t, w = o_f.shape
    tile = ROW_TILE
    row_spec = pl.BlockSpec((tile, w), lambda i: (i, 0))
    return pl.pallas_call(
        _hgrn_out_kernel,
        grid=(t // tile,),
        in_specs=[row_spec, row_spec, pl.BlockSpec((tile, w), lambda i: (i, 2)),
                  pl.BlockSpec((1, HEAD_DIM), lambda i: (0, 0))],
        out_specs=row_spec,
        out_shape=jax.ShapeDtypeStruct((t, w), bf16),
        compiler_params=_params("parallel"),
        name="hgrn_readout",
    )(o_f, o_b, qio, hgrn_norm.reshape(1, HEAD_DIM))


def _router_kernel(h_ref, w_ref, b_ref, e_ref, wt_ref, rank_ref, cnt_ref, run_ref):
    n_e = w_ref.shape[0]
    tm = h_ref.shape[0]
    per = n_e // N_GROUPS

    @pl.when(pl.program_id(0) == 0)
    def _():
        run_ref[...] = jnp.zeros_like(run_ref)

    scores = jax.nn.sigmoid(_nt_dot(w_ref[...], h_ref[...]))
    biased = scores + b_ref[...]
    neg = -jnp.inf

    grp = biased.reshape(N_GROUPS, per, tm)
    idx_in = lax.broadcasted_iota(i32, grp.shape, 1)
    m1 = jnp.max(grp, axis=1, keepdims=True)
    first = jnp.min(jnp.where(grp == m1, idx_in, per), axis=1, keepdims=True)
    m2 = jnp.max(jnp.where(idx_in == first, neg, grp), axis=1, keepdims=True)
    cur = m1 + m2

    g_idx = lax.broadcasted_iota(i32, cur.shape, 0)
    g_sel = jnp.zeros(cur.shape, f32)
    for _ in range(TOPK_GROUPS):
        mx = jnp.max(cur, axis=0, keepdims=True)
        pick = g_idx == jnp.min(jnp.where(cur == mx, g_idx, N_GROUPS), axis=0, keepdims=True)
        g_sel = jnp.where(pick, 1.0, g_sel)
        cur = jnp.where(pick, neg, cur)
    e_mask = jnp.broadcast_to(g_sel, (N_GROUPS, per, tm)).reshape(n_e, tm) > 0.0

    e_idx = lax.broadcasted_iota(i32, (n_e, tm), 0)
    cur = jnp.where(e_mask, biased, MASK_SCORE)
    ws, picks = [], []
    member = jnp.zeros((n_e, tm), f32)
    for r in range(TOP_K):
        mx = jnp.max(cur, axis=0, keepdims=True)
        first = jnp.min(jnp.where(cur == mx, e_idx, n_e), axis=0, keepdims=True)
        pick = e_idx == first
        e_ref[r:r + 1, :] = first
        ws.append(jnp.sum(jnp.where(pick, scores, 0.0), axis=0, keepdims=True))
        cur = jnp.where(pick, neg, cur)
        member = jnp.where(pick, 1.0, member)
        picks.append(pick)
    total = ws[0]
    for r in range(1, TOP_K):
        total = total + ws[r]
    for r in range(TOP_K):
        wt_ref[r:r + 1, :] = ws[r] / total * ROUTE_SCALE

    u_i = lax.broadcasted_iota(i32, (tm, tm), 0)
    t_i = lax.broadcasted_iota(i32, (tm, tm), 1)
    prefix = jnp.dot(member.astype(bf16), (u_i <= t_i).astype(bf16), preferred_element_type=f32)
    rank = run_ref[...] + prefix
    for r in range(TOP_K):
        rank_ref[r:r + 1, :] = jnp.sum(jnp.where(picks[r], rank, 0.0), axis=0, keepdims=True).astype(i32)
    run_ref[...] = run_ref[...] + jnp.sum(member, axis=1, keepdims=True)
    cnt_ref[...] = run_ref[...].astype(i32)


def moe_route(h, w_router_t, b_router):
    t, d = h.shape
    n_e = w_router_t.shape[0]
    tm = 512
    assert t % tm == 0
    out_spec = pl.BlockSpec((TOP_K, tm), lambda i: (0, i))
    return pl.pallas_call(
        _router_kernel,
        grid=(t // tm,),
        in_specs=[pl.BlockSpec((tm, d), lambda i: (i, 0)), pl.BlockSpec((n_e, d), lambda i: (0, 0)),
                  pl.BlockSpec((n_e, 1), lambda i: (0, 0))],
        out_specs=[out_spec, out_spec, out_spec, pl.BlockSpec((n_e, 1), lambda i: (0, 0))],
        out_shape=[jax.ShapeDtypeStruct((TOP_K, t), i32), jax.ShapeDtypeStruct((TOP_K, t), f32),
                   jax.ShapeDtypeStruct((TOP_K, t), i32), jax.ShapeDtypeStruct((n_e, 1), i32)],
        scratch_shapes=[pltpu.VMEM((n_e, 1), f32)],
        compiler_params=_params("arbitrary"),
        name="moe_route",
    )(h, w_router_t, b_router.reshape(n_e, 1).astype(f32))


def _row_copy(src_ref, dst_ref, src_row, dst_row, sem):
    return pltpu.make_async_copy(src_ref.at[pl.ds(src_row, 1)], dst_ref.at[pl.ds(dst_row, 1)], sem)


GATHER_DEPTH = 3


def _expert_kernel(be_ref, idx0_ref, idx1_ref, idx_ahead_ref, h_ref, wg_ref, wu_ref, wd_ref, y_ref,
                   xbuf, sem, x_s, wg_b, wu_b, wd_b):
    i = pl.program_id(0)
    last = pl.num_programs(0) - 1
    tm = xbuf.shape[1] // PACK_ROWS
    slot = i % GATHER_DEPTH
    ahead = (i + GATHER_DEPTH - 1) % GATHER_DEPTH

    def gather_start(idx_ref, dst_slot):
        for r in range(tm):
            src_row = pl.multiple_of(idx_ref[0, 0, r] * PACK_ROWS, PACK_ROWS)
            pltpu.make_async_copy(h_ref.at[pl.ds(src_row, PACK_ROWS)],
                                  xbuf.at[dst_slot, pl.ds(r * PACK_ROWS, PACK_ROWS)], sem.at[dst_slot]).start()

    def gather_wait(dst_slot):
        pltpu.make_async_copy(h_ref.at[pl.ds(0, tm * PACK_ROWS)], xbuf.at[dst_slot], sem.at[dst_slot]).wait()

    @pl.when(i == 0)
    def _():
        gather_start(idx0_ref, 0)
        gather_start(idx1_ref, 1)

    @pl.when((i == 0) | (be_ref[i] != be_ref[jnp.maximum(i - 1, 0)]))
    def _():
        wg_b[...] = wg_ref[0, 0].astype(bf16)
        wu_b[...] = wu_ref[0, 0].astype(bf16)
        wd_b[...] = wd_ref[0, 0].astype(bf16)

    gather_wait(slot)
    half = PACK_ROWS * HEAD_DIM
    for s in range(PACK_ROWS):
        word = xbuf[slot, pl.ds(s, tm, stride=PACK_ROWS), :]
        x_s[:, s * HEAD_DIM:(s + 1) * HEAD_DIM] = lax.bitcast_convert_type(word << 16, f32).astype(bf16)
        x_s[:, half + s * HEAD_DIM:half + (s + 1) * HEAD_DIM] = lax.bitcast_convert_type(
            word & jnp.uint32(0xFFFF0000), f32).astype(bf16)
    gather_start(idx_ahead_ref, ahead)
    x = x_s[...]
    g = jnp.dot(x, wg_b[...], preferred_element_type=f32)
    u = jnp.dot(x, wu_b[...], preferred_element_type=f32)
    a = (_silu(g) * u).astype(bf16)
    y_ref[...] = jnp.dot(a, wd_b[...], preferred_element_type=f32)

    @pl.when(i == last)
    def _():
        for k in range(1, GATHER_DEPTH):
            gather_wait((i + k) % GATHER_DEPTH)


def moe_experts(h_packed, slot_tok, w_gate, w_up, w_down, layer, block_e):
    tm = MOE_TILE
    n_blocks = slot_tok.shape[0] // tm
    assert n_blocks >= GATHER_DEPTH
    d, d_e = w_gate.shape[-2:]
    idx = slot_tok.reshape(n_blocks, 1, tm)

    def idx_spec(block_fn):
        return pl.BlockSpec((1, 1, tm), lambda i, be: (block_fn(i), 0, 0), memory_space=pltpu.SMEM)

    def w_spec(shape):
        return pl.BlockSpec((1, 1) + shape, lambda i, be: (layer, be[i], 0, 0))

    return pl.pallas_call(
        _expert_kernel,
        grid_spec=pltpu.PrefetchScalarGridSpec(
            num_scalar_prefetch=1,
            grid=(n_blocks,),
            in_specs=[idx_spec(lambda i: 0), idx_spec(lambda i: 1),
                      idx_spec(lambda i: jnp.minimum(i + GATHER_DEPTH - 1, n_blocks - 1)),
                      pl.BlockSpec(memory_space=pl.ANY),
                      w_spec((d, d_e)), w_spec((d, d_e)), w_spec((d_e, d))],
            out_specs=pl.BlockSpec((tm, d), lambda i, be: (i, 0)),
            scratch_shapes=[pltpu.VMEM((GATHER_DEPTH, tm * PACK_ROWS, HEAD_DIM), jnp.uint32),
                            pltpu.SemaphoreType.DMA((GATHER_DEPTH,)), pltpu.VMEM((tm, d), bf16),
                            pltpu.VMEM((d, d_e), bf16), pltpu.VMEM((d, d_e), bf16), pltpu.VMEM((d_e, d), bf16)],
        ),
        out_shape=jax.ShapeDtypeStruct((n_blocks * tm, d), f32),
        compiler_params=_params("arbitrary"),
        name="moe_experts",
    )(block_e, idx, idx, idx, h_packed, w_gate, w_up, w_down)


def _combine_kernel(idx_ref, idx_next_ref, w_ref, y_ref, o_ref, buf, sem):
    i = pl.program_id(0)
    last = pl.num_programs(0) - 1
    _, n_k, rows, _ = buf.shape
    slot = i % 2

    def gather_start(src_idx_ref, dst_slot):
        def issue(r, carry):
            for kk in range(n_k):
                _row_copy(y_ref, buf.at[dst_slot, kk], src_idx_ref[0, kk, r], r, sem.at[dst_slot]).start()
            return carry
        lax.fori_loop(0, rows, issue, 0, unroll=8)

    def gather_wait(dst_slot):
        for kk in range(n_k):
            pltpu.make_async_copy(y_ref.at[pl.ds(0, rows)], buf.at[dst_slot, kk], sem.at[dst_slot]).wait()

    @pl.when(i == 0)
    def _():
        gather_start(idx_ref, 0)

    gather_start(idx_next_ref, 1 - slot)
    gather_wait(slot)
    w = w_ref[...]
    acc = buf[slot, 0] * w[:, 0:1]
    for kk in range(1, n_k):
        acc = acc + buf[slot, kk] * w[:, kk:kk + 1]
    o_ref[...] = acc

    @pl.when(i == last)
    def _():
        gather_wait(1 - slot)


def moe_combine(y, dest, top_w):
    n_k, t = dest.shape
    d = y.shape[1]
    rows = 128
    n_tiles = t // rows
    idx = dest.reshape(n_k, n_tiles, rows).transpose(1, 0, 2)

    def idx_spec(tile_fn):
        return pl.BlockSpec((1, n_k, rows), lambda i: (tile_fn(i), 0, 0), memory_space=pltpu.SMEM)

    return pl.pallas_call(
        _combine_kernel,
        grid=(n_tiles,),
        in_specs=[idx_spec(lambda i: i), idx_spec(lambda i: jnp.minimum(i + 1, n_tiles - 1)),
                  pl.BlockSpec((rows, n_k), lambda i: (i, 0)),
                  pl.BlockSpec(memory_space=pl.ANY)],
        out_specs=pl.BlockSpec((rows, d), lambda i: (i, 0)),
        out_shape=jax.ShapeDtypeStruct((t, d), f32),
        scratch_shapes=[pltpu.VMEM((2, n_k, rows, d), f32), pltpu.SemaphoreType.DMA((2,))],
        compiler_params=_params("arbitrary"),
        name="moe_combine",
    )(idx, idx, top_w.T, y)


def _moe_plan(top_e, rank, counts):
    n_k, t = top_e.shape
    n_experts = counts.shape[0]
    tm = MOE_TILE
    n_blocks = -(-(n_k * t) // tm) + n_experts
    padded = (counts + tm - 1) // tm * tm
    pad_end = jnp.cumsum(padded)
    pad_start = pad_end - padded
    onehot = top_e[:, :, None] == jnp.arange(n_experts, dtype=i32)
    dest = jnp.sum(jnp.where(onehot, pad_start, 0), axis=-1) + rank - 1
    tok = jnp.broadcast_to(jnp.arange(t, dtype=i32)[None, :], (n_k, t))
    slot_tok = jnp.zeros((n_blocks * tm,), i32).at[dest.reshape(-1)].set(tok.reshape(-1), unique_indices=True)
    n_used = pad_end[-1] // tm
    blk = jnp.minimum(jnp.arange(n_blocks, dtype=i32), n_used - 1)
    block_e = jnp.sum((pad_end[None, :] <= (blk * tm)[:, None]).astype(i32), axis=1)
    block_e = jnp.minimum(block_e, n_experts - 1)
    return dest.astype(i32), slot_tok, block_e


def _rope_tables(n_ctx, n_lat):
    t = jnp.arange(n_lat)
    pos = jnp.stack([t // GRID_W, t % GRID_W], axis=-1).astype(f32)
    n_freq = HEAD_DIM // 4
    inv_freq = ROPE_THETA ** (-jnp.arange(n_freq, dtype=f32) / n_freq)
    ang = pos[:, :, None] * inv_freq
    cos, sin = jnp.cos(ang), jnp.sin(ang)
    cos_t = jnp.concatenate([cos[:, 0], cos[:, 0], cos[:, 1], cos[:, 1]], axis=-1)
    sin_t = jnp.concatenate([-sin[:, 0], sin[:, 0], -sin[:, 1], sin[:, 1]], axis=-1)
    cos_t = jnp.concatenate([jnp.ones((n_ctx, HEAD_DIM), f32), cos_t], axis=0)
    sin_t = jnp.concatenate([jnp.zeros((n_ctx, HEAD_DIM), f32), sin_t], axis=0)
    return cos_t, sin_t


def kernel(x, c, ctx, c_ctx, w_ada, b_ada, norm_mix, norm_ffn, w_in, q_norm_a, k_norm_a, hgrn_lb_logits, hgrn_out_norm, q_norm_c, k_norm_c, lambda_q1, lambda_k1, lambda_q2, lambda_k2, diff_sub_norm, w_branch, w_out, w_router, b_router, w_exp_gate, w_exp_up, w_exp_down, w_sh_gate, w_sh_up, w_sh_down):
    n_batch, n_lat, d = x.shape
    n_ctx = ctx.shape[1]
    depth = w_ada.shape[0]
    rows_b = n_ctx + n_lat
    t = n_batch * rows_b
    branch_w = d // 2
    a_heads = branch_w // HEAD_DIM
    a_kv = a_heads // 4
    b_heads = branch_w // HEAD_DIM
    c_heads = branch_w // (2 * HEAD_DIM)
    n_experts = w_router.shape[-1]
    tm = MM_TILE_M if rows_b % MM_TILE_M == 0 else ROW_TILE
    tiles_per_batch = rows_b // tm
    assert n_ctx % ROW_TILE == 0 and n_lat % ROW_TILE == 0 and rows_b % SCAN_TILE == 0

    cos_t, sin_t = _rope_tables(n_ctx, n_lat)
    p_lb = jax.nn.softmax(hgrn_lb_logits.astype(f32), axis=0)
    lower_bounds = jnp.cumsum(p_lb, axis=0) - p_lb[0]
    cond = jnp.zeros((8, d), f32).at[:n_batch].set(c).at[n_batch].set(c_ctx)

    xa = jnp.concatenate([ctx, x], axis=1).reshape(t, d)

    def mod_specs(tn):
        ctx_spec = pl.BlockSpec((1, 1, tn), lambda i, j: (n_batch, 0, j))
        b_spec = pl.BlockSpec((1, 1, tn), lambda i, j: (i // tiles_per_batch, 0, j))
        return ctx_spec, b_spec

    splits = (a_heads * HEAD_DIM, a_kv * HEAD_DIM, a_kv * HEAD_DIM,
              branch_w, branch_w, branch_w, branch_w, branch_w,
              branch_w, branch_w, branch_w, 3 * d)
    offs = [0]
    for s in splits:
        offs.append(offs[-1] + s)

    def cols(w, *ids):
        return jnp.concatenate([w[:, offs[i]:offs[i + 1]] for i in ids], axis=1).astype(bf16)

    n_qk_heads = a_heads + a_kv + 4 * c_heads
    qk_tn = (n_qk_heads // 2) * HEAD_DIM if n_qk_heads % 2 == 0 else n_qk_heads * HEAD_DIM
    sm_scale = HEAD_DIM ** -0.5 * LOG2E

    for l in range(depth):
        lam_init = 0.8 - 0.6 * math.exp(-0.3 * l)
        mod = fused_matmul([(cond, w_ada)], [(b_ada[l].reshape(1, -1), pl.BlockSpec((1, 1024), lambda i, j: (0, j)))],
                           _ep_bias, jax.ShapeDtypeStruct((8, 6 * d), f32), _tile_spec(8, 1024), 8, 1024,
                           a_fn=_silu, name="ada_ln", layer=l)
        mod = mod[:n_batch + 1].reshape(n_batch + 1, 6, 1, d)
        mods = [mod[:, k] for k in range(6)]

        (h,) = norm_modulate(xa, norm_mix[l], mods[0], mods[1], rows_b, n_ctx)
        w_l = w_in[l]
        gains = jnp.concatenate([jnp.tile(q_norm_a[l] * sm_scale, a_heads), jnp.tile(k_norm_a[l], a_kv),
                                 jnp.tile(q_norm_c[l] * sm_scale, 2 * c_heads), jnp.tile(k_norm_c[l], 2 * c_heads)])
        qk = fused_matmul(
            [(h, cols(w_l, 0, 1, 8, 9))],
            [(gains.reshape(1, -1).astype(f32), pl.BlockSpec((1, qk_tn), lambda i, j: (0, j))),
             (cos_t, pl.BlockSpec((tm, HEAD_DIM), lambda i, j: (i % tiles_per_batch, 0))),
             (sin_t, pl.BlockSpec((tm, HEAD_DIM), lambda i, j: (i % tiles_per_batch, 0)))],
            _ep_qk, jax.ShapeDtypeStruct((n_qk_heads, t, HEAD_DIM), bf16),
            pl.BlockSpec((qk_tn // HEAD_DIM, tm, HEAD_DIM), lambda i, j: (j, i, 0)), tm, qk_tn, name="in_proj_qk")
        va = fused_matmul([(h, cols(w_l, 2))], [], _ep_store(bf16), jax.ShapeDtypeStruct((t, a_kv * HEAD_DIM), bf16),
                          _tile_spec(tm, a_kv * HEAD_DIM), tm, a_kv * HEAD_DIM, name="in_proj_va")
        vc = fused_matmul([(h, cols(w_l, 10))], [], _ep_store(bf16), jax.ShapeDtypeStruct((t, branch_w), bf16),
                          _tile_spec(tm, 512), tm, 512, name="in_proj_vc")
        qio = fused_matmul([(h, cols(w_l, 3, 4, 7))], [], _ep_store(bf16), jax.ShapeDtypeStruct((t, 3 * branch_w), bf16),
                           _tile_spec(tm, 512), tm, 512, name="in_proj_hgrn")
        zfb = fused_matmul([(h, cols(w_l, 5, 6))], [], _ep_store(f32), jax.ShapeDtypeStruct((t, 2 * branch_w), f32),
                           _tile_spec(tm, 512), tm, 512, name="in_proj_forget")
        gates = fused_matmul([(h, cols(w_l, 11))], [], _ep_sigmoid, jax.ShapeDtypeStruct((t, 3 * d), bf16),
                             _tile_spec(tm, 512), tm, 512, name="in_proj_gates")

        dims = dict(n_batch=n_batch, rows_per_batch=rows_b, n_ctx=n_ctx)
        o_a = attention_a(qk, va, a_heads=a_heads, a_kv_heads=a_kv, **dims)
        lam_params = jnp.stack([lambda_q1[l], lambda_k1[l], lambda_q2[l], lambda_k2[l]]).astype(f32)
        o_c = attention_c(qk, vc, lam_params, diff_sub_norm[l], head0_q=a_heads + a_kv,
                          head0_k=a_heads + a_kv + 2 * c_heads, c_heads=c_heads, lam_init=lam_init, **dims)
        o_f, o_r = hgrn_scan(qio, zfb, lower_bounds[l], n_batch=n_batch, rows_per_batch=rows_b, b_heads=b_heads)
        o_b = hgrn_readout(o_f, o_r, qio, hgrn_out_norm[l], b_heads)

        wb = w_branch[l].astype(bf16)
        n_col = d // 512
        y = fused_matmul(
            [(o_a, wb[0]), (o_b, wb[1]), (o_c, wb[2])],
            [(gates, pl.BlockSpec((tm, 512), lambda i, j, r=r: (i, r * n_col + j))) for r in range(3)],
            _ep_merge, jax.ShapeDtypeStruct((t, d), bf16), _tile_spec(tm, 512), tm, 512, name="merge")
        ctx_spec, b_spec = mod_specs(512)
        xa = fused_matmul(
            [(y, w_out[l].astype(bf16))],
            [(xa, _tile_spec(tm, 512)), (mods[2], ctx_spec), (mods[2], b_spec)],
            _make_ep_residual(tm, rows_b, n_ctx, False), jax.ShapeDtypeStruct((t, d), f32),
            _tile_spec(tm, 512), tm, 512, name="out_proj")

        h2, h2_packed = norm_modulate(xa, norm_ffn[l], mods[3], mods[4], rows_b, n_ctx, packed_rows=True)
        top_e, top_w, rank, counts = moe_route(h2, w_router[l].T.astype(bf16), b_router[l])
        dest, slot_tok, block_e = _moe_plan(top_e, rank, counts.reshape(-1))
        ys = moe_experts(h2_packed, slot_tok, w_exp_gate, w_exp_up, w_exp_down, l, block_e)
        routed = moe_combine(ys, dest, top_w)
        d_sh = w_sh_gate.shape[-1]
        act = fused_matmul([(h2, w_sh_gate[l].astype(bf16)), (h2, w_sh_up[l].astype(bf16))], [], _ep_glu,
                           jax.ShapeDtypeStruct((t, d_sh), bf16), _tile_spec(tm, d_sh), tm, d_sh, name="shared_glu")
        xa = fused_matmul(
            [(act, w_sh_down[l].astype(bf16))],
            [(xa, _tile_spec(tm, 512)), (mods[5], ctx_spec), (mods[5], b_spec), (routed, _tile_spec(tm, 512))],
            _make_ep_residual(tm, rows_b, n_ctx, True), jax.ShapeDtypeStruct((t, d), f32),
            _tile_spec(tm, 512), tm, 512, name="moe_out")

    return xa.reshape(n_batch, rows_b, d)[:, n_ctx:]
```

```python
import functools
import math

import jax
import jax.numpy as jnp
from jax import lax
from jax.experimental import pallas as pl
from jax.experimental.pallas import tpu as pltpu

f32 = jnp.float32
bf16 = jnp.bfloat16
i32 = jnp.int32

HEAD_DIM = 128
GRID_W = 64
ROPE_THETA = 10000.0
EPS = 1e-6
F_MIN = 1e-6
SCAN_CHUNK = 64
SCAN_SUB = 8
N_GROUPS = 8
TOPK_GROUPS = 4
TOP_K = 8
ROUTE_SCALE = 2.5
MASK_SCORE = -1e9
LOG2E = 1.4426950408889634

VMEM_LIMIT_BYTES = 48 * 1024 * 1024
ROW_TILE = 256
MM_TILE_M = 768
MOE_TILE = 256
MXU_COLS = 256
SCAN_TILE = 256


def _params(*semantics, flags=None):
    return pltpu.CompilerParams(dimension_semantics=semantics, vmem_limit_bytes=VMEM_LIMIT_BYTES, flags=flags)


def _silu(x):
    return x * jax.nn.sigmoid(x)


def _nt_dot(a, b):
    return lax.dot_general(a, b, (((1,), (1,)), ((), ())), preferred_element_type=f32)


def _tn_dot(a, b):
    return lax.dot_general(a, b, (((0,), (0,)), ((), ())), preferred_element_type=f32)


def _row_is_ctx(tile_index, tile_rows, rows_per_batch, n_ctx):
    row0 = (tile_index % (rows_per_batch // tile_rows)) * tile_rows
    rid = row0 + lax.broadcasted_iota(i32, (tile_rows, 1), 0)
    return rid < n_ctx


def _fm_kernel(*refs, n_pairs, n_extra, a_fn, epilogue):
    pair_refs = refs[:2 * n_pairs]
    extra = refs[2 * n_pairs:2 * n_pairs + n_extra]
    outs = refs[2 * n_pairs + n_extra:]
    lhs = []
    for p in range(n_pairs):
        a = pair_refs[2 * p][...]
        if a_fn is not None:
            a = a_fn(a)
        lhs.append(a.astype(bf16))
    tn = pair_refs[1].shape[-1]

    def dots(cs):
        accs = []
        for p in range(n_pairs):
            w_ref = pair_refs[2 * p + 1]
            w = w_ref[0, :, cs] if len(w_ref.shape) == 3 else w_ref[:, cs]
            accs.append(jnp.dot(lhs[p], w.astype(bf16), preferred_element_type=f32))
        return accs

    chunks = [slice(c0, min(c0 + MXU_COLS, tn)) for c0 in range(0, tn, MXU_COLS)]
    pending = dots(chunks[0])
    for ci in range(1, len(chunks)):
        accs = dots(chunks[ci])
        epilogue(pending, extra, outs, chunks[ci - 1])
        pending = accs
    epilogue(pending, extra, outs, chunks[-1])


def fused_matmul(pairs, extras, epilogue, out_shapes, out_specs, tm, tn, a_fn=None, name=None, layer=None):
    m = pairs[0][0].shape[0]
    n = pairs[0][1].shape[-1]
    assert m % tm == 0 and n % tn == 0, (m, tm, n, tn)
    in_specs, args = [], []
    for a, w in pairs:
        k = a.shape[1]
        assert w.shape[-2] == k
        if layer is None:
            w_spec = pl.BlockSpec((k, tn), lambda i, j: (0, j))
        else:
            w_spec = pl.BlockSpec((1, k, tn), lambda i, j: (layer, 0, j))
        in_specs += [pl.BlockSpec((tm, k), lambda i, j: (i, 0)), w_spec]
        args += [a, w]
    for arr, spec in extras:
        in_specs.append(spec)
        args.append(arr)
    return pl.pallas_call(
        functools.partial(_fm_kernel, n_pairs=len(pairs), n_extra=len(extras), a_fn=a_fn, epilogue=epilogue),
        grid=(m // tm, n // tn),
        in_specs=in_specs,
        out_specs=out_specs,
        out_shape=out_shapes,
        compiler_params=_params("parallel", "arbitrary"),
        name=name,
    )(*args)


def _tile_spec(tm, tn):
    return pl.BlockSpec((tm, tn), lambda i, j: (i, j))


def _ep_store(dtype):
    def ep(accs, extra, outs, cs):
        outs[0][:, cs] = accs[0].astype(dtype)
    return ep


def _ep_bias(accs, extra, outs, cs):
    outs[0][:, cs] = accs[0] + extra[0][:, cs]


def _ep_sigmoid(accs, extra, outs, cs):
    outs[0][:, cs] = (0.5 * jnp.tanh(0.5 * accs[0]) + 0.5).astype(outs[0].dtype)


def _ep_qk(accs, extra, outs, cs):
    gain_ref, cos_ref, sin_ref = extra
    acc = accs[0]
    cos = cos_ref[...]
    sin = sin_ref[...]
    for c in range(acc.shape[1] // HEAD_DIM):
        head = cs.start // HEAD_DIM + c
        x = acc[:, c * HEAD_DIM:(c + 1) * HEAD_DIM]
        y = x * lax.rsqrt(jnp.mean(x * x, axis=-1, keepdims=True) + EPS)
        y = y * gain_ref[:, head * HEAD_DIM:(head + 1) * HEAD_DIM]
        outs[0][head] = (y * cos + pltpu.roll(y, HEAD_DIM // 2, 1) * sin).astype(outs[0].dtype)


def _ep_merge(accs, extra, outs, cs):
    y = extra[0][:, cs].astype(f32) * accs[0]
    y = y + extra[1][:, cs].astype(f32) * accs[1]
    y = y + extra[2][:, cs].astype(f32) * accs[2]
    outs[0][:, cs] = y.astype(outs[0].dtype)


def _ep_glu(accs, extra, outs, cs):
    outs[0][:, cs] = (_silu(accs[0]) * accs[1]).astype(outs[0].dtype)


def _make_ep_residual(tm, rows_per_batch, n_ctx, with_routed):
    def ep(accs, extra, outs, cs):
        x_ref, mod_ctx_ref, mod_b_ref = extra[:3]
        y = accs[0]
        if with_routed:
            y = y + extra[3][:, cs]
        is_ctx = _row_is_ctx(pl.program_id(0), tm, rows_per_batch, n_ctx)
        mod = jnp.where(is_ctx, mod_ctx_ref[0, :, cs], mod_b_ref[0, :, cs])
        outs[0][:, cs] = x_ref[:, cs] + mod * y
    return ep


PACK_ROWS = 8


def _pack_words(x):
    bits = lax.bitcast_convert_type(x.astype(bf16).astype(f32), jnp.uint32)
    half = bits.shape[1] // 2
    return [bits[:, half + s * HEAD_DIM:half + (s + 1) * HEAD_DIM] | (bits[:, s * HEAD_DIM:(s + 1) * HEAD_DIM] >> 16)
            for s in range(PACK_ROWS)]


def _unpack_word(word):
    return (lax.bitcast_convert_type(word << 16, f32),
            lax.bitcast_convert_type(word & jnp.uint32(0xFFFF0000), f32))


def _norm_kernel(x_ref, w_ref, shc_ref, shb_ref, scc_ref, scb_ref, o_ref, *packed, tile, rows_per_batch, n_ctx):
    x = x_ref[...]
    y = x * lax.rsqrt(jnp.mean(x * x, axis=-1, keepdims=True) + EPS) * w_ref[...]
    is_ctx = _row_is_ctx(pl.program_id(0), tile, rows_per_batch, n_ctx)
    shift = jnp.where(is_ctx, shc_ref[0], shb_ref[0])
    scale = jnp.where(is_ctx, scc_ref[0], scb_ref[0])
    hb = (y * (1.0 + scale) + shift).astype(bf16)
    o_ref[...] = hb
    if packed:
        for s, word in enumerate(_pack_words(hb)):
            packed[0][pl.ds(s, tile, stride=PACK_ROWS), :] = word


def norm_modulate(x, w, shift, scale, rows_per_batch, n_ctx, packed_rows=False):
    t, d = x.shape
    tile = ROW_TILE
    n_b = shift.shape[0] - 1
    tiles_per_batch = rows_per_batch // tile
    ctx_spec = pl.BlockSpec((1, 1, d), lambda i: (n_b, 0, 0))
    b_spec = pl.BlockSpec((1, 1, d), lambda i: (i // tiles_per_batch, 0, 0))
    row_spec = pl.BlockSpec((tile, d), lambda i: (i, 0))
    out_specs, out_shape = [row_spec], [jax.ShapeDtypeStruct((t, d), bf16)]
    if packed_rows:
        assert d == 2 * PACK_ROWS * HEAD_DIM
        out_specs.append(pl.BlockSpec((tile * PACK_ROWS, HEAD_DIM), lambda i: (i, 0)))
        out_shape.append(jax.ShapeDtypeStruct((t * PACK_ROWS, HEAD_DIM), jnp.uint32))
    return pl.pallas_call(
        functools.partial(_norm_kernel, tile=tile, rows_per_batch=rows_per_batch, n_ctx=n_ctx),
        grid=(t // tile,),
        in_specs=[row_spec, pl.BlockSpec((1, d), lambda i: (0, 0)), ctx_spec, b_spec, ctx_spec, b_spec],
        out_specs=out_specs,
        out_shape=out_shape,
        compiler_params=_params("parallel"),
        name="norm_modulate",
    )(x, w.reshape(1, d), shift, shift, scale, scale)


def _attend(q, k_at, v_at, kv_len, tk, dv):
    m = l = acc = None
    n = kv_len // tk
    s_next = _nt_dot(q, k_at(0))
    for c in range(n):
        s = s_next
        if c + 1 < n:
            s_next = _nt_dot(q, k_at((c + 1) * tk))
        s_max = jnp.max(s, axis=1, keepdims=True)
        if c == 0:
            m = s_max
            p = jnp.exp2(s - m)
            l = jnp.sum(p, axis=1, keepdims=True)
            acc = jnp.dot(p.astype(bf16), v_at(c * tk), preferred_element_type=f32)
        else:
            m_new = jnp.maximum(m, s_max)
            alpha = jnp.exp2(m - m_new)
            p = jnp.exp2(s - m_new)
            l = alpha * l + jnp.sum(p, axis=1, keepdims=True)
            acc = alpha * acc + jnp.dot(p.astype(bf16), v_at(c * tk), preferred_element_type=f32)
            m = m_new
    return acc / l


def _kv_chunk(kv_len):
    for tk in (768, 1408, 512, 256, 128):
        if kv_len % tk == 0:
            return tk
    raise ValueError(kv_len)


def _ctx_or_all(n_ctx_tiles, n_ctx, kv_all, run):
    is_ctx = pl.program_id(2) < n_ctx_tiles

    @pl.when(is_ctx)
    def _():
        run(n_ctx)

    @pl.when(jnp.logical_not(is_ctx))
    def _():
        run(kv_all)


def _attn_a_kernel(q_ref, k_ref, v_ref, o_ref, *, n_ctx):
    g, tq, d = q_ref.shape

    def run(kv_len):
        tk = _kv_chunk(kv_len)
        q = q_ref[...].reshape(g * tq, d)
        o = _attend(q, lambda off: k_ref[0, off:off + tk, :], lambda off: v_ref[off:off + tk, :], kv_len, tk, d)
        for h in range(g):
            o_ref[:, h * d:(h + 1) * d] = o[h * tq:(h + 1) * tq].astype(o_ref.dtype)

    _ctx_or_all(n_ctx // tq, n_ctx, k_ref.shape[1], run)


def _attn_c_kernel(lam_ref, dn_ref, q_ref, k_ref, v_ref, o_ref, *, n_ctx, lam_init):
    tq = q_ref.shape[1]
    dv = v_ref.shape[-1]

    def run(kv_len):
        tk = _kv_chunk(kv_len)
        o1 = _attend(q_ref[0], lambda off: k_ref[0, off:off + tk, :], lambda off: v_ref[off:off + tk, :],
                     kv_len, tk, dv)
        o2 = _attend(q_ref[1], lambda off: k_ref[1, off:off + tk, :], lambda off: v_ref[off:off + tk, :],
                     kv_len, tk, dv)
        lam_p = lam_ref[...]
        lam = (jnp.exp(jnp.sum(lam_p[0:1] * lam_p[1:2], axis=-1, keepdims=True))
               - jnp.exp(jnp.sum(lam_p[2:3] * lam_p[3:4], axis=-1, keepdims=True)) + lam_init)
        o = o1 - lam * o2
        y = o * lax.rsqrt(jnp.mean(o * o, axis=-1, keepdims=True) + EPS) * dn_ref[...]
        o_ref[...] = (y * (1.0 - lam_init)).astype(o_ref.dtype)

    _ctx_or_all(n_ctx // tq, n_ctx, k_ref.shape[1], run)


def attention_a(qk, va, *, n_batch, rows_per_batch, n_ctx, a_heads, a_kv_heads):
    t = qk.shape[1]
    group = a_heads // a_kv_heads
    tq = ROW_TILE
    n_q = rows_per_batch // tq
    return pl.pallas_call(
        functools.partial(_attn_a_kernel, n_ctx=n_ctx),
        grid=(n_batch, a_kv_heads, n_q),
        in_specs=[
            pl.BlockSpec((group, tq, HEAD_DIM), lambda b, h, i: (h, b * n_q + i, 0)),
            pl.BlockSpec((1, rows_per_batch, HEAD_DIM), lambda b, h, i: (a_heads + h, b, 0)),
            pl.BlockSpec((rows_per_batch, HEAD_DIM), lambda b, h, i: (b, h)),
        ],
        out_specs=pl.BlockSpec((tq, group * HEAD_DIM), lambda b, h, i: (b * n_q + i, h)),
        out_shape=jax.ShapeDtypeStruct((t, a_heads * HEAD_DIM), bf16),
        compiler_params=_params("parallel", "parallel", "arbitrary"),
        name="attention_a",
    )(qk, qk, va)


def attention_c(qk, vc, lam_params, diff_norm, *, n_batch, rows_per_batch, n_ctx, head0_q, head0_k,
                c_heads, lam_init):
    t = qk.shape[1]
    dv = 2 * HEAD_DIM
    tq = ROW_TILE
    n_q = rows_per_batch // tq
    return pl.pallas_call(
        functools.partial(_attn_c_kernel, n_ctx=n_ctx, lam_init=lam_init),
        grid=(n_batch, c_heads, n_q),
        in_specs=[
            pl.BlockSpec((4, HEAD_DIM), lambda b, h, i: (0, 0)),
            pl.BlockSpec((1, dv), lambda b, h, i: (0, 0)),
            pl.BlockSpec((2, tq, HEAD_DIM), lambda b, h, i: (head0_q // 2 + h, b * n_q + i, 0)),
            pl.BlockSpec((2, rows_per_batch, HEAD_DIM), lambda b, h, i: (head0_k // 2 + h, b, 0)),
            pl.BlockSpec((rows_per_batch, dv), lambda b, h, i: (b, h)),
        ],
        out_specs=pl.BlockSpec((tq, dv), lambda b, h, i: (b * n_q + i, h)),
        out_shape=jax.ShapeDtypeStruct((t, c_heads * dv), bf16),
        compiler_params=_params("parallel", "parallel", "arbitrary"),
        name="attention_c",
    )(lam_params, diff_norm.reshape(1, dv), qk, qk, vc)


def _hgrn_tables(rev):
    c = SCAN_CHUNK
    half = SCAN_SUB // 2
    t_i = lax.broadcasted_iota(i32, (c, c), 0)
    u_i = lax.broadcasted_iota(i32, (c, c), 1)
    blk8 = t_i & ~(SCAN_SUB - 1)
    blk4 = t_i & ~(half - 1)
    if rev:
        rows_upto = [t_i, blk4 + half - 1, blk8 + half, blk8]
        sums = jnp.concatenate([(u_i >= r).astype(f32) for r in rows_upto], axis=0)
    else:
        rows_upto = [t_i, blk4, blk8 + half - 1, blk8 + SCAN_SUB - 1]
        sums = jnp.concatenate([(u_i <= r).astype(f32) for r in rows_upto], axis=0)
    causal = (u_i >= t_i) if rev else (u_i <= t_i)
    same8 = (u_i & ~(SCAN_SUB - 1)) == blk8
    same4 = (u_i & ~(half - 1)) == blk4
    return sums, same8 & jnp.logical_not(same4) & causal, same4 & causal


def _hgrn_gates(qp, z, lb):
    q = _silu(qp.astype(f32))
    one_m = 1.0 - lb
    g = jnp.log(jnp.maximum(lb + one_m * jax.nn.sigmoid(z), F_MIN))
    k = one_m * jax.nn.sigmoid(-z)
    return q, k, g


def _hgrn_local(q, k, b_all, v, tables, rev):
    c = SCAN_CHUNK
    nb = c // SCAN_SUB
    _, across, inner = tables
    b, rho4, rho_mid, rho_edge = (b_all[i * c:(i + 1) * c] for i in range(4))
    end = 0 if rev else c - 1
    b_end = b[end:end + 1, :]
    q_in = (q * jnp.exp(b)).astype(bf16)
    d_st = _tn_dot(v, (k * jnp.exp(b_end - b)).astype(bf16))

    def edge_row(j):
        return j * SCAN_SUB if rev else j * SCAN_SUB + SCAN_SUB - 1

    row = lax.broadcasted_iota(i32, (c, HEAD_DIM), 0)
    k_til = k * jnp.exp(rho_edge - b)
    qs, ks = [], []
    for j in (range(1, nb) if rev else range(nb - 1)):
        rho = b[edge_row(j):edge_row(j) + 1, :]
        q_rows = (row < j * SCAN_SUB) if rev else (row >= (j + 1) * SCAN_SUB)
        qs.append((q * jnp.exp(jnp.where(q_rows, b - rho, -jnp.inf))).astype(bf16))
        k_rows = (row >= j * SCAN_SUB) & (row < (j + 1) * SCAN_SUB)
        ks.append(jnp.where(k_rows, k_til, 0.0).astype(bf16))
    att = _nt_dot(jnp.concatenate(qs, axis=1), jnp.concatenate(ks, axis=1))

    att_mid = _nt_dot((q * jnp.exp(jnp.minimum(b - rho_mid, 0.0))).astype(bf16),
                      (k * jnp.exp(jnp.minimum(rho_mid - b, 0.0))).astype(bf16))
    att_in = _nt_dot((q * jnp.exp(b - rho4)).astype(bf16), (k * jnp.exp(rho4 - b)).astype(bf16))
    att = att + jnp.where(across, att_mid, 0.0) + jnp.where(inner, att_in, 0.0)
    return jnp.dot(att.astype(bf16), v, preferred_element_type=f32), q_in, d_st, jnp.exp(b_end)


def _hgrn_kernel(lb_ref, qf_ref, vf_ref, zf_ref, qb_ref, vb_ref, zb_ref, of_ref, ob_ref, sf_ref, sb_ref):
    @pl.when(pl.program_id(2) == 0)
    def _():
        sf_ref[...] = jnp.zeros_like(sf_ref)
        sb_ref[...] = jnp.zeros_like(sb_ref)

    lb = lb_ref[...]
    n_chunks = qf_ref.shape[0] // SCAN_CHUNK
    work = []
    for c in range(n_chunks):
        work.append((slice(c * SCAN_CHUNK, (c + 1) * SCAN_CHUNK), False))
        work.append((slice((n_chunks - 1 - c) * SCAN_CHUNK, (n_chunks - c) * SCAN_CHUNK), True))
    refs = {False: (qf_ref, vf_ref, zf_ref, of_ref), True: (qb_ref, vb_ref, zb_ref, ob_ref)}
    tables = {False: _hgrn_tables(False), True: _hgrn_tables(True)}

    gates = [_hgrn_gates(refs[rev][0][rows, :], refs[rev][2][rows, :], lb) for rows, rev in work]
    sums = [jnp.dot(tables[rev][0], g, precision=lax.Precision.HIGHEST, preferred_element_type=f32)
            for (_, rev), (_, _, g) in zip(work, gates)]
    local = [_hgrn_local(q, k, b_all, refs[rev][1][rows, :], tables[rev], rev)
             for (rows, rev), (q, k, _), b_all in zip(work, gates, sums)]
    state = {False: sf_ref[...], True: sb_ref[...]}
    for (rows, rev), (o_in, q_in, d_st, decay) in zip(work, local):
        st = state[rev]
        refs[rev][3][rows, :] = o_in + _nt_dot(q_in, st.astype(bf16))
        state[rev] = st * decay + d_st
    sf_ref[...] = state[False]
    sb_ref[...] = state[True]


def hgrn_scan(qio, zfb, lower_bound, *, n_batch, rows_per_batch, b_heads):
    t = qio.shape[0]
    tile = SCAN_TILE
    n_tiles = rows_per_batch // tile

    def fwd(b, h, j):
        return b * n_tiles + j

    def bwd(b, h, j):
        return b * n_tiles + jnp.where(j == 0, 0, n_tiles - j)

    def spec(tile_fn, col0):
        return pl.BlockSpec((tile, HEAD_DIM), lambda b, h, j: (tile_fn(b, h, j), col0 + h))

    return pl.pallas_call(
        _hgrn_kernel,
        grid=(n_batch, b_heads, n_tiles),
        in_specs=[
            pl.BlockSpec((1, HEAD_DIM), lambda b, h, j: (0, h)),
            spec(fwd, 0), spec(fwd, b_heads), spec(fwd, 0),
            spec(bwd, 0), spec(bwd, b_heads), spec(bwd, b_heads),
        ],
        out_specs=[spec(fwd, 0), spec(bwd, 0)],
        out_shape=[jax.ShapeDtypeStruct((t, b_heads * HEAD_DIM), f32)] * 2,
        scratch_shapes=[pltpu.VMEM((HEAD_DIM, HEAD_DIM), f32)] * 2,
        compiler_params=_params("parallel", "parallel", "arbitrary"),
        name="hgrn_scan",
    )(lower_bound.reshape(1, -1), qio, qio, zfb, qio, qio, zfb)


def _hgrn_out_kernel(of_ref, ob_ref, og_ref, w_ref, o_ref):
    o = of_ref[...] + ob_ref[...]
    w = w_ref[...]
    for c in range(o.shape[1] // HEAD_DIM):
        sl = slice(c * HEAD_DIM, (c + 1) * HEAD_DIM)
        x = o[:, sl]
        y = x * lax.rsqrt(jnp.mean(x * x, axis=-1, keepdims=True) + EPS) * w
        o_ref[:, sl] = (y * _silu(og_ref[:, sl].astype(f32))).astype(o_ref.dtype)


def hgrn_readout(o_f, o_b, qio, hgrn_norm, b_heads):
    t, w = o_f.shape
    tile = ROW_TILE
    row_spec = pl.BlockSpec((tile, w), lambda i: (i, 0))
    return pl.pallas_call(
        _hgrn_out_kernel,
        grid=(t // tile,),
        in_specs=[row_spec, row_spec, pl.BlockSpec((tile, w), lambda i: (i, 2)),
                  pl.BlockSpec((1, HEAD_DIM), lambda i: (0, 0))],
        out_specs=row_spec,
        out_shape=jax.ShapeDtypeStruct((t, w), bf16),
        compiler_params=_params("parallel"),
        name="hgrn_readout",
    )(o_f, o_b, qio, hgrn_norm.reshape(1, HEAD_DIM))


def _router_kernel(h_ref, w_ref, b_ref, e_ref, wt_ref, rank_ref, cnt_ref, run_ref):
    n_e = w_ref.shape[0]
    tm = h_ref.shape[0]
    per = n_e // N_GROUPS

    @pl.when(pl.program_id(0) == 0)
    def _():
        run_ref[...] = jnp.zeros_like(run_ref)

    scores = jax.nn.sigmoid(_nt_dot(w_ref[...], h_ref[...]))
    biased = scores + b_ref[...]
    neg = -jnp.inf

    grp = biased.reshape(N_GROUPS, per, tm)
    idx_in = lax.broadcasted_iota(i32, grp.shape, 1)
    m1 = jnp.max(grp, axis=1, keepdims=True)
    first = jnp.min(jnp.where(grp == m1, idx_in, per), axis=1, keepdims=True)
    m2 = jnp.max(jnp.where(idx_in == first, neg, grp), axis=1, keepdims=True)
    cur = m1 + m2

    g_idx = lax.broadcasted_iota(i32, cur.shape, 0)
    g_sel = jnp.zeros(cur.shape, f32)
    for _ in range(TOPK_GROUPS):
        mx = jnp.max(cur, axis=0, keepdims=True)
        pick = g_idx == jnp.min(jnp.where(cur == mx, g_idx, N_GROUPS), axis=0, keepdims=True)
        g_sel = jnp.where(pick, 1.0, g_sel)
        cur = jnp.where(pick, neg, cur)
    e_mask = jnp.broadcast_to(g_sel, (N_GROUPS, per, tm)).reshape(n_e, tm) > 0.0

    e_idx = lax.broadcasted_iota(i32, (n_e, tm), 0)
    cur = jnp.where(e_mask, biased, MASK_SCORE)
    ws, picks = [], []
    member = jnp.zeros((n_e, tm), f32)
    for r in range(TOP_K):
        mx = jnp.max(cur, axis=0, keepdims=True)
        first = jnp.min(jnp.where(cur == mx, e_idx, n_e), axis=0, keepdims=True)
        pick = e_idx == first
        e_ref[r:r + 1, :] = first
        ws.append(jnp.sum(jnp.where(pick, scores, 0.0), axis=0, keepdims=True))
        cur = jnp.where(pick, neg, cur)
        member = jnp.where(pick, 1.0, member)
        picks.append(pick)
    total = ws[0]
    for r in range(1, TOP_K):
        total = total + ws[r]
    for r in range(TOP_K):
        wt_ref[r:r + 1, :] = ws[r] / total * ROUTE_SCALE

    u_i = lax.broadcasted_iota(i32, (tm, tm), 0)
    t_i = lax.broadcasted_iota(i32, (tm, tm), 1)
    prefix = jnp.dot(member.astype(bf16), (u_i <= t_i).astype(bf16), preferred_element_type=f32)
    rank = run_ref[...] + prefix
    for r in range(TOP_K):
        rank_ref[r:r + 1, :] = jnp.sum(jnp.where(picks[r], rank, 0.0), axis=0, keepdims=True).astype(i32)
    run_ref[...] = run_ref[...] + jnp.sum(member, axis=1, keepdims=True)
    cnt_ref[...] = run_ref[...].astype(i32)


def moe_route(h, w_router_t, b_router):
    t, d = h.shape
    n_e = w_router_t.shape[0]
    tm = 512
    assert t % tm == 0
    out_spec = pl.BlockSpec((TOP_K, tm), lambda i: (0, i))
    return pl.pallas_call(
        _router_kernel,
        grid=(t // tm,),
        in_specs=[pl.BlockSpec((tm, d), lambda i: (i, 0)), pl.BlockSpec((n_e, d), lambda i: (0, 0)),
                  pl.BlockSpec((n_e, 1), lambda i: (0, 0))],
        out_specs=[out_spec, out_spec, out_spec, pl.BlockSpec((n_e, 1), lambda i: (0, 0))],
        out_shape=[jax.ShapeDtypeStruct((TOP_K, t), i32), jax.ShapeDtypeStruct((TOP_K, t), f32),
                   jax.ShapeDtypeStruct((TOP_K, t), i32), jax.ShapeDtypeStruct((n_e, 1), i32)],
        scratch_shapes=[pltpu.VMEM((n_e, 1), f32)],
        compiler_params=_params("arbitrary"),
        name="moe_route",
    )(h, w_router_t, b_router.reshape(n_e, 1).astype(f32))


GATHER_DEPTH = 3


def _expert_kernel(be_ref, idx0_ref, idx1_ref, idx_ahead_ref, h_ref, wg_ref, wu_ref, wd_ref, y_ref,
                   xbuf, sem, x_s, wg_b, wu_b, wd_b):
    i = pl.program_id(0)
    last = pl.num_programs(0) - 1
    tm = xbuf.shape[1] // PACK_ROWS
    slot = i % GATHER_DEPTH
    ahead = (i + GATHER_DEPTH - 1) % GATHER_DEPTH

    def gather_start(idx_ref, dst_slot):
        for r in range(tm):
            src_row = pl.multiple_of(idx_ref[0, 0, r] * PACK_ROWS, PACK_ROWS)
            pltpu.make_async_copy(h_ref.at[pl.ds(src_row, PACK_ROWS)],
                                  xbuf.at[dst_slot, pl.ds(r * PACK_ROWS, PACK_ROWS)],
                                  sem.at[dst_slot]).start(priority=r % 2)

    def gather_wait(dst_slot):
        pltpu.make_async_copy(h_ref.at[pl.ds(0, tm * PACK_ROWS)], xbuf.at[dst_slot], sem.at[dst_slot]).wait()

    @pl.when(i == 0)
    def _():
        gather_start(idx0_ref, 0)
        gather_start(idx1_ref, 1)

    @pl.when((i == 0) | (be_ref[i] != be_ref[jnp.maximum(i - 1, 0)]))
    def _():
        wg_b[...] = wg_ref[0, 0].astype(bf16)
        wu_b[...] = wu_ref[0, 0].astype(bf16)
        wd_b[...] = wd_ref[0, 0].astype(bf16)

    gather_wait(slot)
    half = PACK_ROWS * HEAD_DIM
    for s in range(PACK_ROWS):
        lo, hi = _unpack_word(xbuf[slot, pl.ds(s, tm, stride=PACK_ROWS), :])
        x_s[:, s * HEAD_DIM:(s + 1) * HEAD_DIM] = lo.astype(bf16)
        x_s[:, half + s * HEAD_DIM:half + (s + 1) * HEAD_DIM] = hi.astype(bf16)
    gather_start(idx_ahead_ref, ahead)
    x = x_s[...]
    g = jnp.dot(x, wg_b[...], preferred_element_type=f32)
    u = jnp.dot(x, wu_b[...], preferred_element_type=f32)
    a = (_silu(g) * u).astype(bf16)
    y = jnp.dot(a, wd_b[...], preferred_element_type=f32)
    for s, word in enumerate(_pack_words(y)):
        y_ref[pl.ds(s, tm, stride=PACK_ROWS), :] = word

    @pl.when(i == last)
    def _():
        for k in range(1, GATHER_DEPTH):
            gather_wait((i + k) % GATHER_DEPTH)


def moe_experts(h_packed, slot_tok, w_gate, w_up, w_down, layer, block_e):
    tm = MOE_TILE
    n_blocks = slot_tok.shape[0] // tm
    assert n_blocks >= GATHER_DEPTH
    d, d_e = w_gate.shape[-2:]
    idx = slot_tok.reshape(n_blocks, 1, tm)

    def idx_spec(block_fn):
        return pl.BlockSpec((1, 1, tm), lambda i, be: (block_fn(i), 0, 0), memory_space=pltpu.SMEM)

    def w_spec(shape):
        return pl.BlockSpec((1, 1) + shape, lambda i, be: (layer, be[i], 0, 0))

    return pl.pallas_call(
        _expert_kernel,
        grid_spec=pltpu.PrefetchScalarGridSpec(
            num_scalar_prefetch=1,
            grid=(n_blocks,),
            in_specs=[idx_spec(lambda i: 0), idx_spec(lambda i: 1),
                      idx_spec(lambda i: jnp.minimum(i + GATHER_DEPTH - 1, n_blocks - 1)),
                      pl.BlockSpec(memory_space=pl.ANY),
                      w_spec((d, d_e)), w_spec((d, d_e)), w_spec((d_e, d))],
            out_specs=pl.BlockSpec((tm * PACK_ROWS, HEAD_DIM), lambda i, be: (i, 0)),
            scratch_shapes=[pltpu.VMEM((GATHER_DEPTH, tm * PACK_ROWS, HEAD_DIM), jnp.uint32),
                            pltpu.SemaphoreType.DMA((GATHER_DEPTH,)), pltpu.VMEM((tm, d), bf16),
                            pltpu.VMEM((d, d_e), bf16), pltpu.VMEM((d, d_e), bf16), pltpu.VMEM((d_e, d), bf16)],
        ),
        out_shape=jax.ShapeDtypeStruct((n_blocks * tm * PACK_ROWS, HEAD_DIM), jnp.uint32),
        compiler_params=_params("arbitrary"),
        name="moe_experts",
    )(block_e, idx, idx, idx, h_packed, w_gate, w_up, w_down)


def _combine_kernel(idx_ref, idx_next_ref, w_ref, y_ref, o_ref, buf, sem):
    i = pl.program_id(0)
    last = pl.num_programs(0) - 1
    n_k = buf.shape[1]
    rows = buf.shape[2] // PACK_ROWS
    slot = i % 2

    def gather_start(src_idx_ref, dst_slot):
        for r in range(rows):
            for kk in range(n_k):
                src_row = pl.multiple_of(src_idx_ref[0, kk, r] * PACK_ROWS, PACK_ROWS)
                pltpu.make_async_copy(y_ref.at[pl.ds(src_row, PACK_ROWS)],
                                      buf.at[dst_slot, kk, pl.ds(r * PACK_ROWS, PACK_ROWS)],
                                      sem.at[dst_slot]).start(priority=(r * n_k + kk) % 2)

    def gather_wait(dst_slot):
        for kk in range(n_k):
            pltpu.make_async_copy(y_ref.at[pl.ds(0, rows * PACK_ROWS)], buf.at[dst_slot, kk],
                                  sem.at[dst_slot]).wait()

    @pl.when(i == 0)
    def _():
        gather_start(idx_ref, 0)

    gather_start(idx_next_ref, 1 - slot)
    gather_wait(slot)
    w = w_ref[...]
    half = PACK_ROWS * HEAD_DIM
    for s in range(PACK_ROWS):
        acc_lo = acc_hi = None
        for kk in range(n_k):
            lo, hi = _unpack_word(buf[slot, kk, pl.ds(s, rows, stride=PACK_ROWS), :])
            w_k = w[:, kk:kk + 1]
            acc_lo = lo * w_k if kk == 0 else acc_lo + lo * w_k
            acc_hi = hi * w_k if kk == 0 else acc_hi + hi * w_k
        o_ref[:, s * HEAD_DIM:(s + 1) * HEAD_DIM] = acc_lo
        o_ref[:, half + s * HEAD_DIM:half + (s + 1) * HEAD_DIM] = acc_hi

    @pl.when(i == last)
    def _():
        gather_wait(1 - slot)


def moe_combine(y, dest, top_w):
    n_k, t = dest.shape
    d = 2 * PACK_ROWS * HEAD_DIM
    rows = 128
    n_tiles = t // rows
    idx = dest.reshape(n_k, n_tiles, rows).transpose(1, 0, 2)

    def idx_spec(tile_fn):
        return pl.BlockSpec((1, n_k, rows), lambda i: (tile_fn(i), 0, 0), memory_space=pltpu.SMEM)

    return pl.pallas_call(
        _combine_kernel,
        grid=(n_tiles,),
        in_specs=[idx_spec(lambda i: i), idx_spec(lambda i: jnp.minimum(i + 1, n_tiles - 1)),
                  pl.BlockSpec((rows, n_k), lambda i: (i, 0)),
                  pl.BlockSpec(memory_space=pl.ANY)],
        out_specs=pl.BlockSpec((rows, d), lambda i: (i, 0)),
        out_shape=jax.ShapeDtypeStruct((t, d), f32),
        scratch_shapes=[pltpu.VMEM((2, n_k, rows * PACK_ROWS, HEAD_DIM), jnp.uint32),
                        pltpu.SemaphoreType.DMA((2,))],
        compiler_params=_params("arbitrary"),
        name="moe_combine",
    )(idx, idx, top_w.T, y)


def _moe_plan(top_e, rank, counts):
    n_k, t = top_e.shape
    n_experts = counts.shape[0]
    tm = MOE_TILE
    n_blocks = -(-(n_k * t) // tm) + n_experts
    padded = (counts + tm - 1) // tm * tm
    pad_end = jnp.cumsum(padded)
    pad_start = pad_end - padded
    onehot = top_e[:, :, None] == jnp.arange(n_experts, dtype=i32)
    dest = jnp.sum(jnp.where(onehot, pad_start, 0), axis=-1) + rank - 1
    n_used = pad_end[-1] // tm
    blk = jnp.minimum(jnp.arange(n_blocks, dtype=i32), n_used - 1)
    block_e = jnp.sum((pad_end[None, :] <= (blk * tm)[:, None]).astype(i32), axis=1)
    block_e = jnp.minimum(block_e, n_experts - 1)
    tok = jnp.broadcast_to(jnp.arange(t, dtype=i32)[None, :], (n_k, t))
    sorted_tok = jnp.sort((top_e * t + tok).reshape(-1)) % t
    of_block = block_e[:, None] == jnp.arange(n_experts, dtype=i32)
    grp_start = jnp.cumsum(counts) - counts

    def per_block(table):
        return jnp.sum(jnp.where(of_block, table, 0), axis=1, keepdims=True)

    j = jnp.arange(n_blocks * tm, dtype=i32).reshape(n_blocks, tm) - per_block(pad_start)
    valid = (j >= 0) & (j < per_block(counts))
    src = jnp.clip(per_block(grp_start) + j, 0, n_k * t - 1)
    slot_tok = jnp.where(valid, jnp.take(sorted_tok, src.reshape(-1)).reshape(n_blocks, tm), 0)
    return dest.astype(i32), slot_tok.reshape(-1), block_e


def _rope_perm():
    quarter = HEAD_DIM // 4
    return jnp.concatenate([jnp.arange(q * quarter, (q + 1) * quarter, dtype=i32) for q in (0, 2, 1, 3)])


def _rope_tables(n_ctx, n_lat):
    t = jnp.arange(n_lat)
    pos = jnp.stack([t // GRID_W, t % GRID_W], axis=-1).astype(f32)
    n_freq = HEAD_DIM // 4
    inv_freq = ROPE_THETA ** (-jnp.arange(n_freq, dtype=f32) / n_freq)
    ang = pos[:, :, None] * inv_freq
    cos, sin = jnp.cos(ang), jnp.sin(ang)
    cos_t = jnp.concatenate([cos[:, 0], cos[:, 1], cos[:, 0], cos[:, 1]], axis=-1)
    sin_t = jnp.concatenate([-sin[:, 0], -sin[:, 1], sin[:, 0], sin[:, 1]], axis=-1)
    cos_t = jnp.concatenate([jnp.ones((n_ctx, HEAD_DIM), f32), cos_t], axis=0)
    sin_t = jnp.concatenate([jnp.zeros((n_ctx, HEAD_DIM), f32), sin_t], axis=0)
    return cos_t, sin_t


def kernel(x, c, ctx, c_ctx, w_ada, b_ada, norm_mix, norm_ffn, w_in, q_norm_a, k_norm_a, hgrn_lb_logits, hgrn_out_norm, q_norm_c, k_norm_c, lambda_q1, lambda_k1, lambda_q2, lambda_k2, diff_sub_norm, w_branch, w_out, w_router, b_router, w_exp_gate, w_exp_up, w_exp_down, w_sh_gate, w_sh_up, w_sh_down):
    n_batch, n_lat, d = x.shape
    n_ctx = ctx.shape[1]
    depth = w_ada.shape[0]
    rows_b = n_ctx + n_lat
    t = n_batch * rows_b
    branch_w = d // 2
    a_heads = branch_w // HEAD_DIM
    a_kv = a_heads // 4
    b_heads = branch_w // HEAD_DIM
    c_heads = branch_w // (2 * HEAD_DIM)
    n_experts = w_router.shape[-1]
    tm = MM_TILE_M if rows_b % MM_TILE_M == 0 else ROW_TILE
    tiles_per_batch = rows_b // tm
    assert n_ctx % ROW_TILE == 0 and n_lat % ROW_TILE == 0 and rows_b % SCAN_TILE == 0

    cos_t, sin_t = _rope_tables(n_ctx, n_lat)
    p_lb = jax.nn.softmax(hgrn_lb_logits.astype(f32), axis=0)
    lower_bounds = jnp.cumsum(p_lb, axis=0) - p_lb[0]
    cond = jnp.zeros((8, d), f32).at[:n_batch].set(c).at[n_batch].set(c_ctx)

    xa = jnp.concatenate([ctx, x], axis=1).reshape(t, d)

    def mod_specs(tn):
        ctx_spec = pl.BlockSpec((1, 1, tn), lambda i, j: (n_batch, 0, j))
        b_spec = pl.BlockSpec((1, 1, tn), lambda i, j: (i // tiles_per_batch, 0, j))
        return ctx_spec, b_spec

    splits = (a_heads * HEAD_DIM, a_kv * HEAD_DIM, a_kv * HEAD_DIM,
              branch_w, branch_w, branch_w, branch_w, branch_w,
              branch_w, branch_w, branch_w, 3 * d)
    offs = [0]
    for s in splits:
        offs.append(offs[-1] + s)

    def cols(w, *ids):
        return jnp.concatenate([w[:, offs[i]:offs[i + 1]] for i in ids], axis=1).astype(bf16)

    n_qk_heads = a_heads + a_kv + 4 * c_heads
    qk_tn = (n_qk_heads // 2) * HEAD_DIM if n_qk_heads % 2 == 0 else n_qk_heads * HEAD_DIM
    sm_scale = HEAD_DIM ** -0.5 * LOG2E

    for l in range(depth):
        lam_init = 0.8 - 0.6 * math.exp(-0.3 * l)
        mod = fused_matmul([(cond, w_ada)], [(b_ada[l].reshape(1, -1), pl.BlockSpec((1, 1024), lambda i, j: (0, j)))],
                           _ep_bias, jax.ShapeDtypeStruct((8, 6 * d), f32), _tile_spec(8, 1024), 8, 1024,
                           a_fn=_silu, name="ada_ln", layer=l)
        mod = mod[:n_batch + 1].reshape(n_batch + 1, 6, 1, d)
        mods = [mod[:, k] for k in range(6)]

        (h,) = norm_modulate(xa, norm_mix[l], mods[0], mods[1], rows_b, n_ctx)
        w_l = w_in[l]
        perm = _rope_perm()
        gains = jnp.concatenate([jnp.tile(q_norm_a[l][perm] * sm_scale, a_heads),
                                 jnp.tile(k_norm_a[l][perm], a_kv),
                                 jnp.tile(q_norm_c[l][perm] * sm_scale, 2 * c_heads),
                                 jnp.tile(k_norm_c[l][perm], 2 * c_heads)])
        w_qk = cols(w_l, 0, 1, 8, 9)
        w_qk = w_qk.reshape(d, n_qk_heads, HEAD_DIM)[:, :, perm].reshape(d, n_qk_heads * HEAD_DIM)
        qk = fused_matmul(
            [(h, w_qk)],
            [(gains.reshape(1, -1).astype(f32), pl.BlockSpec((1, qk_tn), lambda i, j: (0, j))),
             (cos_t, pl.BlockSpec((tm, HEAD_DIM), lambda i, j: (i % tiles_per_batch, 0))),
             (sin_t, pl.BlockSpec((tm, HEAD_DIM), lambda i, j: (i % tiles_per_batch, 0)))],
            _ep_qk, jax.ShapeDtypeStruct((n_qk_heads, t, HEAD_DIM), bf16),
            pl.BlockSpec((qk_tn // HEAD_DIM, tm, HEAD_DIM), lambda i, j: (j, i, 0)), tm, qk_tn, name="in_proj_qk")
        va = fused_matmul([(h, cols(w_l, 2))], [], _ep_store(bf16), jax.ShapeDtypeStruct((t, a_kv * HEAD_DIM), bf16),
                          _tile_spec(tm, a_kv * HEAD_DIM), tm, a_kv * HEAD_DIM, name="in_proj_va")
        vc = fused_matmul([(h, cols(w_l, 10))], [], _ep_store(bf16), jax.ShapeDtypeStruct((t, branch_w), bf16),
                          _tile_spec(tm, 1024), tm, 1024, name="in_proj_vc")
        qio = fused_matmul([(h, cols(w_l, 3, 4, 7))], [], _ep_store(bf16), jax.ShapeDtypeStruct((t, 3 * branch_w), bf16),
                           _tile_spec(tm, 1024), tm, 1024, name="in_proj_hgrn")
        zfb = fused_matmul([(h, cols(w_l, 5, 6))], [], _ep_store(f32), jax.ShapeDtypeStruct((t, 2 * branch_w), f32),
                           _tile_spec(tm, 1024), tm, 1024, name="in_proj_forget")
        gates = fused_matmul([(h, cols(w_l, 11))], [], _ep_sigmoid, jax.ShapeDtypeStruct((t, 3 * d), bf16),
                             _tile_spec(tm, 1024), tm, 1024, name="in_proj_gates")

        dims = dict(n_batch=n_batch, rows_per_batch=rows_b, n_ctx=n_ctx)
        o_a = attention_a(qk, va, a_heads=a_heads, a_kv_heads=a_kv, **dims)
        lam_params = jnp.stack([lambda_q1[l], lambda_k1[l], lambda_q2[l], lambda_k2[l]]).astype(f32)
        o_c = attention_c(qk, vc, lam_params, diff_sub_norm[l], head0_q=a_heads + a_kv,
                          head0_k=a_heads + a_kv + 2 * c_heads, c_heads=c_heads, lam_init=lam_init, **dims)
        o_f, o_r = hgrn_scan(qio, zfb, lower_bounds[l], n_batch=n_batch, rows_per_batch=rows_b, b_heads=b_heads)
        o_b = hgrn_readout(o_f, o_r, qio, hgrn_out_norm[l], b_heads)

        wb = w_branch[l].astype(bf16)
        n_col = d // 512
        y = fused_matmul(
            [(o_a, wb[0]), (o_b, wb[1]), (o_c, wb[2])],
            [(gates, pl.BlockSpec((tm, 512), lambda i, j, r=r: (i, r * n_col + j))) for r in range(3)],
            _ep_merge, jax.ShapeDtypeStruct((t, d), bf16), _tile_spec(tm, 512), tm, 512, name="merge")
        ctx_spec, b_spec = mod_specs(512)
        xa = fused_matmul(
            [(y, w_out[l].astype(bf16))],
            [(xa, _tile_spec(tm, 512)), (mods[2], ctx_spec), (mods[2], b_spec)],
            _make_ep_residual(tm, rows_b, n_ctx, False), jax.ShapeDtypeStruct((t, d), f32),
            _tile_spec(tm, 512), tm, 512, name="out_proj")

        h2, h2_packed = norm_modulate(xa, norm_ffn[l], mods[3], mods[4], rows_b, n_ctx, packed_rows=True)
        top_e, top_w, rank, counts = moe_route(h2, w_router[l].T.astype(bf16), b_router[l])
        dest, slot_tok, block_e = _moe_plan(top_e, rank, counts.reshape(-1))
        ys = moe_experts(h2_packed, slot_tok, w_exp_gate, w_exp_up, w_exp_down, l, block_e)
        routed = moe_combine(ys, dest, top_w)
        d_sh = w_sh_gate.shape[-1]
        act = fused_matmul([(h2, w_sh_gate[l].astype(bf16)), (h2, w_sh_up[l].astype(bf16))], [], _ep_glu,
                           jax.ShapeDtypeStruct((t, d_sh), bf16), _tile_spec(tm, d_sh), tm, d_sh, name="shared_glu")
        xa = fused_matmul(
            [(act, w_sh_down[l].astype(bf16))],
            [(xa, _tile_spec(tm, 512)), (mods[5], ctx_spec), (mods[5], b_spec), (routed, _tile_spec(tm, 512))],
            _make_ep_residual(tm, rows_b, n_ctx, True), jax.ShapeDtypeStruct((t, d), f32),
            _tile_spec(tm, 512), tm, 512, name="moe_out")

    return xa.reshape(n_batch, rows_b, d)[:, n_ctx:]
```

```python
import functools
import math

import jax
import jax.numpy as jnp
from jax import lax
from jax.experimental import pallas as pl
from jax.experimental.pallas import tpu as pltpu

f32 = jnp.float32
bf16 = jnp.bfloat16
i32 = jnp.int32

HEAD_DIM = 128
GRID_W = 64
ROPE_THETA = 10000.0
EPS = 1e-6
F_MIN = 1e-6
SCAN_CHUNK = 64
SCAN_SUB = 8
N_GROUPS = 8
TOPK_GROUPS = 4
TOP_K = 8
ROUTE_SCALE = 2.5
MASK_SCORE = -1e9
LOG2E = 1.4426950408889634

VMEM_LIMIT_BYTES = 48 * 1024 * 1024
ROW_TILE = 256
MM_TILE_M = 768
MOE_TILES = (256, 512)
MOE_VMEM_LIMIT_BYTES = 56 * 1024 * 1024
MXU_COLS = 256
SCAN_TILE = 256


def _params(*semantics, vmem=VMEM_LIMIT_BYTES):
    return pltpu.CompilerParams(dimension_semantics=semantics, vmem_limit_bytes=vmem)


def _silu(x):
    return x * jax.nn.sigmoid(x)


def _nt_dot(a, b):
    return lax.dot_general(a, b, (((1,), (1,)), ((), ())), preferred_element_type=f32)


def _tn_dot(a, b):
    return lax.dot_general(a, b, (((0,), (0,)), ((), ())), preferred_element_type=f32)


def _row_is_ctx(tile_index, tile_rows, rows_per_batch, n_ctx):
    row0 = (tile_index % (rows_per_batch // tile_rows)) * tile_rows
    rid = row0 + lax.broadcasted_iota(i32, (tile_rows, 1), 0)
    return rid < n_ctx


def _fm_kernel(*refs, n_pairs, n_extra, a_fn, epilogue):
    pair_refs = refs[:2 * n_pairs]
    extra = refs[2 * n_pairs:2 * n_pairs + n_extra]
    outs = refs[2 * n_pairs + n_extra:]
    lhs = []
    for p in range(n_pairs):
        a = pair_refs[2 * p][...]
        if a_fn is not None:
            a = a_fn(a)
        lhs.append(a.astype(bf16))
    tn = pair_refs[1].shape[-1]

    def dots(cs):
        accs = []
        for p in range(n_pairs):
            w_ref = pair_refs[2 * p + 1]
            w = w_ref[0, :, cs] if len(w_ref.shape) == 3 else w_ref[:, cs]
            accs.append(jnp.dot(lhs[p], w.astype(bf16), preferred_element_type=f32))
        return accs

    chunks = [slice(c0, min(c0 + MXU_COLS, tn)) for c0 in range(0, tn, MXU_COLS)]
    pending = dots(chunks[0])
    for ci in range(1, len(chunks)):
        accs = dots(chunks[ci])
        epilogue(pending, extra, outs, chunks[ci - 1])
        pending = accs
    epilogue(pending, extra, outs, chunks[-1])


def fused_matmul(pairs, extras, epilogue, out_shapes, out_specs, tm, tn, a_fn=None, name=None, layer=None):
    m = pairs[0][0].shape[0]
    n = pairs[0][1].shape[-1]
    assert m % tm == 0 and n % tn == 0, (m, tm, n, tn)
    in_specs, args = [], []
    for a, w in pairs:
        k = a.shape[1]
        assert w.shape[-2] == k
        if layer is None:
            w_spec = pl.BlockSpec((k, tn), lambda i, j: (0, j))
        else:
            w_spec = pl.BlockSpec((1, k, tn), lambda i, j: (layer, 0, j))
        in_specs += [pl.BlockSpec((tm, k), lambda i, j: (i, 0)), w_spec]
        args += [a, w]
    for arr, spec in extras:
        in_specs.append(spec)
        args.append(arr)
    return pl.pallas_call(
        functools.partial(_fm_kernel, n_pairs=len(pairs), n_extra=len(extras), a_fn=a_fn, epilogue=epilogue),
        grid=(m // tm, n // tn),
        in_specs=in_specs,
        out_specs=out_specs,
        out_shape=out_shapes,
        compiler_params=_params("parallel", "arbitrary"),
        name=name,
    )(*args)


def _tile_spec(tm, tn):
    return pl.BlockSpec((tm, tn), lambda i, j: (i, j))


def _ep_store(dtype):
    def ep(accs, extra, outs, cs):
        outs[0][:, cs] = accs[0].astype(dtype)
    return ep


def _ep_bias(accs, extra, outs, cs):
    outs[0][:, cs] = accs[0] + extra[0][:, cs]


def _ep_sigmoid(accs, extra, outs, cs):
    outs[0][:, cs] = (0.5 * jnp.tanh(0.5 * accs[0]) + 0.5).astype(outs[0].dtype)


def _ep_qk(accs, extra, outs, cs):
    gain_ref, cos_ref, sin_ref = extra
    acc = accs[0]
    cos = cos_ref[...]
    sin = sin_ref[...]
    for c in range(acc.shape[1] // HEAD_DIM):
        head = cs.start // HEAD_DIM + c
        x = acc[:, c * HEAD_DIM:(c + 1) * HEAD_DIM]
        y = x * lax.rsqrt(jnp.mean(x * x, axis=-1, keepdims=True) + EPS)
        y = y * gain_ref[:, head * HEAD_DIM:(head + 1) * HEAD_DIM]
        outs[0][head] = (y * cos + pltpu.roll(y, HEAD_DIM // 2, 1) * sin).astype(outs[0].dtype)


def _ep_merge(accs, extra, outs, cs):
    y = extra[0][:, cs].astype(f32) * accs[0]
    y = y + extra[1][:, cs].astype(f32) * accs[1]
    y = y + extra[2][:, cs].astype(f32) * accs[2]
    outs[0][:, cs] = y.astype(outs[0].dtype)


def _ep_glu(accs, extra, outs, cs):
    outs[0][:, cs] = (_silu(accs[0]) * accs[1]).astype(outs[0].dtype)


def _make_ep_residual(tm, rows_per_batch, n_ctx, with_routed):
    def ep(accs, extra, outs, cs):
        x_ref, mod_ctx_ref, mod_b_ref = extra[:3]
        y = accs[0]
        if with_routed:
            y = y + extra[3][:, cs]
        is_ctx = _row_is_ctx(pl.program_id(0), tm, rows_per_batch, n_ctx)
        mod = jnp.where(is_ctx, mod_ctx_ref[0, :, cs], mod_b_ref[0, :, cs])
        outs[0][:, cs] = x_ref[:, cs] + mod * y
    return ep


PACK_ROWS = 8


def _pack_words(x):
    bits = lax.bitcast_convert_type(x.astype(bf16).astype(f32), jnp.uint32)
    half = bits.shape[1] // 2
    return [bits[:, half + s * HEAD_DIM:half + (s + 1) * HEAD_DIM] | (bits[:, s * HEAD_DIM:(s + 1) * HEAD_DIM] >> 16)
            for s in range(PACK_ROWS)]


def _unpack_word(word):
    return (lax.bitcast_convert_type(word << 16, f32),
            lax.bitcast_convert_type(word & jnp.uint32(0xFFFF0000), f32))


def _norm_kernel(x_ref, w_ref, shc_ref, shb_ref, scc_ref, scb_ref, o_ref, *packed, tile, rows_per_batch, n_ctx):
    x = x_ref[...]
    y = x * lax.rsqrt(jnp.mean(x * x, axis=-1, keepdims=True) + EPS) * w_ref[...]
    is_ctx = _row_is_ctx(pl.program_id(0), tile, rows_per_batch, n_ctx)
    shift = jnp.where(is_ctx, shc_ref[0], shb_ref[0])
    scale = jnp.where(is_ctx, scc_ref[0], scb_ref[0])
    hb = (y * (1.0 + scale) + shift).astype(bf16)
    o_ref[...] = hb
    if packed:
        for s, word in enumerate(_pack_words(hb)):
            packed[0][pl.ds(s, tile, stride=PACK_ROWS), :] = word


def norm_modulate(x, w, shift, scale, rows_per_batch, n_ctx, packed_rows=False):
    t, d = x.shape
    tile = ROW_TILE
    n_b = shift.shape[0] - 1
    tiles_per_batch = rows_per_batch // tile
    ctx_spec = pl.BlockSpec((1, 1, d), lambda i: (n_b, 0, 0))
    b_spec = pl.BlockSpec((1, 1, d), lambda i: (i // tiles_per_batch, 0, 0))
    row_spec = pl.BlockSpec((tile, d), lambda i: (i, 0))
    out_specs, out_shape = [row_spec], [jax.ShapeDtypeStruct((t, d), bf16)]
    if packed_rows:
        assert d == 2 * PACK_ROWS * HEAD_DIM
        out_specs.append(pl.BlockSpec((tile * PACK_ROWS, HEAD_DIM), lambda i: (i, 0)))
        out_shape.append(jax.ShapeDtypeStruct((t * PACK_ROWS, HEAD_DIM), jnp.uint32))
    return pl.pallas_call(
        functools.partial(_norm_kernel, tile=tile, rows_per_batch=rows_per_batch, n_ctx=n_ctx),
        grid=(t // tile,),
        in_specs=[row_spec, pl.BlockSpec((1, d), lambda i: (0, 0)), ctx_spec, b_spec, ctx_spec, b_spec],
        out_specs=out_specs,
        out_shape=out_shape,
        compiler_params=_params("parallel"),
        name="norm_modulate",
    )(x, w.reshape(1, d), shift, shift, scale, scale)


def _attend(streams, kv_len, tk):
    n = kv_len // tk
    state = [None] * len(streams)
    s_next = [_nt_dot(q, k_at(0)) for q, k_at, _ in streams]
    for c in range(n):
        s_cur = s_next
        if c + 1 < n:
            s_next = [_nt_dot(q, k_at((c + 1) * tk)) for q, k_at, _ in streams]
        for i, (_, _, v_at) in enumerate(streams):
            s = s_cur[i]
            s_max = jnp.max(s, axis=1, keepdims=True)
            if c == 0:
                m = s_max
                p = jnp.exp2(s - m)
                l = jnp.sum(p, axis=1, keepdims=True)
                acc = jnp.dot(p.astype(bf16), v_at(c * tk), preferred_element_type=f32)
            else:
                m_old, l, acc = state[i]
                m = jnp.maximum(m_old, s_max)
                alpha = jnp.exp2(m_old - m)
                p = jnp.exp2(s - m)
                l = alpha * l + jnp.sum(p, axis=1, keepdims=True)
                acc = alpha * acc + jnp.dot(p.astype(bf16), v_at(c * tk), preferred_element_type=f32)
            state[i] = (m, l, acc)
    return [acc / l for _, l, acc in state]


def _kv_chunk(kv_len):
    for tk in (768, 1408, 512, 256, 128):
        if kv_len % tk == 0:
            return tk
    raise ValueError(kv_len)


def _ctx_or_all(n_ctx_tiles, n_ctx, kv_all, run):
    is_ctx = pl.program_id(2) < n_ctx_tiles

    @pl.when(is_ctx)
    def _():
        run(n_ctx)

    @pl.when(jnp.logical_not(is_ctx))
    def _():
        run(kv_all)


def _attn_a_kernel(q_ref, k_ref, v_ref, o_ref, *, n_ctx):
    g, tq, d = q_ref.shape

    def run(kv_len):
        tk = _kv_chunk(kv_len)

        def k_at(off):
            return k_ref[0, off:off + tk, :]

        def v_at(off):
            return v_ref[off:off + tk, :]

        (o,) = _attend([(q_ref[...].reshape(g * tq, d), k_at, v_at)], kv_len, tk)
        for h in range(g):
            o_ref[:, h * d:(h + 1) * d] = o[h * tq:(h + 1) * tq].astype(o_ref.dtype)

    _ctx_or_all(n_ctx // tq, n_ctx, k_ref.shape[1], run)


def _attn_c_kernel(lam_ref, dn_ref, q_ref, k_ref, v_ref, o_ref, *, n_ctx, lam_init):
    tq = q_ref.shape[1]

    def run(kv_len):
        tk = _kv_chunk(kv_len)
        def v_at(off):
            return v_ref[off:off + tk, :]

        o1, o2 = _attend([(q_ref[0], lambda off: k_ref[0, off:off + tk, :], v_at),
                          (q_ref[1], lambda off: k_ref[1, off:off + tk, :], v_at)], kv_len, tk)
        lam_p = lam_ref[...]
        lam = (jnp.exp(jnp.sum(lam_p[0:1] * lam_p[1:2], axis=-1, keepdims=True))
               - jnp.exp(jnp.sum(lam_p[2:3] * lam_p[3:4], axis=-1, keepdims=True)) + lam_init)
        o = o1 - lam * o2
        y = o * lax.rsqrt(jnp.mean(o * o, axis=-1, keepdims=True) + EPS) * dn_ref[...]
        o_ref[...] = (y * (1.0 - lam_init)).astype(o_ref.dtype)

    _ctx_or_all(n_ctx // tq, n_ctx, k_ref.shape[1], run)


def attention_a(qk, va, *, n_batch, rows_per_batch, n_ctx, a_heads, a_kv_heads):
    t = qk.shape[1]
    group = a_heads // a_kv_heads
    tq = ROW_TILE
    n_q = rows_per_batch // tq
    return pl.pallas_call(
        functools.partial(_attn_a_kernel, n_ctx=n_ctx),
        grid=(n_batch, a_kv_heads, n_q),
        in_specs=[
            pl.BlockSpec((group, tq, HEAD_DIM), lambda b, h, i: (h, b * n_q + i, 0)),
            pl.BlockSpec((1, rows_per_batch, HEAD_DIM), lambda b, h, i: (a_heads + h, b, 0)),
            pl.BlockSpec((rows_per_batch, HEAD_DIM), lambda b, h, i: (b, h)),
        ],
        out_specs=pl.BlockSpec((tq, group * HEAD_DIM), lambda b, h, i: (b * n_q + i, h)),
        out_shape=jax.ShapeDtypeStruct((t, a_heads * HEAD_DIM), bf16),
        compiler_params=_params("parallel", "parallel", "arbitrary"),
        name="attention_a",
    )(qk, qk, va)


def attention_c(qk, vc, lam_params, diff_norm, *, n_batch, rows_per_batch, n_ctx, head0_q, head0_k,
                c_heads, lam_init):
    t = qk.shape[1]
    dv = 2 * HEAD_DIM
    tq = ROW_TILE
    n_q = rows_per_batch // tq
    return pl.pallas_call(
        functools.partial(_attn_c_kernel, n_ctx=n_ctx, lam_init=lam_init),
        grid=(n_batch, c_heads, n_q),
        in_specs=[
            pl.BlockSpec((4, HEAD_DIM), lambda b, h, i: (0, 0)),
            pl.BlockSpec((1, dv), lambda b, h, i: (0, 0)),
            pl.BlockSpec((2, tq, HEAD_DIM), lambda b, h, i: (head0_q // 2 + h, b * n_q + i, 0)),
            pl.BlockSpec((2, rows_per_batch, HEAD_DIM), lambda b, h, i: (head0_k // 2 + h, b, 0)),
            pl.BlockSpec((rows_per_batch, dv), lambda b, h, i: (b, h)),
        ],
        out_specs=pl.BlockSpec((tq, dv), lambda b, h, i: (b * n_q + i, h)),
        out_shape=jax.ShapeDtypeStruct((t, c_heads * dv), bf16),
        compiler_params=_params("parallel", "parallel", "arbitrary"),
        name="attention_c",
    )(lam_params, diff_norm.reshape(1, dv), qk, qk, vc)


def _hgrn_tables(rev):
    c = SCAN_CHUNK
    half = SCAN_SUB // 2
    t_i = lax.broadcasted_iota(i32, (c, c), 0)
    u_i = lax.broadcasted_iota(i32, (c, c), 1)
    blk8 = t_i & ~(SCAN_SUB - 1)
    blk4 = t_i & ~(half - 1)
    if rev:
        rows_upto = [t_i, blk4 + half - 1, blk8 + half, blk8]
        sums = jnp.concatenate([(u_i >= r).astype(f32) for r in rows_upto], axis=0)
    else:
        rows_upto = [t_i, blk4, blk8 + half - 1, blk8 + SCAN_SUB - 1]
        sums = jnp.concatenate([(u_i <= r).astype(f32) for r in rows_upto], axis=0)
    causal = (u_i >= t_i) if rev else (u_i <= t_i)
    same8 = (u_i & ~(SCAN_SUB - 1)) == blk8
    same4 = (u_i & ~(half - 1)) == blk4
    return sums, same8 & jnp.logical_not(same4) & causal, same4 & causal


def _hgrn_gates(qp, z, lb):
    q = _silu(qp.astype(f32))
    one_m = 1.0 - lb
    g = jnp.log(jnp.maximum(lb + one_m * jax.nn.sigmoid(z), F_MIN))
    k = one_m * jax.nn.sigmoid(-z)
    return q, k, g


def _hgrn_local(q, k, b_all, v, tables, rev):
    c = SCAN_CHUNK
    nb = c // SCAN_SUB
    _, across, inner = tables
    b, rho4, rho_mid, rho_edge = (b_all[i * c:(i + 1) * c] for i in range(4))
    end = 0 if rev else c - 1
    b_end = b[end:end + 1, :]
    q_in = (q * jnp.exp(b)).astype(bf16)
    d_st = _tn_dot(v, (k * jnp.exp(b_end - b)).astype(bf16))

    def edge_row(j):
        return j * SCAN_SUB if rev else j * SCAN_SUB + SCAN_SUB - 1

    row = lax.broadcasted_iota(i32, (c, HEAD_DIM), 0)
    k_til = k * jnp.exp(rho_edge - b)
    qs, ks = [], []
    for j in (range(1, nb) if rev else range(nb - 1)):
        rho = b[edge_row(j):edge_row(j) + 1, :]
        q_rows = (row < j * SCAN_SUB) if rev else (row >= (j + 1) * SCAN_SUB)
        qs.append((q * jnp.exp(jnp.where(q_rows, b - rho, -jnp.inf))).astype(bf16))
        k_rows = (row >= j * SCAN_SUB) & (row < (j + 1) * SCAN_SUB)
        ks.append(jnp.where(k_rows, k_til, 0.0).astype(bf16))
    att = _nt_dot(jnp.concatenate(qs, axis=1), jnp.concatenate(ks, axis=1))

    att_mid = _nt_dot((q * jnp.exp(jnp.minimum(b - rho_mid, 0.0))).astype(bf16),
                      (k * jnp.exp(jnp.minimum(rho_mid - b, 0.0))).astype(bf16))
    att_in = _nt_dot((q * jnp.exp(b - rho4)).astype(bf16), (k * jnp.exp(rho4 - b)).astype(bf16))
    att = att + jnp.where(across, att_mid, 0.0) + jnp.where(inner, att_in, 0.0)
    return jnp.dot(att.astype(bf16), v, preferred_element_type=f32), q_in, d_st, jnp.exp(b_end)


def _hgrn_kernel(lb_ref, qf_ref, vf_ref, zf_ref, qb_ref, vb_ref, zb_ref, of_ref, ob_ref, sf_ref, sb_ref):
    @pl.when(pl.program_id(2) == 0)
    def _():
        sf_ref[...] = jnp.zeros_like(sf_ref)
        sb_ref[...] = jnp.zeros_like(sb_ref)

    lb = lb_ref[...]
    n_chunks = qf_ref.shape[0] // SCAN_CHUNK
    work = []
    for c in range(n_chunks):
        work.append((slice(c * SCAN_CHUNK, (c + 1) * SCAN_CHUNK), False))
        work.append((slice((n_chunks - 1 - c) * SCAN_CHUNK, (n_chunks - c) * SCAN_CHUNK), True))
    refs = {False: (qf_ref, vf_ref, zf_ref, of_ref), True: (qb_ref, vb_ref, zb_ref, ob_ref)}
    tables = {False: _hgrn_tables(False), True: _hgrn_tables(True)}

    gates = [_hgrn_gates(refs[rev][0][rows, :], refs[rev][2][rows, :], lb) for rows, rev in work]
    sums = [jnp.dot(tables[rev][0], g, precision=lax.Precision.HIGHEST, preferred_element_type=f32)
            for (_, rev), (_, _, g) in zip(work, gates)]
    local = [_hgrn_local(q, k, b_all, refs[rev][1][rows, :], tables[rev], rev)
             for (rows, rev), (q, k, _), b_all in zip(work, gates, sums)]
    state = {False: sf_ref[...], True: sb_ref[...]}
    for (rows, rev), (o_in, q_in, d_st, decay) in zip(work, local):
        st = state[rev]
        refs[rev][3][rows, :] = o_in + _nt_dot(q_in, st.astype(bf16))
        state[rev] = st * decay + d_st
    sf_ref[...] = state[False]
    sb_ref[...] = state[True]


def hgrn_scan(qio, zfb, lower_bound, *, n_batch, rows_per_batch, b_heads):
    t = qio.shape[0]
    tile = SCAN_TILE
    n_tiles = rows_per_batch // tile

    def fwd(b, h, j):
        return b * n_tiles + j

    def bwd(b, h, j):
        return b * n_tiles + jnp.where(j == 0, 0, n_tiles - j)

    def spec(tile_fn, col0):
        return pl.BlockSpec((tile, HEAD_DIM), lambda b, h, j: (tile_fn(b, h, j), col0 + h))

    return pl.pallas_call(
        _hgrn_kernel,
        grid=(n_batch, b_heads, n_tiles),
        in_specs=[
            pl.BlockSpec((1, HEAD_DIM), lambda b, h, j: (0, h)),
            spec(fwd, 0), spec(fwd, b_heads), spec(fwd, 0),
            spec(bwd, 0), spec(bwd, b_heads), spec(bwd, b_heads),
        ],
        out_specs=[spec(fwd, 0), spec(bwd, 0)],
        out_shape=[jax.ShapeDtypeStruct((t, b_heads * HEAD_DIM), f32)] * 2,
        scratch_shapes=[pltpu.VMEM((HEAD_DIM, HEAD_DIM), f32)] * 2,
        compiler_params=_params("parallel", "parallel", "arbitrary"),
        name="hgrn_scan",
    )(lower_bound.reshape(1, -1), qio, qio, zfb, qio, qio, zfb)


def _hgrn_out_kernel(of_ref, ob_ref, og_ref, w_ref, o_ref):
    o = of_ref[...] + ob_ref[...]
    w = w_ref[...]
    for c in range(o.shape[1] // HEAD_DIM):
        sl = slice(c * HEAD_DIM, (c + 1) * HEAD_DIM)
        x = o[:, sl]
        y = x * lax.rsqrt(jnp.mean(x * x, axis=-1, keepdims=True) + EPS) * w
        o_ref[:, sl] = (y * _silu(og_ref[:, sl].astype(f32))).astype(o_ref.dtype)


def hgrn_readout(o_f, o_b, qio, hgrn_norm, b_heads):
    t, w = o_f.shape
    tile = ROW_TILE
    row_spec = pl.BlockSpec((tile, w), lambda i: (i, 0))
    return pl.pallas_call(
        _hgrn_out_kernel,
        grid=(t // tile,),
        in_specs=[row_spec, row_spec, pl.BlockSpec((tile, w), lambda i: (i, 2)),
                  pl.BlockSpec((1, HEAD_DIM), lambda i: (0, 0))],
        out_specs=row_spec,
        out_shape=jax.ShapeDtypeStruct((t, w), bf16),
        compiler_params=_params("parallel"),
        name="hgrn_readout",
    )(o_f, o_b, qio, hgrn_norm.reshape(1, HEAD_DIM))


def _router_kernel(h_ref, w_ref, b_ref, e_ref, wt_ref, rank_ref, cnt_ref, run_ref):
    n_e = w_ref.shape[0]
    tm = h_ref.shape[0]
    per = n_e // N_GROUPS

    @pl.when(pl.program_id(0) == 0)
    def _():
        run_ref[...] = jnp.zeros_like(run_ref)

    scores = jax.nn.sigmoid(_nt_dot(w_ref[...], h_ref[...]))
    biased = scores + b_ref[...]
    neg = -jnp.inf

    grp = biased.reshape(N_GROUPS, per, tm)
    idx_in = lax.broadcasted_iota(i32, grp.shape, 1)
    m1 = jnp.max(grp, axis=1, keepdims=True)
    first = jnp.min(jnp.where(grp == m1, idx_in, per), axis=1, keepdims=True)
    m2 = jnp.max(jnp.where(idx_in == first, neg, grp), axis=1, keepdims=True)
    cur = m1 + m2

    g_idx = lax.broadcasted_iota(i32, cur.shape, 0)
    g_sel = jnp.zeros(cur.shape, f32)
    for _ in range(TOPK_GROUPS):
        mx = jnp.max(cur, axis=0, keepdims=True)
        pick = g_idx == jnp.min(jnp.where(cur == mx, g_idx, N_GROUPS), axis=0, keepdims=True)
        g_sel = jnp.where(pick, 1.0, g_sel)
        cur = jnp.where(pick, neg, cur)
    e_mask = jnp.broadcast_to(g_sel, (N_GROUPS, per, tm)).reshape(n_e, tm) > 0.0

    e_idx = lax.broadcasted_iota(i32, (n_e, tm), 0)
    cur = jnp.where(e_mask, biased, MASK_SCORE)
    ws, picks = [], []
    member = jnp.zeros((n_e, tm), f32)
    for r in range(TOP_K):
        mx = jnp.max(cur, axis=0, keepdims=True)
        first = jnp.min(jnp.where(cur == mx, e_idx, n_e), axis=0, keepdims=True)
        pick = e_idx == first
        e_ref[r:r + 1, :] = first
        ws.append(jnp.sum(jnp.where(pick, scores, 0.0), axis=0, keepdims=True))
        cur = jnp.where(pick, neg, cur)
        member = jnp.where(pick, 1.0, member)
        picks.append(pick)
    total = ws[0]
    for r in range(1, TOP_K):
        total = total + ws[r]
    for r in range(TOP_K):
        wt_ref[r:r + 1, :] = ws[r] / total * ROUTE_SCALE

    u_i = lax.broadcasted_iota(i32, (tm, tm), 0)
    t_i = lax.broadcasted_iota(i32, (tm, tm), 1)
    prefix = jnp.dot(member.astype(bf16), (u_i <= t_i).astype(bf16), preferred_element_type=f32)
    rank = run_ref[...] + prefix
    for r in range(TOP_K):
        rank_ref[r:r + 1, :] = jnp.sum(jnp.where(picks[r], rank, 0.0), axis=0, keepdims=True).astype(i32)
    run_ref[...] = run_ref[...] + jnp.sum(member, axis=1, keepdims=True)
    cnt_ref[...] = run_ref[...].astype(i32)


def moe_route(h, w_router_t, b_router):
    t, d = h.shape
    n_e = w_router_t.shape[0]
    tm = 512
    assert t % tm == 0
    out_spec = pl.BlockSpec((TOP_K, tm), lambda i: (0, i))
    return pl.pallas_call(
        _router_kernel,
        grid=(t // tm,),
        in_specs=[pl.BlockSpec((tm, d), lambda i: (i, 0)), pl.BlockSpec((n_e, d), lambda i: (0, 0)),
                  pl.BlockSpec((n_e, 1), lambda i: (0, 0))],
        out_specs=[out_spec, out_spec, out_spec, pl.BlockSpec((n_e, 1), lambda i: (0, 0))],
        out_shape=[jax.ShapeDtypeStruct((TOP_K, t), i32), jax.ShapeDtypeStruct((TOP_K, t), f32),
                   jax.ShapeDtypeStruct((TOP_K, t), i32), jax.ShapeDtypeStruct((n_e, 1), i32)],
        scratch_shapes=[pltpu.VMEM((n_e, 1), f32)],
        compiler_params=_params("arbitrary"),
        name="moe_route",
    )(h, w_router_t, b_router.reshape(n_e, 1).astype(f32))


GATHER_DEPTH = 3


def _expert_kernel(be_ref, idx0_ref, idx1_ref, idx_ahead_ref, h_ref, wg_ref, wu_ref, wd_ref, y_ref,
                   xbuf, sem, x_s, wg_b, wu_b, wd_b):
    i = pl.program_id(0)
    last = pl.num_programs(0) - 1
    tm = xbuf.shape[1] // PACK_ROWS
    slot = i % GATHER_DEPTH
    ahead = (i + GATHER_DEPTH - 1) % GATHER_DEPTH

    def gather_start(idx_ref, dst_slot):
        for r in range(tm):
            src_row = pl.multiple_of(idx_ref[0, 0, r] * PACK_ROWS, PACK_ROWS)
            pltpu.make_async_copy(h_ref.at[pl.ds(src_row, PACK_ROWS)],
                                  xbuf.at[dst_slot, pl.ds(r * PACK_ROWS, PACK_ROWS)],
                                  sem.at[dst_slot]).start(priority=r % 2)

    def gather_wait(dst_slot):
        pltpu.make_async_copy(h_ref.at[pl.ds(0, tm * PACK_ROWS)], xbuf.at[dst_slot], sem.at[dst_slot]).wait()

    @pl.when(i == 0)
    def _():
        gather_start(idx0_ref, 0)
        gather_start(idx1_ref, 1)

    @pl.when((i == 0) | (be_ref[i] != be_ref[jnp.maximum(i - 1, 0)]))
    def _():
        wg_b[...] = wg_ref[0, 0].astype(bf16)
        wu_b[...] = wu_ref[0, 0].astype(bf16)
        wd_b[...] = wd_ref[0, 0].astype(bf16)

    gather_wait(slot)
    half = PACK_ROWS * HEAD_DIM
    for s in range(PACK_ROWS):
        lo, hi = _unpack_word(xbuf[slot, pl.ds(s, tm, stride=PACK_ROWS), :])
        x_s[:, s * HEAD_DIM:(s + 1) * HEAD_DIM] = lo.astype(bf16)
        x_s[:, half + s * HEAD_DIM:half + (s + 1) * HEAD_DIM] = hi.astype(bf16)
    gather_start(idx_ahead_ref, ahead)
    x = x_s[...]
    g = jnp.dot(x, wg_b[...], preferred_element_type=f32)
    u = jnp.dot(x, wu_b[...], preferred_element_type=f32)
    a = (_silu(g) * u).astype(bf16)
    y = jnp.dot(a, wd_b[...], preferred_element_type=f32)
    for s, word in enumerate(_pack_words(y)):
        y_ref[pl.ds(s, tm, stride=PACK_ROWS), :] = word

    @pl.when(i == last)
    def _():
        for k in range(1, GATHER_DEPTH):
            gather_wait((i + k) % GATHER_DEPTH)


def moe_experts(h_packed, slot_tok, w_gate, w_up, w_down, layer, block_e):
    tm = slot_tok.shape[0] // block_e.shape[0]
    n_blocks = block_e.shape[0]
    assert n_blocks >= GATHER_DEPTH
    d, d_e = w_gate.shape[-2:]
    idx = slot_tok.reshape(n_blocks, 1, tm)

    def idx_spec(block_fn):
        return pl.BlockSpec((1, 1, tm), lambda i, be: (block_fn(i), 0, 0), memory_space=pltpu.SMEM)

    def w_spec(shape):
        return pl.BlockSpec((1, 1) + shape, lambda i, be: (layer, be[i], 0, 0))

    return pl.pallas_call(
        _expert_kernel,
        grid_spec=pltpu.PrefetchScalarGridSpec(
            num_scalar_prefetch=1,
            grid=(n_blocks,),
            in_specs=[idx_spec(lambda i: 0), idx_spec(lambda i: 1),
                      idx_spec(lambda i: jnp.minimum(i + GATHER_DEPTH - 1, n_blocks - 1)),
                      pl.BlockSpec(memory_space=pl.ANY),
                      w_spec((d, d_e)), w_spec((d, d_e)), w_spec((d_e, d))],
            out_specs=pl.BlockSpec((tm * PACK_ROWS, HEAD_DIM), lambda i, be: (i, 0)),
            scratch_shapes=[pltpu.VMEM((GATHER_DEPTH, tm * PACK_ROWS, HEAD_DIM), jnp.uint32),
                            pltpu.SemaphoreType.DMA((GATHER_DEPTH,)), pltpu.VMEM((tm, d), bf16),
                            pltpu.VMEM((d, d_e), bf16), pltpu.VMEM((d, d_e), bf16), pltpu.VMEM((d_e, d), bf16)],
        ),
        out_shape=jax.ShapeDtypeStruct((n_blocks * tm * PACK_ROWS, HEAD_DIM), jnp.uint32),
        compiler_params=_params("arbitrary", vmem=MOE_VMEM_LIMIT_BYTES),
        name="moe_experts",
    )(block_e, idx, idx, idx, h_packed, w_gate, w_up, w_down)


def _combine_kernel(idx_ref, idx_next_ref, w_ref, y_ref, o_ref, buf, sem):
    i = pl.program_id(0)
    last = pl.num_programs(0) - 1
    n_k = buf.shape[1]
    rows = buf.shape[2] // PACK_ROWS
    slot = i % 2

    def gather_start(src_idx_ref, dst_slot):
        for r in range(rows):
            for kk in range(n_k):
                src_row = pl.multiple_of(src_idx_ref[0, kk, r] * PACK_ROWS, PACK_ROWS)
                pltpu.make_async_copy(y_ref.at[pl.ds(src_row, PACK_ROWS)],
                                      buf.at[dst_slot, kk, pl.ds(r * PACK_ROWS, PACK_ROWS)],
                                      sem.at[dst_slot]).start(priority=(r * n_k + kk) % 2)

    def gather_wait(dst_slot):
        for kk in range(n_k):
            pltpu.make_async_copy(y_ref.at[pl.ds(0, rows * PACK_ROWS)], buf.at[dst_slot, kk],
                                  sem.at[dst_slot]).wait()

    @pl.when(i == 0)
    def _():
        gather_start(idx_ref, 0)

    gather_start(idx_next_ref, 1 - slot)
    gather_wait(slot)
    w = w_ref[...]
    half = PACK_ROWS * HEAD_DIM
    for s in range(PACK_ROWS):
        acc_lo = acc_hi = None
        for kk in range(n_k):
            lo, hi = _unpack_word(buf[slot, kk, pl.ds(s, rows, stride=PACK_ROWS), :])
            w_k = w[:, kk:kk + 1]
            acc_lo = lo * w_k if kk == 0 else acc_lo + lo * w_k
            acc_hi = hi * w_k if kk == 0 else acc_hi + hi * w_k
        o_ref[:, s * HEAD_DIM:(s + 1) * HEAD_DIM] = acc_lo
        o_ref[:, half + s * HEAD_DIM:half + (s + 1) * HEAD_DIM] = acc_hi

    @pl.when(i == last)
    def _():
        gather_wait(1 - slot)


def moe_combine(y, dest, top_w):
    n_k, t = dest.shape
    d = 2 * PACK_ROWS * HEAD_DIM
    rows = 128
    n_tiles = t // rows
    idx = dest.reshape(n_k, n_tiles, rows).transpose(1, 0, 2)

    def idx_spec(tile_fn):
        return pl.BlockSpec((1, n_k, rows), lambda i: (tile_fn(i), 0, 0), memory_space=pltpu.SMEM)

    return pl.pallas_call(
        _combine_kernel,
        grid=(n_tiles,),
        in_specs=[idx_spec(lambda i: i), idx_spec(lambda i: jnp.minimum(i + 1, n_tiles - 1)),
                  pl.BlockSpec((rows, n_k), lambda i: (i, 0)),
                  pl.BlockSpec(memory_space=pl.ANY)],
        out_specs=pl.BlockSpec((rows, d), lambda i: (i, 0)),
        out_shape=jax.ShapeDtypeStruct((t, d), f32),
        scratch_shapes=[pltpu.VMEM((2, n_k, rows * PACK_ROWS, HEAD_DIM), jnp.uint32),
                        pltpu.SemaphoreType.DMA((2,))],
        compiler_params=_params("arbitrary"),
        name="moe_combine",
    )(idx, idx, top_w.T, y)


def _moe_plan(top_e, rank, counts, tm):
    n_k, t = top_e.shape
    n_experts = counts.shape[0]
    n_blocks = -(-(n_k * t) // tm) + n_experts
    padded = (counts + tm - 1) // tm * tm
    pad_end = jnp.cumsum(padded)
    pad_start = pad_end - padded
    onehot = top_e[:, :, None] == jnp.arange(n_experts, dtype=i32)
    dest = jnp.sum(jnp.where(onehot, pad_start, 0), axis=-1) + rank - 1
    n_used = pad_end[-1] // tm
    blk = jnp.minimum(jnp.arange(n_blocks, dtype=i32), n_used - 1)
    block_e = jnp.sum((pad_end[None, :] <= (blk * tm)[:, None]).astype(i32), axis=1)
    block_e = jnp.minimum(block_e, n_experts - 1)
    tok = jnp.broadcast_to(jnp.arange(t, dtype=i32)[None, :], (n_k, t))
    sorted_tok = jnp.sort((top_e * t + tok).reshape(-1)) % t
    of_block = block_e[:, None] == jnp.arange(n_experts, dtype=i32)
    grp_start = jnp.cumsum(counts) - counts

    def per_block(table):
        return jnp.sum(jnp.where(of_block, table, 0), axis=1, keepdims=True)

    j = jnp.arange(n_blocks * tm, dtype=i32).reshape(n_blocks, tm) - per_block(pad_start)
    valid = (j >= 0) & (j < per_block(counts))
    src = jnp.clip(per_block(grp_start) + j, 0, n_k * t - 1)
    slot_tok = jnp.where(valid, jnp.take(sorted_tok, src.reshape(-1)).reshape(n_blocks, tm), 0)
    return dest.astype(i32), slot_tok.reshape(-1), block_e


def _rope_perm():
    quarter = HEAD_DIM // 4
    return jnp.concatenate([jnp.arange(q * quarter, (q + 1) * quarter, dtype=i32) for q in (0, 2, 1, 3)])


def _rope_tables(n_ctx, n_lat):
    t = jnp.arange(n_lat)
    pos = jnp.stack([t // GRID_W, t % GRID_W], axis=-1).astype(f32)
    n_freq = HEAD_DIM // 4
    inv_freq = ROPE_THETA ** (-jnp.arange(n_freq, dtype=f32) / n_freq)
    ang = pos[:, :, None] * inv_freq
    cos, sin = jnp.cos(ang), jnp.sin(ang)
    cos_t = jnp.concatenate([cos[:, 0], cos[:, 1], cos[:, 0], cos[:, 1]], axis=-1)
    sin_t = jnp.concatenate([-sin[:, 0], -sin[:, 1], sin[:, 0], sin[:, 1]], axis=-1)
    cos_t = jnp.concatenate([jnp.ones((n_ctx, HEAD_DIM), f32), cos_t], axis=0)
    sin_t = jnp.concatenate([jnp.zeros((n_ctx, HEAD_DIM), f32), sin_t], axis=0)
    return cos_t, sin_t


def kernel(x, c, ctx, c_ctx, w_ada, b_ada, norm_mix, norm_ffn, w_in, q_norm_a, k_norm_a, hgrn_lb_logits, hgrn_out_norm, q_norm_c, k_norm_c, lambda_q1, lambda_k1, lambda_q2, lambda_k2, diff_sub_norm, w_branch, w_out, w_router, b_router, w_exp_gate, w_exp_up, w_exp_down, w_sh_gate, w_sh_up, w_sh_down):
    n_batch, n_lat, d = x.shape
    n_ctx = ctx.shape[1]
    depth = w_ada.shape[0]
    rows_b = n_ctx + n_lat
    t = n_batch * rows_b
    branch_w = d // 2
    a_heads = branch_w // HEAD_DIM
    a_kv = a_heads // 4
    b_heads = branch_w // HEAD_DIM
    c_heads = branch_w // (2 * HEAD_DIM)
    n_experts = w_router.shape[-1]
    tm = MM_TILE_M if rows_b % MM_TILE_M == 0 else ROW_TILE
    tiles_per_batch = rows_b // tm
    assert n_ctx % ROW_TILE == 0 and n_lat % ROW_TILE == 0 and rows_b % SCAN_TILE == 0

    cos_t, sin_t = _rope_tables(n_ctx, n_lat)
    p_lb = jax.nn.softmax(hgrn_lb_logits.astype(f32), axis=0)
    lower_bounds = jnp.cumsum(p_lb, axis=0) - p_lb[0]
    cond = jnp.zeros((8, d), f32).at[:n_batch].set(c).at[n_batch].set(c_ctx)

    xa = jnp.concatenate([ctx, x], axis=1).reshape(t, d)

    def mod_specs(tn):
        ctx_spec = pl.BlockSpec((1, 1, tn), lambda i, j: (n_batch, 0, j))
        b_spec = pl.BlockSpec((1, 1, tn), lambda i, j: (i // tiles_per_batch, 0, j))
        return ctx_spec, b_spec

    splits = (a_heads * HEAD_DIM, a_kv * HEAD_DIM, a_kv * HEAD_DIM,
              branch_w, branch_w, branch_w, branch_w, branch_w,
              branch_w, branch_w, branch_w, 3 * d)
    offs = [0]
    for s in splits:
        offs.append(offs[-1] + s)

    def cols(w, *ids):
        return jnp.concatenate([w[:, offs[i]:offs[i + 1]] for i in ids], axis=1).astype(bf16)

    n_qk_heads = a_heads + a_kv + 4 * c_heads
    qk_tn = (n_qk_heads // 2) * HEAD_DIM if n_qk_heads % 2 == 0 else n_qk_heads * HEAD_DIM
    sm_scale = HEAD_DIM ** -0.5 * LOG2E

    for l in range(depth):
        lam_init = 0.8 - 0.6 * math.exp(-0.3 * l)
        mod = fused_matmul([(cond, w_ada)], [(b_ada[l].reshape(1, -1), pl.BlockSpec((1, 1024), lambda i, j: (0, j)))],
                           _ep_bias, jax.ShapeDtypeStruct((8, 6 * d), f32), _tile_spec(8, 1024), 8, 1024,
                           a_fn=_silu, name="ada_ln", layer=l)
        mod = mod[:n_batch + 1].reshape(n_batch + 1, 6, 1, d)
        mods = [mod[:, k] for k in range(6)]

        (h,) = norm_modulate(xa, norm_mix[l], mods[0], mods[1], rows_b, n_ctx)
        w_l = w_in[l]
        perm = _rope_perm()
        gains = jnp.concatenate([jnp.tile(q_norm_a[l][perm] * sm_scale, a_heads),
                                 jnp.tile(k_norm_a[l][perm], a_kv),
                                 jnp.tile(q_norm_c[l][perm] * sm_scale, 2 * c_heads),
                                 jnp.tile(k_norm_c[l][perm], 2 * c_heads)])
        w_qk = cols(w_l, 0, 1, 8, 9)
        w_qk = w_qk.reshape(d, n_qk_heads, HEAD_DIM)[:, :, perm].reshape(d, n_qk_heads * HEAD_DIM)
        qk = fused_matmul(
            [(h, w_qk)],
            [(gains.reshape(1, -1).astype(f32), pl.BlockSpec((1, qk_tn), lambda i, j: (0, j))),
             (cos_t, pl.BlockSpec((tm, HEAD_DIM), lambda i, j: (i % tiles_per_batch, 0))),
             (sin_t, pl.BlockSpec((tm, HEAD_DIM), lambda i, j: (i % tiles_per_batch, 0)))],
            _ep_qk, jax.ShapeDtypeStruct((n_qk_heads, t, HEAD_DIM), bf16),
            pl.BlockSpec((qk_tn // HEAD_DIM, tm, HEAD_DIM), lambda i, j: (j, i, 0)), tm, qk_tn, name="in_proj_qk")
        va = fused_matmul([(h, cols(w_l, 2))], [], _ep_store(bf16), jax.ShapeDtypeStruct((t, a_kv * HEAD_DIM), bf16),
                          _tile_spec(tm, a_kv * HEAD_DIM), tm, a_kv * HEAD_DIM, name="in_proj_va")
        vc = fused_matmul([(h, cols(w_l, 10))], [], _ep_store(bf16), jax.ShapeDtypeStruct((t, branch_w), bf16),
                          _tile_spec(tm, 1024), tm, 1024, name="in_proj_vc")
        qio = fused_matmul([(h, cols(w_l, 3, 4, 7))], [], _ep_store(bf16), jax.ShapeDtypeStruct((t, 3 * branch_w), bf16),
                           _tile_spec(tm, 1024), tm, 1024, name="in_proj_hgrn")
        zfb = fused_matmul([(h, cols(w_l, 5, 6))], [], _ep_store(f32), jax.ShapeDtypeStruct((t, 2 * branch_w), f32),
                           _tile_spec(tm, 1024), tm, 1024, name="in_proj_forget")
        gates = fused_matmul([(h, cols(w_l, 11))], [], _ep_sigmoid, jax.ShapeDtypeStruct((t, 3 * d), bf16),
                             _tile_spec(tm, 1024), tm, 1024, name="in_proj_gates")

        dims = dict(n_batch=n_batch, rows_per_batch=rows_b, n_ctx=n_ctx)
        o_a = attention_a(qk, va, a_heads=a_heads, a_kv_heads=a_kv, **dims)
        lam_params = jnp.stack([lambda_q1[l], lambda_k1[l], lambda_q2[l], lambda_k2[l]]).astype(f32)
        o_c = attention_c(qk, vc, lam_params, diff_sub_norm[l], head0_q=a_heads + a_kv,
                          head0_k=a_heads + a_kv + 2 * c_heads, c_heads=c_heads, lam_init=lam_init, **dims)
        o_f, o_r = hgrn_scan(qio, zfb, lower_bounds[l], n_batch=n_batch, rows_per_batch=rows_b, b_heads=b_heads)
        o_b = hgrn_readout(o_f, o_r, qio, hgrn_out_norm[l], b_heads)

        wb = w_branch[l].astype(bf16)
        n_col = d // 512
        y = fused_matmul(
            [(o_a, wb[0]), (o_b, wb[1]), (o_c, wb[2])],
            [(gates, pl.BlockSpec((tm, 512), lambda i, j, r=r: (i, r * n_col + j))) for r in range(3)],
            _ep_merge, jax.ShapeDtypeStruct((t, d), bf16), _tile_spec(tm, 512), tm, 512, name="merge")
        ctx_spec, b_spec = mod_specs(512)
        xa = fused_matmul(
            [(y, w_out[l].astype(bf16))],
            [(xa, _tile_spec(tm, 512)), (mods[2], ctx_spec), (mods[2], b_spec)],
            _make_ep_residual(tm, rows_b, n_ctx, False), jax.ShapeDtypeStruct((t, d), f32),
            _tile_spec(tm, 512), tm, 512, name="out_proj")

        h2, h2_packed = norm_modulate(xa, norm_ffn[l], mods[3], mods[4], rows_b, n_ctx, packed_rows=True)
        top_e, top_w, rank, counts = moe_route(h2, w_router[l].T.astype(bf16), b_router[l])
        dest, slot_tok, block_e = _moe_plan(top_e, rank, counts.reshape(-1), MOE_TILES[l % len(MOE_TILES)])
        ys = moe_experts(h2_packed, slot_tok, w_exp_gate, w_exp_up, w_exp_down, l, block_e)
        routed = moe_combine(ys, dest, top_w)
        d_sh = w_sh_gate.shape[-1]
        act = fused_matmul([(h2, w_sh_gate[l].astype(bf16)), (h2, w_sh_up[l].astype(bf16))], [], _ep_glu,
                           jax.ShapeDtypeStruct((t, d_sh), bf16), _tile_spec(tm, d_sh), tm, d_sh, name="shared_glu")
        xa = fused_matmul(
            [(act, w_sh_down[l].astype(bf16))],
            [(xa, _tile_spec(tm, 512)), (mods[5], ctx_spec), (mods[5], b_spec), (routed, _tile_spec(tm, 512))],
            _make_ep_residual(tm, rows_b, n_ctx, True), jax.ShapeDtypeStruct((t, d), f32),
            _tile_spec(tm, 512), tm, 512, name="moe_out")

    return xa.reshape(n_batch, rows_b, d)[:, n_ctx:]
```

```python
import functools
import math

import jax
import jax.numpy as jnp
from jax import lax
from jax.experimental import pallas as pl
from jax.experimental.pallas import tpu as pltpu

f32 = jnp.float32
bf16 = jnp.bfloat16
i32 = jnp.int32

HEAD_DIM = 128
GRID_W = 64
ROPE_THETA = 10000.0
EPS = 1e-6
F_MIN = 1e-6
SCAN_CHUNK = 64
SCAN_SUB = 8
N_GROUPS = 8
TOPK_GROUPS = 4
TOP_K = 8
ROUTE_SCALE = 2.5
MASK_SCORE = -1e9
LOG2E = 1.4426950408889634

VMEM_LIMIT_BYTES = 48 * 1024 * 1024
ROW_TILE = 256
MM_TILE_M = 768
MOE_TILE = 256
MXU_COLS = 256
SCAN_TILE = 256


def _params(*semantics):
    return pltpu.CompilerParams(dimension_semantics=semantics, vmem_limit_bytes=VMEM_LIMIT_BYTES)


def _silu(x):
    return x * jax.nn.sigmoid(x)


def _nt_dot(a, b):
    return lax.dot_general(a, b, (((1,), (1,)), ((), ())), preferred_element_type=f32)


def _tn_dot(a, b):
    return lax.dot_general(a, b, (((0,), (0,)), ((), ())), preferred_element_type=f32)


def _row_is_ctx(tile_index, tile_rows, rows_per_batch, n_ctx):
    row0 = (tile_index % (rows_per_batch // tile_rows)) * tile_rows
    rid = row0 + lax.broadcasted_iota(i32, (tile_rows, 1), 0)
    return rid < n_ctx


def _fm_kernel(*refs, n_pairs, n_extra, a_fn, epilogue):
    pair_refs = refs[:2 * n_pairs]
    extra = refs[2 * n_pairs:2 * n_pairs + n_extra]
    outs = refs[2 * n_pairs + n_extra:]
    lhs = []
    for p in range(n_pairs):
        a = pair_refs[2 * p][...]
        if a_fn is not None:
            a = a_fn(a)
        lhs.append(a.astype(bf16))
    tn = pair_refs[1].shape[-1]

    def dots(cs):
        accs = []
        for p in range(n_pairs):
            w_ref = pair_refs[2 * p + 1]
            w = w_ref[0, :, cs] if len(w_ref.shape) == 3 else w_ref[:, cs]
            accs.append(jnp.dot(lhs[p], w.astype(bf16), preferred_element_type=f32))
        return accs

    chunks = [slice(c0, min(c0 + MXU_COLS, tn)) for c0 in range(0, tn, MXU_COLS)]
    pending = dots(chunks[0])
    for ci in range(1, len(chunks)):
        accs = dots(chunks[ci])
        epilogue(pending, extra, outs, chunks[ci - 1])
        pending = accs
    epilogue(pending, extra, outs, chunks[-1])


def fused_matmul(pairs, extras, epilogue, out_shapes, out_specs, tm, tn, a_fn=None, name=None, layer=None):
    m = pairs[0][0].shape[0]
    n = pairs[0][1].shape[-1]
    assert m % tm == 0 and n % tn == 0, (m, tm, n, tn)
    in_specs, args = [], []
    for a, w in pairs:
        k = a.shape[1]
        assert w.shape[-2] == k
        if layer is None:
            w_spec = pl.BlockSpec((k, tn), lambda i, j: (0, j))
        else:
            w_spec = pl.BlockSpec((1, k, tn), lambda i, j: (layer, 0, j))
        in_specs += [pl.BlockSpec((tm, k), lambda i, j: (i, 0)), w_spec]
        args += [a, w]
    for arr, spec in extras:
        in_specs.append(spec)
        args.append(arr)
    return pl.pallas_call(
        functools.partial(_fm_kernel, n_pairs=len(pairs), n_extra=len(extras), a_fn=a_fn, epilogue=epilogue),
        grid=(m // tm, n // tn),
        in_specs=in_specs,
        out_specs=out_specs,
        out_shape=out_shapes,
        compiler_params=_params("parallel", "arbitrary"),
        name=name,
    )(*args)


def _tile_spec(tm, tn):
    return pl.BlockSpec((tm, tn), lambda i, j: (i, j))


def _ep_store(dtype):
    def ep(accs, extra, outs, cs):
        outs[0][:, cs] = accs[0].astype(dtype)
    return ep


def _ep_bias(accs, extra, outs, cs):
    outs[0][:, cs] = accs[0] + extra[0][:, cs]


def _ep_bias_store(dtype):
    def ep(accs, extra, outs, cs):
        outs[0][:, cs] = (accs[0] + extra[0][:, cs]).astype(dtype)
    return ep


def _ep_sigmoid(accs, extra, outs, cs):
    outs[0][:, cs] = (0.5 * jnp.tanh(0.5 * accs[0]) + 0.5).astype(outs[0].dtype)


def _ep_qk(accs, extra, outs, cs):
    gain_ref, cos_ref, sin_ref = extra
    acc = accs[0]
    cos = cos_ref[...]
    sin = sin_ref[...]
    for c in range(acc.shape[1] // HEAD_DIM):
        head = cs.start // HEAD_DIM + c
        x = acc[:, c * HEAD_DIM:(c + 1) * HEAD_DIM]
        y = x * lax.rsqrt(jnp.mean(x * x, axis=-1, keepdims=True) + EPS)
        y = y * gain_ref[:, head * HEAD_DIM:(head + 1) * HEAD_DIM]
        outs[0][head] = (y * cos + pltpu.roll(y, HEAD_DIM // 2, 1) * sin).astype(outs[0].dtype)


def _ep_merge(accs, extra, outs, cs):
    y = extra[0][:, cs].astype(f32) * accs[0]
    y = y + extra[1][:, cs].astype(f32) * accs[1]
    y = y + extra[2][:, cs].astype(f32) * accs[2]
    outs[0][:, cs] = y.astype(outs[0].dtype)


def _ep_glu(accs, extra, outs, cs):
    outs[0][:, cs] = (_silu(accs[0]) * accs[1]).astype(outs[0].dtype)


def _make_ep_residual(tm, rows_per_batch, n_ctx, with_routed):
    def ep(accs, extra, outs, cs):
        x_ref, mod_ctx_ref, mod_b_ref = extra[:3]
        y = accs[0]
        if with_routed:
            y = y + extra[3][:, cs]
        is_ctx = _row_is_ctx(pl.program_id(0), tm, rows_per_batch, n_ctx)
        mod = jnp.where(is_ctx, mod_ctx_ref[0, :, cs], mod_b_ref[0, :, cs])
        outs[0][:, cs] = x_ref[:, cs] + mod * y
    return ep


PACK_ROWS = 8


def _pack_words(x):
    bits = lax.bitcast_convert_type(x.astype(bf16).astype(f32), jnp.uint32)
    half = bits.shape[1] // 2
    return [bits[:, half + s * HEAD_DIM:half + (s + 1) * HEAD_DIM] | (bits[:, s * HEAD_DIM:(s + 1) * HEAD_DIM] >> 16)
            for s in range(PACK_ROWS)]


def _unpack_word(word):
    return (lax.bitcast_convert_type(word << 16, f32),
            lax.bitcast_convert_type(word & jnp.uint32(0xFFFF0000), f32))


def _norm_kernel(x_ref, w_ref, shc_ref, shb_ref, scc_ref, scb_ref, o_ref, *packed, tile, rows_per_batch, n_ctx):
    x = x_ref[...]
    y = x * lax.rsqrt(jnp.mean(x * x, axis=-1, keepdims=True) + EPS) * w_ref[...]
    is_ctx = _row_is_ctx(pl.program_id(0), tile, rows_per_batch, n_ctx)
    shift = jnp.where(is_ctx, shc_ref[0], shb_ref[0])
    scale = jnp.where(is_ctx, scc_ref[0], scb_ref[0])
    hb = (y * (1.0 + scale) + shift).astype(bf16)
    o_ref[...] = hb
    if packed:
        for s, word in enumerate(_pack_words(hb)):
            packed[0][pl.ds(s, tile, stride=PACK_ROWS), :] = word


def norm_modulate(x, w, shift, scale, rows_per_batch, n_ctx, packed_rows=False):
    t, d = x.shape
    tile = ROW_TILE
    n_b = shift.shape[0] - 1
    tiles_per_batch = rows_per_batch // tile
    ctx_spec = pl.BlockSpec((1, 1, d), lambda i: (n_b, 0, 0))
    b_spec = pl.BlockSpec((1, 1, d), lambda i: (i // tiles_per_batch, 0, 0))
    row_spec = pl.BlockSpec((tile, d), lambda i: (i, 0))
    out_specs, out_shape = [row_spec], [jax.ShapeDtypeStruct((t, d), bf16)]
    if packed_rows:
        assert d == 2 * PACK_ROWS * HEAD_DIM
        out_specs.append(pl.BlockSpec((tile * PACK_ROWS, HEAD_DIM), lambda i: (i, 0)))
        out_shape.append(jax.ShapeDtypeStruct((t * PACK_ROWS, HEAD_DIM), jnp.uint32))
    return pl.pallas_call(
        functools.partial(_norm_kernel, tile=tile, rows_per_batch=rows_per_batch, n_ctx=n_ctx),
        grid=(t // tile,),
        in_specs=[row_spec, pl.BlockSpec((1, d), lambda i: (0, 0)), ctx_spec, b_spec, ctx_spec, b_spec],
        out_specs=out_specs,
        out_shape=out_shape,
        compiler_params=_params("parallel"),
        name="norm_modulate",
    )(x, w.reshape(1, d), shift, shift, scale, scale)


def _attend(streams, kv_len, tk):
    n = kv_len // tk
    state = [None] * len(streams)
    s_next = [_nt_dot(q, k_at(0)) for q, k_at, _ in streams]
    for c in range(n):
        s_cur = s_next
        if c + 1 < n:
            s_next = [_nt_dot(q, k_at((c + 1) * tk)) for q, k_at, _ in streams]
        for i, (_, _, v_at) in enumerate(streams):
            s = s_cur[i]
            s_max = jnp.max(s, axis=1, keepdims=True)
            if c == 0:
                m = s_max
                p = jnp.exp2(s - m)
                l = jnp.sum(p, axis=1, keepdims=True)
                acc = jnp.dot(p.astype(bf16), v_at(c * tk), preferred_element_type=f32)
            else:
                m_old, l, acc = state[i]
                m = jnp.maximum(m_old, s_max)
                alpha = jnp.exp2(m_old - m)
                p = jnp.exp2(s - m)
                l = alpha * l + jnp.sum(p, axis=1, keepdims=True)
                acc = alpha * acc + jnp.dot(p.astype(bf16), v_at(c * tk), preferred_element_type=f32)
            state[i] = (m, l, acc)
    return [acc / l for _, l, acc in state]


def _attend_sum_in_v(q, k_at, v_at, kv_len, tk, d):
    n = kv_len // tk
    m = acc = None
    s_next = _nt_dot(q, k_at(0))
    for c in range(n):
        s = s_next
        if c + 1 < n:
            s_next = _nt_dot(q, k_at((c + 1) * tk))
        s_max = jnp.max(s, axis=1, keepdims=True)
        if c == 0:
            m = s_max
            acc = jnp.dot(jnp.exp2((s - m).astype(bf16)), v_at(0), preferred_element_type=f32)
        else:
            m_new = jnp.maximum(m, s_max)
            alpha = jnp.exp2(m - m_new)
            acc = alpha * acc + jnp.dot(jnp.exp2((s - m_new).astype(bf16)), v_at(c * tk),
                                        preferred_element_type=f32)
            m = m_new
    return acc[:, :d] / acc[:, d:]


def _kv_chunk(kv_len):
    for tk in (768, 1408, 512, 256, 128):
        if kv_len % tk == 0:
            return tk
    raise ValueError(kv_len)


def _ctx_or_all(n_ctx_tiles, n_ctx, kv_all, run):
    is_ctx = pl.program_id(2) < n_ctx_tiles

    @pl.when(is_ctx)
    def _():
        run(n_ctx)

    @pl.when(jnp.logical_not(is_ctx))
    def _():
        run(kv_all)


def _attn_a_kernel(q_ref, k_ref, v_ref, o_ref, *, n_ctx):
    g, tq, d = q_ref.shape

    def run(kv_len):
        tk = _kv_chunk(kv_len)

        def k_at(off):
            return k_ref[0, off:off + tk, :]

        def v_at(off):
            return v_ref[off:off + tk, :]

        o = _attend_sum_in_v(q_ref[...].reshape(g * tq, d), k_at, v_at, kv_len, tk, d)
        for h in range(g):
            o_ref[:, h * d:(h + 1) * d] = o[h * tq:(h + 1) * tq].astype(o_ref.dtype)

    _ctx_or_all(n_ctx // tq, n_ctx, k_ref.shape[1], run)


def _attn_c_kernel(lam_ref, dn_ref, q_ref, k_ref, v_ref, o_ref, *, n_ctx, lam_init):
    tq = q_ref.shape[1]

    def run(kv_len):
        tk = _kv_chunk(kv_len)
        def v_at(off):
            return v_ref[off:off + tk, :]

        (o1,) = _attend([(q_ref[0], lambda off: k_ref[0, off:off + tk, :], v_at)], kv_len, tk)
        (o2,) = _attend([(q_ref[1], lambda off: k_ref[1, off:off + tk, :], v_at)], kv_len, tk)
        lam_p = lam_ref[...]
        lam = (jnp.exp(jnp.sum(lam_p[0:1] * lam_p[1:2], axis=-1, keepdims=True))
               - jnp.exp(jnp.sum(lam_p[2:3] * lam_p[3:4], axis=-1, keepdims=True)) + lam_init)
        o = o1 - lam * o2
        y = o * lax.rsqrt(jnp.mean(o * o, axis=-1, keepdims=True) + EPS) * dn_ref[...]
        o_ref[...] = (y * (1.0 - lam_init)).astype(o_ref.dtype)

    _ctx_or_all(n_ctx // tq, n_ctx, k_ref.shape[1], run)


def attention_a(qk, va, *, n_batch, rows_per_batch, n_ctx, a_heads, a_kv_heads):
    t = qk.shape[1]
    group = a_heads // a_kv_heads
    tq = ROW_TILE
    n_q = rows_per_batch // tq
    return pl.pallas_call(
        functools.partial(_attn_a_kernel, n_ctx=n_ctx),
        grid=(n_batch, a_kv_heads, n_q),
        in_specs=[
            pl.BlockSpec((group, tq, HEAD_DIM), lambda b, h, i: (h, b * n_q + i, 0)),
            pl.BlockSpec((1, rows_per_batch, HEAD_DIM), lambda b, h, i: (a_heads + h, b, 0)),
            pl.BlockSpec((rows_per_batch, 2 * HEAD_DIM), lambda b, h, i: (b, h)),
        ],
        out_specs=pl.BlockSpec((tq, group * HEAD_DIM), lambda b, h, i: (b * n_q + i, h)),
        out_shape=jax.ShapeDtypeStruct((t, a_heads * HEAD_DIM), bf16),
        compiler_params=_params("parallel", "parallel", "arbitrary"),
        name="attention_a",
    )(qk, qk, va)


def attention_c(qk, vc, lam_params, diff_norm, *, n_batch, rows_per_batch, n_ctx, head0_q, head0_k,
                c_heads, lam_init):
    t = qk.shape[1]
    dv = 2 * HEAD_DIM
    tq = ROW_TILE
    n_q = rows_per_batch // tq
    return pl.pallas_call(
        functools.partial(_attn_c_kernel, n_ctx=n_ctx, lam_init=lam_init),
        grid=(n_batch, c_heads, n_q),
        in_specs=[
            pl.BlockSpec((4, HEAD_DIM), lambda b, h, i: (0, 0)),
            pl.BlockSpec((1, dv), lambda b, h, i: (0, 0)),
            pl.BlockSpec((2, tq, HEAD_DIM), lambda b, h, i: (head0_q // 2 + h, b * n_q + i, 0)),
            pl.BlockSpec((2, rows_per_batch, HEAD_DIM), lambda b, h, i: (head0_k // 2 + h, b, 0)),
            pl.BlockSpec((rows_per_batch, dv), lambda b, h, i: (b, h)),
        ],
        out_specs=pl.BlockSpec((tq, dv), lambda b, h, i: (b * n_q + i, h)),
        out_shape=jax.ShapeDtypeStruct((t, c_heads * dv), bf16),
        compiler_params=_params("parallel", "parallel", "arbitrary"),
        name="attention_c",
    )(lam_params, diff_norm.reshape(1, dv), qk, qk, vc)


def _hgrn_tables(rev):
    c = SCAN_CHUNK
    half = SCAN_SUB // 2
    t_i = lax.broadcasted_iota(i32, (c, c), 0)
    u_i = lax.broadcasted_iota(i32, (c, c), 1)
    blk8 = t_i & ~(SCAN_SUB - 1)
    blk4 = t_i & ~(half - 1)
    if rev:
        rows_upto = [t_i, blk4 + half - 1, blk8 + half, blk8]
        sums = jnp.concatenate([(u_i >= r).astype(f32) for r in rows_upto], axis=0)
    else:
        rows_upto = [t_i, blk4, blk8 + half - 1, blk8 + SCAN_SUB - 1]
        sums = jnp.concatenate([(u_i <= r).astype(f32) for r in rows_upto], axis=0)
    causal = (u_i >= t_i) if rev else (u_i <= t_i)
    same8 = (u_i & ~(SCAN_SUB - 1)) == blk8
    same4 = (u_i & ~(half - 1)) == blk4
    return sums, same8 & jnp.logical_not(same4) & causal, same4 & causal


def _hgrn_gates(qp, z, lb):
    q = _silu(qp.astype(f32))
    one_m = 1.0 - lb
    g = jnp.log(jnp.maximum(lb + one_m * jax.nn.sigmoid(z), F_MIN))
    k = one_m * jax.nn.sigmoid(-z)
    return q, k, g


def _hgrn_local(q, k, b_all, v, tables, rev):
    c = SCAN_CHUNK
    nb = c // SCAN_SUB
    _, across, inner = tables
    b, rho4, rho_mid, rho_edge = (b_all[i * c:(i + 1) * c] for i in range(4))
    end = 0 if rev else c - 1
    b_end = b[end:end + 1, :]
    q_in = (q * jnp.exp(b)).astype(bf16)
    d_st = _tn_dot(v, (k * jnp.exp(b_end - b)).astype(bf16))

    def edge_row(j):
        return j * SCAN_SUB if rev else j * SCAN_SUB + SCAN_SUB - 1

    row = lax.broadcasted_iota(i32, (c, HEAD_DIM), 0)
    k_til = k * jnp.exp(rho_edge - b)
    qs, ks = [], []
    for j in (range(1, nb) if rev else range(nb - 1)):
        rho = b[edge_row(j):edge_row(j) + 1, :]
        q_rows = (row < j * SCAN_SUB) if rev else (row >= (j + 1) * SCAN_SUB)
        qs.append((q * jnp.exp(jnp.where(q_rows, b - rho, -jnp.inf))).astype(bf16))
        k_rows = (row >= j * SCAN_SUB) & (row < (j + 1) * SCAN_SUB)
        ks.append(jnp.where(k_rows, k_til, 0.0).astype(bf16))
    att = _nt_dot(jnp.concatenate(qs, axis=1), jnp.concatenate(ks, axis=1))

    att_mid = _nt_dot((q * jnp.exp(jnp.minimum(b - rho_mid, 0.0))).astype(bf16),
                      (k * jnp.exp(jnp.minimum(rho_mid - b, 0.0))).astype(bf16))
    att_in = _nt_dot((q * jnp.exp(b - rho4)).astype(bf16), (k * jnp.exp(rho4 - b)).astype(bf16))
    att = att + jnp.where(across, att_mid, 0.0) + jnp.where(inner, att_in, 0.0)
    return jnp.dot(att.astype(bf16), v, preferred_element_type=f32), q_in, d_st, jnp.exp(b_end)


def _hgrn_kernel(lb_ref, qf_ref, vf_ref, zf_ref, qb_ref, vb_ref, zb_ref, of_ref, ob_ref, sf_ref, sb_ref):
    @pl.when(pl.program_id(2) == 0)
    def _():
        sf_ref[...] = jnp.zeros_like(sf_ref)
        sb_ref[...] = jnp.zeros_like(sb_ref)

    lb = lb_ref[...]
    n_chunks = qf_ref.shape[0] // SCAN_CHUNK
    work = []
    for c in range(n_chunks):
        work.append((slice(c * SCAN_CHUNK, (c + 1) * SCAN_CHUNK), False))
        work.append((slice((n_chunks - 1 - c) * SCAN_CHUNK, (n_chunks - c) * SCAN_CHUNK), True))
    refs = {False: (qf_ref, vf_ref, zf_ref, of_ref), True: (qb_ref, vb_ref, zb_ref, ob_ref)}
    tables = {False: _hgrn_tables(False), True: _hgrn_tables(True)}

    gates = [_hgrn_gates(refs[rev][0][rows, :], refs[rev][2][rows, :], lb) for rows, rev in work]
    sums = [jnp.dot(tables[rev][0], g, precision=lax.Precision.HIGHEST, preferred_element_type=f32)
            for (_, rev), (_, _, g) in zip(work, gates)]
    local = [_hgrn_local(q, k, b_all, refs[rev][1][rows, :], tables[rev], rev)
             for (rows, rev), (q, k, _), b_all in zip(work, gates, sums)]
    state = {False: sf_ref[...], True: sb_ref[...]}
    for (rows, rev), (o_in, q_in, d_st, decay) in zip(work, local):
        st = state[rev]
        refs[rev][3][rows, :] = o_in + _nt_dot(q_in, st.astype(bf16))
        state[rev] = st * decay + d_st
    sf_ref[...] = state[False]
    sb_ref[...] = state[True]


def hgrn_scan(qio, zfb, lower_bound, *, n_batch, rows_per_batch, b_heads):
    t = qio.shape[0]
    tile = SCAN_TILE
    n_tiles = rows_per_batch // tile

    def fwd(b, h, j):
        return b * n_tiles + j

    def bwd(b, h, j):
        return b * n_tiles + jnp.where(j == 0, 0, n_tiles - j)

    def spec(tile_fn, col0):
        return pl.BlockSpec((tile, HEAD_DIM), lambda b, h, j: (tile_fn(b, h, j), col0 + h))

    return pl.pallas_call(
        _hgrn_kernel,
        grid=(n_batch, b_heads, n_tiles),
        in_specs=[
            pl.BlockSpec((1, HEAD_DIM), lambda b, h, j: (0, h)),
            spec(fwd, 0), spec(fwd, b_heads), spec(fwd, 0),
            spec(bwd, 0), spec(bwd, b_heads), spec(bwd, b_heads),
        ],
        out_specs=[spec(fwd, 0), spec(bwd, 0)],
        out_shape=[jax.ShapeDtypeStruct((t, b_heads * HEAD_DIM), f32)] * 2,
        scratch_shapes=[pltpu.VMEM((HEAD_DIM, HEAD_DIM), f32)] * 2,
        compiler_params=_params("parallel", "parallel", "arbitrary"),
        name="hgrn_scan",
    )(lower_bound.reshape(1, -1), qio, qio, zfb, qio, qio, zfb)


def _hgrn_out_kernel(of_ref, ob_ref, og_ref, w_ref, o_ref):
    o = of_ref[...] + ob_ref[...]
    w = w_ref[...]
    for c in range(o.shape[1] // HEAD_DIM):
        sl = slice(c * HEAD_DIM, (c + 1) * HEAD_DIM)
        x = o[:, sl]
        y = x * lax.rsqrt(jnp.mean(x * x, axis=-1, keepdims=True) + EPS) * w
        o_ref[:, sl] = (y * _silu(og_ref[:, sl].astype(f32))).astype(o_ref.dtype)


def hgrn_readout(o_f, o_b, qio, hgrn_norm, b_heads):
    t, w = o_f.shape
    tile = ROW_TILE
    row_spec = pl.BlockSpec((tile, w), lambda i: (i, 0))
    return pl.pallas_call(
        _hgrn_out_kernel,
        grid=(t // tile,),
        in_specs=[row_spec, row_spec, pl.BlockSpec((tile, w), lambda i: (i, 2)),
                  pl.BlockSpec((1, HEAD_DIM), lambda i: (0, 0))],
        out_specs=row_spec,
        out_shape=jax.ShapeDtypeStruct((t, w), bf16),
        compiler_params=_params("parallel"),
        name="hgrn_readout",
    )(o_f, o_b, qio, hgrn_norm.reshape(1, HEAD_DIM))


def _router_kernel(h_ref, w_ref, b_ref, e_ref, wt_ref, rank_ref, cnt_ref, run_ref):
    n_e = w_ref.shape[0]
    tm = h_ref.shape[0]
    per = n_e // N_GROUPS

    @pl.when(pl.program_id(0) == 0)
    def _():
        run_ref[...] = jnp.zeros_like(run_ref)

    scores = jax.nn.sigmoid(_nt_dot(w_ref[...], h_ref[...]))
    biased = scores + b_ref[...]
    neg = -jnp.inf

    grp = biased.reshape(N_GROUPS, per, tm)
    idx_in = lax.broadcasted_iota(i32, grp.shape, 1)
    m1 = jnp.max(grp, axis=1, keepdims=True)
    first = jnp.min(jnp.where(grp == m1, idx_in, per), axis=1, keepdims=True)
    m2 = jnp.max(jnp.where(idx_in == first, neg, grp), axis=1, keepdims=True)
    cur = m1 + m2

    g_idx = lax.broadcasted_iota(i32, cur.shape, 0)
    g_sel = jnp.zeros(cur.shape, f32)
    for _ in range(TOPK_GROUPS):
        mx = jnp.max(cur, axis=0, keepdims=True)
        pick = g_idx == jnp.min(jnp.where(cur == mx, g_idx, N_GROUPS), axis=0, keepdims=True)
        g_sel = jnp.where(pick, 1.0, g_sel)
        cur = jnp.where(pick, neg, cur)
    e_mask = jnp.broadcast_to(g_sel, (N_GROUPS, per, tm)).reshape(n_e, tm) > 0.0

    e_idx = lax.broadcasted_iota(i32, (n_e, tm), 0)
    cur = jnp.where(e_mask, biased, MASK_SCORE)
    ws, picks = [], []
    member = jnp.zeros((n_e, tm), f32)
    for r in range(TOP_K):
        mx = jnp.max(cur, axis=0, keepdims=True)
        first = jnp.min(jnp.where(cur == mx, e_idx, n_e), axis=0, keepdims=True)
        pick = e_idx == first
        e_ref[r:r + 1, :] = first
        ws.append(jnp.sum(jnp.where(pick, scores, 0.0), axis=0, keepdims=True))
        cur = jnp.where(pick, neg, cur)
        member = jnp.where(pick, 1.0, member)
        picks.append(pick)
    total = ws[0]
    for r in range(1, TOP_K):
        total = total + ws[r]
    for r in range(TOP_K):
        wt_ref[r:r + 1, :] = ws[r] / total * ROUTE_SCALE

    u_i = lax.broadcasted_iota(i32, (tm, tm), 0)
    t_i = lax.broadcasted_iota(i32, (tm, tm), 1)
    prefix = jnp.dot(member.astype(bf16), (u_i <= t_i).astype(bf16), preferred_element_type=f32)
    rank = run_ref[...] + prefix
    for r in range(TOP_K):
        rank_ref[r:r + 1, :] = jnp.sum(jnp.where(picks[r], rank, 0.0), axis=0, keepdims=True).astype(i32)
    run_ref[...] = run_ref[...] + jnp.sum(member, axis=1, keepdims=True)
    cnt_ref[...] = run_ref[...].astype(i32)


def moe_route(h, w_router_t, b_router):
    t, d = h.shape
    n_e = w_router_t.shape[0]
    tm = 512
    assert t % tm == 0
    out_spec = pl.BlockSpec((TOP_K, tm), lambda i: (0, i))
    return pl.pallas_call(
        _router_kernel,
        grid=(t // tm,),
        in_specs=[pl.BlockSpec((tm, d), lambda i: (i, 0)), pl.BlockSpec((n_e, d), lambda i: (0, 0)),
                  pl.BlockSpec((n_e, 1), lambda i: (0, 0))],
        out_specs=[out_spec, out_spec, out_spec, pl.BlockSpec((n_e, 1), lambda i: (0, 0))],
        out_shape=[jax.ShapeDtypeStruct((TOP_K, t), i32), jax.ShapeDtypeStruct((TOP_K, t), f32),
                   jax.ShapeDtypeStruct((TOP_K, t), i32), jax.ShapeDtypeStruct((n_e, 1), i32)],
        scratch_shapes=[pltpu.VMEM((n_e, 1), f32)],
        compiler_params=_params("arbitrary"),
        name="moe_route",
    )(h, w_router_t, b_router.reshape(n_e, 1).astype(f32))


GATHER_DEPTH = 3


def _expert_kernel(be_ref, idx0_ref, idx1_ref, idx_ahead_ref, h_ref, wg_ref, wu_ref, wd_ref, y_ref,
                   xbuf, sem, x_s, wg_b, wu_b, wd_b):
    i = pl.program_id(0)
    last = pl.num_programs(0) - 1
    tm = xbuf.shape[1] // PACK_ROWS
    slot = i % GATHER_DEPTH
    ahead = (i + GATHER_DEPTH - 1) % GATHER_DEPTH

    def gather_start(idx_ref, dst_slot):
        for r in range(tm):
            src_row = pl.multiple_of(idx_ref[0, 0, r] * PACK_ROWS, PACK_ROWS)
            pltpu.make_async_copy(h_ref.at[pl.ds(src_row, PACK_ROWS)],
                                  xbuf.at[dst_slot, pl.ds(r * PACK_ROWS, PACK_ROWS)],
                                  sem.at[dst_slot]).start(priority=r % 2)

    def gather_wait(dst_slot):
        pltpu.make_async_copy(h_ref.at[pl.ds(0, tm * PACK_ROWS)], xbuf.at[dst_slot], sem.at[dst_slot]).wait()

    @pl.when(i == 0)
    def _():
        gather_start(idx0_ref, 0)
        gather_start(idx1_ref, 1)

    @pl.when((i == 0) | (be_ref[i] != be_ref[jnp.maximum(i - 1, 0)]))
    def _():
        wg_b[...] = wg_ref[0, 0].astype(bf16)
        wu_b[...] = wu_ref[0, 0].astype(bf16)
        wd_b[...] = wd_ref[0, 0].astype(bf16)

    gather_wait(slot)
    half = PACK_ROWS * HEAD_DIM
    for s in range(PACK_ROWS):
        lo, hi = _unpack_word(xbuf[slot, pl.ds(s, tm, stride=PACK_ROWS), :])
        x_s[:, s * HEAD_DIM:(s + 1) * HEAD_DIM] = lo.astype(bf16)
        x_s[:, half + s * HEAD_DIM:half + (s + 1) * HEAD_DIM] = hi.astype(bf16)
    gather_start(idx_ahead_ref, ahead)
    x = x_s[...]
    g = jnp.dot(x, wg_b[...], preferred_element_type=f32)
    u = jnp.dot(x, wu_b[...], preferred_element_type=f32)
    a = (_silu(g) * u).astype(bf16)
    y = jnp.dot(a, wd_b[...], preferred_element_type=f32)
    for s, word in enumerate(_pack_words(y)):
        y_ref[pl.ds(s, tm, stride=PACK_ROWS), :] = word

    @pl.when(i == last)
    def _():
        for k in range(1, GATHER_DEPTH):
            gather_wait((i + k) % GATHER_DEPTH)


def moe_experts(h_packed, slot_tok, w_gate, w_up, w_down, layer, block_e):
    tm = slot_tok.shape[0] // block_e.shape[0]
    n_blocks = block_e.shape[0]
    assert n_blocks >= GATHER_DEPTH
    d, d_e = w_gate.shape[-2:]
    idx = slot_tok.reshape(n_blocks, 1, tm)

    def idx_spec(block_fn):
        return pl.BlockSpec((1, 1, tm), lambda i, be: (block_fn(i), 0, 0), memory_space=pltpu.SMEM)

    def w_spec(shape):
        return pl.BlockSpec((1, 1) + shape, lambda i, be: (layer, be[i], 0, 0))

    return pl.pallas_call(
        _expert_kernel,
        grid_spec=pltpu.PrefetchScalarGridSpec(
            num_scalar_prefetch=1,
            grid=(n_blocks,),
            in_specs=[idx_spec(lambda i: 0), idx_spec(lambda i: 1),
                      idx_spec(lambda i: jnp.minimum(i + GATHER_DEPTH - 1, n_blocks - 1)),
                      pl.BlockSpec(memory_space=pl.ANY),
                      w_spec((d, d_e)), w_spec((d, d_e)), w_spec((d_e, d))],
            out_specs=pl.BlockSpec((tm * PACK_ROWS, HEAD_DIM), lambda i, be: (i, 0)),
            scratch_shapes=[pltpu.VMEM((GATHER_DEPTH, tm * PACK_ROWS, HEAD_DIM), jnp.uint32),
                            pltpu.SemaphoreType.DMA((GATHER_DEPTH,)), pltpu.VMEM((tm, d), bf16),
                            pltpu.VMEM((d, d_e), bf16), pltpu.VMEM((d, d_e), bf16), pltpu.VMEM((d_e, d), bf16)],
        ),
        out_shape=jax.ShapeDtypeStruct((n_blocks * tm * PACK_ROWS, HEAD_DIM), jnp.uint32),
        compiler_params=_params("arbitrary"),
        name="moe_experts",
    )(block_e, idx, idx, idx, h_packed, w_gate, w_up, w_down)


def _combine_kernel(idx_ref, idx_next_ref, w_ref, y_ref, o_ref, buf, sem):
    i = pl.program_id(0)
    last = pl.num_programs(0) - 1
    n_k = buf.shape[1]
    rows = buf.shape[2] // PACK_ROWS
    slot = i % 2

    def gather_start(src_idx_ref, dst_slot):
        for r in range(rows):
            for kk in range(n_k):
                src_row = pl.multiple_of(src_idx_ref[0, kk, r] * PACK_ROWS, PACK_ROWS)
                pltpu.make_async_copy(y_ref.at[pl.ds(src_row, PACK_ROWS)],
                                      buf.at[dst_slot, kk, pl.ds(r * PACK_ROWS, PACK_ROWS)],
                                      sem.at[dst_slot]).start(priority=(r * n_k + kk) % 2)

    def gather_wait(dst_slot):
        for kk in range(n_k):
            pltpu.make_async_copy(y_ref.at[pl.ds(0, rows * PACK_ROWS)], buf.at[dst_slot, kk],
                                  sem.at[dst_slot]).wait()

    @pl.when(i == 0)
    def _():
        gather_start(idx_ref, 0)

    gather_start(idx_next_ref, 1 - slot)
    gather_wait(slot)
    w = w_ref[...]
    half = PACK_ROWS * HEAD_DIM
    for s in range(PACK_ROWS):
        acc_lo = acc_hi = None
        for kk in range(n_k):
            lo, hi = _unpack_word(buf[slot, kk, pl.ds(s, rows, stride=PACK_ROWS), :])
            w_k = w[:, kk:kk + 1]
            acc_lo = lo * w_k if kk == 0 else acc_lo + lo * w_k
            acc_hi = hi * w_k if kk == 0 else acc_hi + hi * w_k
        o_ref[:, s * HEAD_DIM:(s + 1) * HEAD_DIM] = acc_lo
        o_ref[:, half + s * HEAD_DIM:half + (s + 1) * HEAD_DIM] = acc_hi

    @pl.when(i == last)
    def _():
        gather_wait(1 - slot)


def moe_combine(y, dest, top_w):
    n_k, t = dest.shape
    d = 2 * PACK_ROWS * HEAD_DIM
    rows = 128
    n_tiles = t // rows
    idx = dest.reshape(n_k, n_tiles, rows).transpose(1, 0, 2)

    def idx_spec(tile_fn):
        return pl.BlockSpec((1, n_k, rows), lambda i: (tile_fn(i), 0, 0), memory_space=pltpu.SMEM)

    return pl.pallas_call(
        _combine_kernel,
        grid=(n_tiles,),
        in_specs=[idx_spec(lambda i: i), idx_spec(lambda i: jnp.minimum(i + 1, n_tiles - 1)),
                  pl.BlockSpec((rows, n_k), lambda i: (i, 0)),
                  pl.BlockSpec(memory_space=pl.ANY)],
        out_specs=pl.BlockSpec((rows, d), lambda i: (i, 0)),
        out_shape=jax.ShapeDtypeStruct((t, d), f32),
        scratch_shapes=[pltpu.VMEM((2, n_k, rows * PACK_ROWS, HEAD_DIM), jnp.uint32),
                        pltpu.SemaphoreType.DMA((2,))],
        compiler_params=_params("arbitrary"),
        name="moe_combine",
    )(idx, idx, top_w.T, y)


def _moe_plan(top_e, rank, counts, tm):
    n_k, t = top_e.shape
    n_experts = counts.shape[0]
    n_blocks = -(-(n_k * t) // tm) + n_experts
    padded = (counts + tm - 1) // tm * tm
    pad_end = jnp.cumsum(padded)
    pad_start = pad_end - padded
    onehot = top_e[:, :, None] == jnp.arange(n_experts, dtype=i32)
    dest = jnp.sum(jnp.where(onehot, pad_start, 0), axis=-1) + rank - 1
    n_used = pad_end[-1] // tm
    blk = jnp.minimum(jnp.arange(n_blocks, dtype=i32), n_used - 1)
    block_e = jnp.sum((pad_end[None, :] <= (blk * tm)[:, None]).astype(i32), axis=1)
    block_e = jnp.minimum(block_e, n_experts - 1)
    tok = jnp.broadcast_to(jnp.arange(t, dtype=i32)[None, :], (n_k, t))
    sorted_tok = jnp.sort((top_e * t + tok).reshape(-1)) % t
    of_block = block_e[:, None] == jnp.arange(n_experts, dtype=i32)
    grp_start = jnp.cumsum(counts) - counts

    def per_block(table):
        return jnp.sum(jnp.where(of_block, table, 0), axis=1, keepdims=True)

    j = jnp.arange(n_blocks * tm, dtype=i32).reshape(n_blocks, tm) - per_block(pad_start)
    valid = (j >= 0) & (j < per_block(counts))
    src = jnp.clip(per_block(grp_start) + j, 0, n_k * t - 1)
    slot_tok = jnp.where(valid, jnp.take(sorted_tok, src.reshape(-1)).reshape(n_blocks, tm), 0)
    return dest.astype(i32), slot_tok.reshape(-1), block_e


def _rope_perm():
    quarter = HEAD_DIM // 4
    return jnp.concatenate([jnp.arange(q * quarter, (q + 1) * quarter, dtype=i32) for q in (0, 2, 1, 3)])


def _rope_tables(n_ctx, n_lat):
    t = jnp.arange(n_lat)
    pos = jnp.stack([t // GRID_W, t % GRID_W], axis=-1).astype(f32)
    n_freq = HEAD_DIM // 4
    inv_freq = ROPE_THETA ** (-jnp.arange(n_freq, dtype=f32) / n_freq)
    ang = pos[:, :, None] * inv_freq
    cos, sin = jnp.cos(ang), jnp.sin(ang)
    cos_t = jnp.concatenate([cos[:, 0], cos[:, 1], cos[:, 0], cos[:, 1]], axis=-1)
    sin_t = jnp.concatenate([-sin[:, 0], -sin[:, 1], sin[:, 0], sin[:, 1]], axis=-1)
    cos_t = jnp.concatenate([jnp.ones((n_ctx, HEAD_DIM), f32), cos_t], axis=0)
    sin_t = jnp.concatenate([jnp.zeros((n_ctx, HEAD_DIM), f32), sin_t], axis=0)
    return cos_t, sin_t


def kernel(x, c, ctx, c_ctx, w_ada, b_ada, norm_mix, norm_ffn, w_in, q_norm_a, k_norm_a, hgrn_lb_logits, hgrn_out_norm, q_norm_c, k_norm_c, lambda_q1, lambda_k1, lambda_q2, lambda_k2, diff_sub_norm, w_branch, w_out, w_router, b_router, w_exp_gate, w_exp_up, w_exp_down, w_sh_gate, w_sh_up, w_sh_down):
    n_batch, n_lat, d = x.shape
    n_ctx = ctx.shape[1]
    depth = w_ada.shape[0]
    rows_b = n_ctx + n_lat
    t = n_batch * rows_b
    branch_w = d // 2
    a_heads = branch_w // HEAD_DIM
    a_kv = a_heads // 4
    b_heads = branch_w // HEAD_DIM
    c_heads = branch_w // (2 * HEAD_DIM)
    n_experts = w_router.shape[-1]
    tm = MM_TILE_M if rows_b % MM_TILE_M == 0 else ROW_TILE
    tiles_per_batch = rows_b // tm
    assert n_ctx % ROW_TILE == 0 and n_lat % ROW_TILE == 0 and rows_b % SCAN_TILE == 0

    cos_t, sin_t = _rope_tables(n_ctx, n_lat)
    p_lb = jax.nn.softmax(hgrn_lb_logits.astype(f32), axis=0)
    lower_bounds = jnp.cumsum(p_lb, axis=0) - p_lb[0]
    cond = jnp.zeros((8, d), f32).at[:n_batch].set(c).at[n_batch].set(c_ctx)

    xa = jnp.concatenate([ctx, x], axis=1).reshape(t, d)

    def mod_specs(tn):
        ctx_spec = pl.BlockSpec((1, 1, tn), lambda i, j: (n_batch, 0, j))
        b_spec = pl.BlockSpec((1, 1, tn), lambda i, j: (i // tiles_per_batch, 0, j))
        return ctx_spec, b_spec

    splits = (a_heads * HEAD_DIM, a_kv * HEAD_DIM, a_kv * HEAD_DIM,
              branch_w, branch_w, branch_w, branch_w, branch_w,
              branch_w, branch_w, branch_w, 3 * d)
    offs = [0]
    for s in splits:
        offs.append(offs[-1] + s)

    def cols(w, *ids):
        return jnp.concatenate([w[:, offs[i]:offs[i + 1]] for i in ids], axis=1).astype(bf16)

    n_qk_heads = a_heads + a_kv + 4 * c_heads
    qk_tn = (n_qk_heads // 2) * HEAD_DIM if n_qk_heads % 2 == 0 else n_qk_heads * HEAD_DIM
    sm_scale = HEAD_DIM ** -0.5 * LOG2E

    for l in range(depth):
        lam_init = 0.8 - 0.6 * math.exp(-0.3 * l)
        mod = fused_matmul([(cond, w_ada)], [(b_ada[l].reshape(1, -1), pl.BlockSpec((1, 1024), lambda i, j: (0, j)))],
                           _ep_bias, jax.ShapeDtypeStruct((8, 6 * d), f32), _tile_spec(8, 1024), 8, 1024,
                           a_fn=_silu, name="ada_ln", layer=l)
        mod = mod[:n_batch + 1].reshape(n_batch + 1, 6, 1, d)
        mods = [mod[:, k] for k in range(6)]

        (h,) = norm_modulate(xa, norm_mix[l], mods[0], mods[1], rows_b, n_ctx)
        w_l = w_in[l]
        perm = _rope_perm()
        gains = jnp.concatenate([jnp.tile(q_norm_a[l][perm] * sm_scale, a_heads),
                                 jnp.tile(k_norm_a[l][perm], a_kv),
                                 jnp.tile(q_norm_c[l][perm] * sm_scale, 2 * c_heads),
                                 jnp.tile(k_norm_c[l][perm], 2 * c_heads)])
        w_qk = cols(w_l, 0, 1, 8, 9)
        w_qk = w_qk.reshape(d, n_qk_heads, HEAD_DIM)[:, :, perm].reshape(d, n_qk_heads * HEAD_DIM)
        qk = fused_matmul(
            [(h, w_qk)],
            [(gains.reshape(1, -1).astype(f32), pl.BlockSpec((1, qk_tn), lambda i, j: (0, j))),
             (cos_t, pl.BlockSpec((tm, HEAD_DIM), lambda i, j: (i % tiles_per_batch, 0))),
             (sin_t, pl.BlockSpec((tm, HEAD_DIM), lambda i, j: (i % tiles_per_batch, 0)))],
            _ep_qk, jax.ShapeDtypeStruct((n_qk_heads, t, HEAD_DIM), bf16),
            pl.BlockSpec((qk_tn // HEAD_DIM, tm, HEAD_DIM), lambda i, j: (j, i, 0)), tm, qk_tn, name="in_proj_qk")
        w_va = jnp.pad(cols(w_l, 2).reshape(d, a_kv, HEAD_DIM), ((0, 0), (0, 0), (0, HEAD_DIM)))
        ones_cols = jnp.tile(jnp.concatenate([jnp.zeros((HEAD_DIM,), f32), jnp.ones((HEAD_DIM,), f32)]), a_kv)
        va_n = 2 * a_kv * HEAD_DIM
        va = fused_matmul([(h, w_va.reshape(d, va_n))],
                          [(ones_cols.reshape(1, va_n), pl.BlockSpec((1, va_n), lambda i, j: (0, 0)))],
                          _ep_bias_store(bf16), jax.ShapeDtypeStruct((t, va_n), bf16),
                          _tile_spec(tm, va_n), tm, va_n, name="in_proj_va")
        vc = fused_matmul([(h, cols(w_l, 10))], [], _ep_store(bf16), jax.ShapeDtypeStruct((t, branch_w), bf16),
                          _tile_spec(tm, 1024), tm, 1024, name="in_proj_vc")
        qio = fused_matmul([(h, cols(w_l, 3, 4, 7))], [], _ep_store(bf16), jax.ShapeDtypeStruct((t, 3 * branch_w), bf16),
                           _tile_spec(tm, 1024), tm, 1024, name="in_proj_hgrn")
        zfb = fused_matmul([(h, cols(w_l, 5, 6))], [], _ep_store(f32), jax.ShapeDtypeStruct((t, 2 * branch_w), f32),
                           _tile_spec(tm, 1024), tm, 1024, name="in_proj_forget")
        gates = fused_matmul([(h, cols(w_l, 11))], [], _ep_sigmoid, jax.ShapeDtypeStruct((t, 3 * d), bf16),
                             _tile_spec(tm, 1024), tm, 1024, name="in_proj_gates")

        dims = dict(n_batch=n_batch, rows_per_batch=rows_b, n_ctx=n_ctx)
        o_a = attention_a(qk, va, a_heads=a_heads, a_kv_heads=a_kv, **dims)
        lam_params = jnp.stack([lambda_q1[l], lambda_k1[l], lambda_q2[l], lambda_k2[l]]).astype(f32)
        o_c = attention_c(qk, vc, lam_params, diff_sub_norm[l], head0_q=a_heads + a_kv,
                          head0_k=a_heads + a_kv + 2 * c_heads, c_heads=c_heads, lam_init=lam_init, **dims)
        o_f, o_r = hgrn_scan(qio, zfb, lower_bounds[l], n_batch=n_batch, rows_per_batch=rows_b, b_heads=b_heads)
        o_b = hgrn_readout(o_f, o_r, qio, hgrn_out_norm[l], b_heads)

        wb = w_branch[l].astype(bf16)
        n_col = d // 512
        y = fused_matmul(
            [(o_a, wb[0]), (o_b, wb[1]), (o_c, wb[2])],
            [(gates, pl.BlockSpec((tm, 512), lambda i, j, r=r: (i, r * n_col + j))) for r in range(3)],
            _ep_merge, jax.ShapeDtypeStruct((t, d), bf16), _tile_spec(tm, 512), tm, 512, name="merge")
        ctx_spec, b_spec = mod_specs(512)
        xa = fused_matmul(
            [(y, w_out[l].astype(bf16))],
            [(xa, _tile_spec(tm, 512)), (mods[2], ctx_spec), (mods[2], b_spec)],
            _make_ep_residual(tm, rows_b, n_ctx, False), jax.ShapeDtypeStruct((t, d), f32),
            _tile_spec(tm, 512), tm, 512, name="out_proj")

        h2, h2_packed = norm_modulate(xa, norm_ffn[l], mods[3], mods[4], rows_b, n_ctx, packed_rows=True)
        top_e, top_w, rank, counts = moe_route(h2, w_router[l].T.astype(bf16), b_router[l])
        dest, slot_tok, block_e = _moe_plan(top_e, rank, counts.reshape(-1), MOE_TILE)
        ys = moe_experts(h2_packed, slot_tok, w_exp_gate, w_exp_up, w_exp_down, l, block_e)
        routed = moe_combine(ys, dest, top_w)
        d_sh = w_sh_gate.shape[-1]
        act = fused_matmul([(h2, w_sh_gate[l].astype(bf16)), (h2, w_sh_up[l].astype(bf16))], [], _ep_glu,
                           jax.ShapeDtypeStruct((t, d_sh), bf16), _tile_spec(tm, d_sh), tm, d_sh, name="shared_glu")
        xa = fused_matmul(
            [(act, w_sh_down[l].astype(bf16))],
            [(xa, _tile_spec(tm, 512)), (mods[5], ctx_spec), (mods[5], b_spec), (routed, _tile_spec(tm, 512))],
            _make_ep_residual(tm, rows_b, n_ctx, True), jax.ShapeDtypeStruct((t, d), f32),
            _tile_spec(tm, 512), tm, 512, name="moe_out")

    return xa.reshape(n_batch, rows_b, d)[:, n_ctx:]
```

```python
import functools
import math

import jax
import jax.numpy as jnp
from jax import lax
from jax.experimental import pallas as pl
from jax.experimental.pallas import tpu as pltpu

f32 = jnp.float32
bf16 = jnp.bfloat16
i32 = jnp.int32

HEAD_DIM = 128
GRID_W = 64
ROPE_THETA = 10000.0
EPS = 1e-6
F_MIN = 1e-6
SCAN_CHUNK = 64
SCAN_SUB = 8
N_GROUPS = 8
TOPK_GROUPS = 4
TOP_K = 8
ROUTE_SCALE = 2.5
MASK_SCORE = -1e9
LOG2E = 1.4426950408889634

VMEM_LIMIT_BYTES = 48 * 1024 * 1024
ROW_TILE = 256
MM_TILE_M = 768
MOE_TILE = 256
MXU_COLS = 256
SCAN_TILE = 256


def _params(*semantics):
    return pltpu.CompilerParams(dimension_semantics=semantics, vmem_limit_bytes=VMEM_LIMIT_BYTES)


def _silu(x):
    return x * jax.nn.sigmoid(x)


def _nt_dot(a, b):
    return lax.dot_general(a, b, (((1,), (1,)), ((), ())), preferred_element_type=f32)


def _tn_dot(a, b):
    return lax.dot_general(a, b, (((0,), (0,)), ((), ())), preferred_element_type=f32)


def _row_is_ctx(tile_index, tile_rows, rows_per_batch, n_ctx):
    row0 = (tile_index % (rows_per_batch // tile_rows)) * tile_rows
    rid = row0 + lax.broadcasted_iota(i32, (tile_rows, 1), 0)
    return rid < n_ctx


def _fm_kernel(*refs, n_pairs, n_extra, a_fn, epilogue):
    pair_refs = refs[:2 * n_pairs]
    extra = refs[2 * n_pairs:2 * n_pairs + n_extra]
    outs = refs[2 * n_pairs + n_extra:]
    lhs = []
    for p in range(n_pairs):
        a = pair_refs[2 * p][...]
        if a_fn is not None:
            a = a_fn(a)
        lhs.append(a.astype(bf16))
    tn = pair_refs[1].shape[-1]

    def dots(cs):
        accs = []
        for p in range(n_pairs):
            w_ref = pair_refs[2 * p + 1]
            w = w_ref[0, :, cs] if len(w_ref.shape) == 3 else w_ref[:, cs]
            accs.append(jnp.dot(lhs[p], w.astype(bf16), preferred_element_type=f32))
        return accs

    chunks = [slice(c0, min(c0 + MXU_COLS, tn)) for c0 in range(0, tn, MXU_COLS)]
    pending = dots(chunks[0])
    for ci in range(1, len(chunks)):
        accs = dots(chunks[ci])
        epilogue(pending, extra, outs, chunks[ci - 1])
        pending = accs
    epilogue(pending, extra, outs, chunks[-1])


def fused_matmul(pairs, extras, epilogue, out_shapes, out_specs, tm, tn, a_fn=None, name=None, layer=None):
    m = pairs[0][0].shape[0]
    n = pairs[0][1].shape[-1]
    assert m % tm == 0 and n % tn == 0, (m, tm, n, tn)
    in_specs, args = [], []
    for a, w in pairs:
        k = a.shape[1]
        assert w.shape[-2] == k
        if layer is None:
            w_spec = pl.BlockSpec((k, tn), lambda i, j: (0, j))
        else:
            w_spec = pl.BlockSpec((1, k, tn), lambda i, j: (layer, 0, j))
        in_specs += [pl.BlockSpec((tm, k), lambda i, j: (i, 0)), w_spec]
        args += [a, w]
    for arr, spec in extras:
        in_specs.append(spec)
        args.append(arr)
    return pl.pallas_call(
        functools.partial(_fm_kernel, n_pairs=len(pairs), n_extra=len(extras), a_fn=a_fn, epilogue=epilogue),
        grid=(m // tm, n // tn),
        in_specs=in_specs,
        out_specs=out_specs,
        out_shape=out_shapes,
        compiler_params=_params("parallel", "arbitrary"),
        name=name,
    )(*args)


def _tile_spec(tm, tn):
    return pl.BlockSpec((tm, tn), lambda i, j: (i, j))


def _ep_store(dtype):
    def ep(accs, extra, outs, cs):
        outs[0][:, cs] = accs[0].astype(dtype)
    return ep


def _ep_bias(accs, extra, outs, cs):
    outs[0][:, cs] = accs[0] + extra[0][:, cs]


def _ep_bias_store(dtype):
    def ep(accs, extra, outs, cs):
        outs[0][:, cs] = (accs[0] + extra[0][:, cs]).astype(dtype)
    return ep


def _ep_sigmoid(accs, extra, outs, cs):
    outs[0][:, cs] = (0.5 * jnp.tanh(0.5 * accs[0]) + 0.5).astype(outs[0].dtype)


def _ep_qk(accs, extra, outs, cs):
    gain_ref, cos_ref, sin_ref = extra
    acc = accs[0]
    cos = cos_ref[...]
    sin = sin_ref[...]
    for c in range(acc.shape[1] // HEAD_DIM):
        head = cs.start // HEAD_DIM + c
        x = acc[:, c * HEAD_DIM:(c + 1) * HEAD_DIM]
        y = x * lax.rsqrt(jnp.mean(x * x, axis=-1, keepdims=True) + EPS)
        y = y * gain_ref[:, head * HEAD_DIM:(head + 1) * HEAD_DIM]
        outs[0][head] = (y * cos + pltpu.roll(y, HEAD_DIM // 2, 1) * sin).astype(outs[0].dtype)


def _ep_merge(accs, extra, outs, cs):
    y = extra[0][:, cs].astype(f32) * accs[0]
    y = y + extra[1][:, cs].astype(f32) * accs[1]
    y = y + extra[2][:, cs].astype(f32) * accs[2]
    outs[0][:, cs] = y.astype(outs[0].dtype)


def _ep_glu(accs, extra, outs, cs):
    outs[0][:, cs] = (_silu(accs[0]) * accs[1]).astype(outs[0].dtype)


def _make_ep_residual(tm, rows_per_batch, n_ctx, with_routed):
    def ep(accs, extra, outs, cs):
        x_ref, mod_ctx_ref, mod_b_ref = extra[:3]
        y = accs[0]
        if with_routed:
            y = y + extra[3][:, cs]
        is_ctx = _row_is_ctx(pl.program_id(0), tm, rows_per_batch, n_ctx)
        mod = jnp.where(is_ctx, mod_ctx_ref[0, :, cs], mod_b_ref[0, :, cs])
        outs[0][:, cs] = x_ref[:, cs] + mod * y
    return ep


PACK_ROWS = 8


def _pack_words(x):
    bits = lax.bitcast_convert_type(x.astype(bf16).astype(f32), jnp.uint32)
    half = bits.shape[1] // 2
    return [bits[:, half + s * HEAD_DIM:half + (s + 1) * HEAD_DIM] | (bits[:, s * HEAD_DIM:(s + 1) * HEAD_DIM] >> 16)
            for s in range(PACK_ROWS)]


def _unpack_word(word):
    return (lax.bitcast_convert_type(word << 16, f32),
            lax.bitcast_convert_type(word & jnp.uint32(0xFFFF0000), f32))


def _norm_kernel(x_ref, w_ref, shc_ref, shb_ref, scc_ref, scb_ref, o_ref, *packed, tile, rows_per_batch, n_ctx):
    x = x_ref[...]
    y = x * lax.rsqrt(jnp.mean(x * x, axis=-1, keepdims=True) + EPS) * w_ref[...]
    is_ctx = _row_is_ctx(pl.program_id(0), tile, rows_per_batch, n_ctx)
    shift = jnp.where(is_ctx, shc_ref[0], shb_ref[0])
    scale = jnp.where(is_ctx, scc_ref[0], scb_ref[0])
    hb = (y * (1.0 + scale) + shift).astype(bf16)
    o_ref[...] = hb
    if packed:
        for s, word in enumerate(_pack_words(hb)):
            packed[0][pl.ds(s, tile, stride=PACK_ROWS), :] = word


def norm_modulate(x, w, shift, scale, rows_per_batch, n_ctx, packed_rows=False):
    t, d = x.shape
    tile = ROW_TILE
    n_b = shift.shape[0] - 1
    tiles_per_batch = rows_per_batch // tile
    ctx_spec = pl.BlockSpec((1, 1, d), lambda i: (n_b, 0, 0))
    b_spec = pl.BlockSpec((1, 1, d), lambda i: (i // tiles_per_batch, 0, 0))
    row_spec = pl.BlockSpec((tile, d), lambda i: (i, 0))
    out_specs, out_shape = [row_spec], [jax.ShapeDtypeStruct((t, d), bf16)]
    if packed_rows:
        assert d == 2 * PACK_ROWS * HEAD_DIM
        out_specs.append(pl.BlockSpec((tile * PACK_ROWS, HEAD_DIM), lambda i: (i, 0)))
        out_shape.append(jax.ShapeDtypeStruct((t * PACK_ROWS, HEAD_DIM), jnp.uint32))
    return pl.pallas_call(
        functools.partial(_norm_kernel, tile=tile, rows_per_batch=rows_per_batch, n_ctx=n_ctx),
        grid=(t // tile,),
        in_specs=[row_spec, pl.BlockSpec((1, d), lambda i: (0, 0)), ctx_spec, b_spec, ctx_spec, b_spec],
        out_specs=out_specs,
        out_shape=out_shape,
        compiler_params=_params("parallel"),
        name="norm_modulate",
    )(x, w.reshape(1, d), shift, shift, scale, scale)


def _attend(streams, kv_len, tk):
    n = kv_len // tk
    state = [None] * len(streams)
    s_next = [_nt_dot(q, k_at(0)) for q, k_at, _ in streams]
    for c in range(n):
        s_cur = s_next
        if c + 1 < n:
            s_next = [_nt_dot(q, k_at((c + 1) * tk)) for q, k_at, _ in streams]
        for i, (_, _, v_at) in enumerate(streams):
            s = s_cur[i]
            s_max = jnp.max(s, axis=1, keepdims=True)
            if c == 0:
                m = s_max
                p = jnp.exp2(s - m)
                l = jnp.sum(p, axis=1, keepdims=True)
                acc = jnp.dot(p.astype(bf16), v_at(c * tk), preferred_element_type=f32)
            else:
                m_old, l, acc = state[i]
                m = jnp.maximum(m_old, s_max)
                alpha = jnp.exp2(m_old - m)
                p = jnp.exp2(s - m)
                l = alpha * l + jnp.sum(p, axis=1, keepdims=True)
                acc = alpha * acc + jnp.dot(p.astype(bf16), v_at(c * tk), preferred_element_type=f32)
            state[i] = (m, l, acc)
    return [acc / l for _, l, acc in state]


def _attend_sum_in_v(q, k_at, v_at, kv_len, tk, d):
    n = kv_len // tk
    m = acc = None
    s_next = _nt_dot(q, k_at(0))
    for c in range(n):
        s = s_next
        if c + 1 < n:
            s_next = _nt_dot(q, k_at((c + 1) * tk))
        s_max = jnp.max(s, axis=1, keepdims=True)
        if c == 0:
            m = s_max
            acc = jnp.dot(jnp.exp2((s - m).astype(bf16)), v_at(0), preferred_element_type=f32)
        else:
            m_new = jnp.maximum(m, s_max)
            alpha = jnp.exp2(m - m_new)
            acc = alpha * acc + jnp.dot(jnp.exp2((s - m_new).astype(bf16)), v_at(c * tk),
                                        preferred_element_type=f32)
            m = m_new
    return acc[:, :d] / acc[:, d:]


def _kv_chunk(kv_len):
    for tk in (768, 1408, 512, 256, 128):
        if kv_len % tk == 0:
            return tk
    raise ValueError(kv_len)


def _ctx_or_all(n_ctx_tiles, n_ctx, kv_all, run):
    is_ctx = pl.program_id(2) < n_ctx_tiles

    @pl.when(is_ctx)
    def _():
        run(n_ctx)

    @pl.when(jnp.logical_not(is_ctx))
    def _():
        run(kv_all)


def _attn_a_kernel(q_ref, k_ref, v_ref, o_ref, *, n_ctx):
    g, tq, d = q_ref.shape

    def run(kv_len):
        tk = _kv_chunk(kv_len)

        def k_at(off):
            return k_ref[0, off:off + tk, :]

        def v_at(off):
            return v_ref[off:off + tk, :]

        o = _attend_sum_in_v(q_ref[...].reshape(g * tq, d), k_at, v_at, kv_len, tk, d)
        for h in range(g):
            o_ref[:, h * d:(h + 1) * d] = o[h * tq:(h + 1) * tq].astype(o_ref.dtype)

    _ctx_or_all(n_ctx // tq, n_ctx, k_ref.shape[1], run)


def _attn_c_kernel(lam_ref, dn_ref, q_ref, k_ref, v_ref, o_ref, *, n_ctx, lam_init):
    tq = q_ref.shape[1]

    def run(kv_len):
        tk = _kv_chunk(kv_len)
        def v_at(off):
            return v_ref[off:off + tk, :]

        (o1,) = _attend([(q_ref[0], lambda off: k_ref[0, off:off + tk, :], v_at)], kv_len, tk)
        (o2,) = _attend([(q_ref[1], lambda off: k_ref[1, off:off + tk, :], v_at)], kv_len, tk)
        lam_p = lam_ref[...]
        lam = (jnp.exp(jnp.sum(lam_p[0:1] * lam_p[1:2], axis=-1, keepdims=True))
               - jnp.exp(jnp.sum(lam_p[2:3] * lam_p[3:4], axis=-1, keepdims=True)) + lam_init)
        o = o1 - lam * o2
        y = o * lax.rsqrt(jnp.mean(o * o, axis=-1, keepdims=True) + EPS) * dn_ref[...]
        o_ref[...] = (y * (1.0 - lam_init)).astype(o_ref.dtype)

    _ctx_or_all(n_ctx // tq, n_ctx, k_ref.shape[1], run)


def attention_a(qk, va, *, n_batch, rows_per_batch, n_ctx, a_heads, a_kv_heads):
    t = qk.shape[1]
    group = a_heads // a_kv_heads
    tq = ROW_TILE
    n_q = rows_per_batch // tq
    return pl.pallas_call(
        functools.partial(_attn_a_kernel, n_ctx=n_ctx),
        grid=(n_batch, a_kv_heads, n_q),
        in_specs=[
            pl.BlockSpec((group, tq, HEAD_DIM), lambda b, h, i: (h, b * n_q + i, 0)),
            pl.BlockSpec((1, rows_per_batch, HEAD_DIM), lambda b, h, i: (a_heads + h, b, 0)),
            pl.BlockSpec((rows_per_batch, 2 * HEAD_DIM), lambda b, h, i: (b, h)),
        ],
        out_specs=pl.BlockSpec((tq, group * HEAD_DIM), lambda b, h, i: (b * n_q + i, h)),
        out_shape=jax.ShapeDtypeStruct((t, a_heads * HEAD_DIM), bf16),
        compiler_params=_params("parallel", "parallel", "arbitrary"),
        name="attention_a",
    )(qk, qk, va)


def attention_c(qk, vc, lam_params, diff_norm, *, n_batch, rows_per_batch, n_ctx, head0_q, head0_k,
                c_heads, lam_init):
    t = qk.shape[1]
    dv = 2 * HEAD_DIM
    tq = ROW_TILE
    n_q = rows_per_batch // tq
    return pl.pallas_call(
        functools.partial(_attn_c_kernel, n_ctx=n_ctx, lam_init=lam_init),
        grid=(n_batch, c_heads, n_q),
        in_specs=[
            pl.BlockSpec((4, HEAD_DIM), lambda b, h, i: (0, 0)),
            pl.BlockSpec((1, dv), lambda b, h, i: (0, 0)),
            pl.BlockSpec((2, tq, HEAD_DIM), lambda b, h, i: (head0_q // 2 + h, b * n_q + i, 0)),
            pl.BlockSpec((2, rows_per_batch, HEAD_DIM), lambda b, h, i: (head0_k // 2 + h, b, 0)),
            pl.BlockSpec((rows_per_batch, dv), lambda b, h, i: (b, h)),
        ],
        out_specs=pl.BlockSpec((tq, dv), lambda b, h, i: (b * n_q + i, h)),
        out_shape=jax.ShapeDtypeStruct((t, c_heads * dv), bf16),
        compiler_params=_params("parallel", "parallel", "arbitrary"),
        name="attention_c",
    )(lam_params, diff_norm.reshape(1, dv), qk, qk, vc)


def _hgrn_tables(rev):
    c = SCAN_CHUNK
    half = SCAN_SUB // 2
    t_i = lax.broadcasted_iota(i32, (c, c), 0)
    u_i = lax.broadcasted_iota(i32, (c, c), 1)
    blk8 = t_i & ~(SCAN_SUB - 1)
    blk4 = t_i & ~(half - 1)
    if rev:
        rows_upto = [t_i, blk4 + half - 1, blk8 + half, blk8]
        sums = jnp.concatenate([(u_i >= r).astype(f32) for r in rows_upto], axis=0)
    else:
        rows_upto = [t_i, blk4, blk8 + half - 1, blk8 + SCAN_SUB - 1]
        sums = jnp.concatenate([(u_i <= r).astype(f32) for r in rows_upto], axis=0)
    causal = (u_i >= t_i) if rev else (u_i <= t_i)
    same8 = (u_i & ~(SCAN_SUB - 1)) == blk8
    same4 = (u_i & ~(half - 1)) == blk4
    return sums, same8 & jnp.logical_not(same4) & causal, same4 & causal


def _hgrn_gates(qp, z, lb):
    q = _silu(qp.astype(f32))
    one_m = 1.0 - lb
    g = jnp.log(jnp.maximum(lb + one_m * jax.nn.sigmoid(z), F_MIN))
    k = one_m * jax.nn.sigmoid(-z)
    return q, k, g


def _hgrn_local(q, k, b_all, v, tables, rev):
    c = SCAN_CHUNK
    nb = c // SCAN_SUB
    _, across, inner = tables
    b, rho4, rho_mid, rho_edge = (b_all[i * c:(i + 1) * c] for i in range(4))
    end = 0 if rev else c - 1
    b_end = b[end:end + 1, :]
    q_in = (q * jnp.exp(b)).astype(bf16)
    d_st = _tn_dot(v, (k * jnp.exp(b_end - b)).astype(bf16))

    def edge_row(j):
        return j * SCAN_SUB if rev else j * SCAN_SUB + SCAN_SUB - 1

    row = lax.broadcasted_iota(i32, (c, HEAD_DIM), 0)
    k_til = k * jnp.exp(rho_edge - b)
    qs, ks = [], []
    for j in (range(1, nb) if rev else range(nb - 1)):
        rho = b[edge_row(j):edge_row(j) + 1, :]
        q_rows = (row < j * SCAN_SUB) if rev else (row >= (j + 1) * SCAN_SUB)
        qs.append((q * jnp.exp(jnp.where(q_rows, b - rho, -jnp.inf))).astype(bf16))
        k_rows = (row >= j * SCAN_SUB) & (row < (j + 1) * SCAN_SUB)
        ks.append(jnp.where(k_rows, k_til, 0.0).astype(bf16))
    att = _nt_dot(jnp.concatenate(qs, axis=1), jnp.concatenate(ks, axis=1))

    att_mid = _nt_dot((q * jnp.exp(jnp.minimum(b - rho_mid, 0.0))).astype(bf16),
                      (k * jnp.exp(jnp.minimum(rho_mid - b, 0.0))).astype(bf16))
    att_in = _nt_dot((q * jnp.exp(b - rho4)).astype(bf16), (k * jnp.exp(rho4 - b)).astype(bf16))
    att = att + jnp.where(across, att_mid, 0.0) + jnp.where(inner, att_in, 0.0)
    return jnp.dot(att.astype(bf16), v, preferred_element_type=f32), q_in, d_st, jnp.exp(b_end)


def _hgrn_kernel(lb_ref, qf_ref, vf_ref, zf_ref, qb_ref, vb_ref, zb_ref, of_ref, ob_ref, sf_ref, sb_ref):
    @pl.when(pl.program_id(2) == 0)
    def _():
        sf_ref[...] = jnp.zeros_like(sf_ref)
        sb_ref[...] = jnp.zeros_like(sb_ref)

    lb = lb_ref[...]
    n_chunks = qf_ref.shape[0] // SCAN_CHUNK
    work = []
    for c in range(n_chunks):
        work.append((slice(c * SCAN_CHUNK, (c + 1) * SCAN_CHUNK), False))
        work.append((slice((n_chunks - 1 - c) * SCAN_CHUNK, (n_chunks - c) * SCAN_CHUNK), True))
    refs = {False: (qf_ref, vf_ref, zf_ref, of_ref), True: (qb_ref, vb_ref, zb_ref, ob_ref)}
    tables = {False: _hgrn_tables(False), True: _hgrn_tables(True)}

    gates = [_hgrn_gates(refs[rev][0][rows, :], refs[rev][2][rows, :], lb) for rows, rev in work]
    sums = [jnp.dot(tables[rev][0], g, precision=lax.Precision.HIGHEST, preferred_element_type=f32)
            for (_, rev), (_, _, g) in zip(work, gates)]
    local = [_hgrn_local(q, k, b_all, refs[rev][1][rows, :], tables[rev], rev)
             for (rows, rev), (q, k, _), b_all in zip(work, gates, sums)]
    state = {False: sf_ref[...], True: sb_ref[...]}
    for (rows, rev), (o_in, q_in, d_st, decay) in zip(work, local):
        st = state[rev]
        refs[rev][3][rows, :] = o_in + _nt_dot(q_in, st.astype(bf16))
        state[rev] = st * decay + d_st
    sf_ref[...] = state[False]
    sb_ref[...] = state[True]


def hgrn_scan(qio, zfb, lower_bound, *, n_batch, rows_per_batch, b_heads):
    t = qio.shape[0]
    tile = SCAN_TILE
    n_tiles = rows_per_batch // tile

    def fwd(b, h, j):
        return b * n_tiles + j

    def bwd(b, h, j):
        return b * n_tiles + jnp.where(j == 0, 0, n_tiles - j)

    def spec(tile_fn, col0):
        return pl.BlockSpec((tile, HEAD_DIM), lambda b, h, j: (tile_fn(b, h, j), col0 + h))

    return pl.pallas_call(
        _hgrn_kernel,
        grid=(n_batch, b_heads, n_tiles),
        in_specs=[
            pl.BlockSpec((1, HEAD_DIM), lambda b, h, j: (0, h)),
            spec(fwd, 0), spec(fwd, b_heads), spec(fwd, 0),
            spec(bwd, 0), spec(bwd, b_heads), spec(bwd, b_heads),
        ],
        out_specs=[spec(fwd, 0), spec(bwd, 0)],
        out_shape=[jax.ShapeDtypeStruct((t, b_heads * HEAD_DIM), f32)] * 2,
        scratch_shapes=[pltpu.VMEM((HEAD_DIM, HEAD_DIM), f32)] * 2,
        compiler_params=_params("parallel", "parallel", "arbitrary"),
        name="hgrn_scan",
    )(lower_bound.reshape(1, -1), qio, qio, zfb, qio, qio, zfb)


def _hgrn_out_kernel(of_ref, ob_ref, og_ref, w_ref, o_ref):
    o = of_ref[...] + ob_ref[...]
    w = w_ref[...]
    for c in range(o.shape[1] // HEAD_DIM):
        sl = slice(c * HEAD_DIM, (c + 1) * HEAD_DIM)
        x = o[:, sl]
        y = x * lax.rsqrt(jnp.mean(x * x, axis=-1, keepdims=True) + EPS) * w
        o_ref[:, sl] = (y * _silu(og_ref[:, sl].astype(f32))).astype(o_ref.dtype)


def hgrn_readout(o_f, o_b, qio, hgrn_norm, b_heads):
    t, w = o_f.shape
    tile = ROW_TILE
    row_spec = pl.BlockSpec((tile, w), lambda i: (i, 0))
    return pl.pallas_call(
        _hgrn_out_kernel,
        grid=(t // tile,),
        in_specs=[row_spec, row_spec, pl.BlockSpec((tile, w), lambda i: (i, 2)),
                  pl.BlockSpec((1, HEAD_DIM), lambda i: (0, 0))],
        out_specs=row_spec,
        out_shape=jax.ShapeDtypeStruct((t, w), bf16),
        compiler_params=_params("parallel"),
        name="hgrn_readout",
    )(o_f, o_b, qio, hgrn_norm.reshape(1, HEAD_DIM))


def _router_kernel(h_ref, w_ref, b_ref, e_ref, wt_ref, rank_ref, cnt_ref, run_ref):
    n_e = w_ref.shape[0]
    tm = h_ref.shape[0]
    per = n_e // N_GROUPS

    @pl.when(pl.program_id(0) == 0)
    def _():
        run_ref[...] = jnp.zeros_like(run_ref)

    scores = jax.nn.sigmoid(_nt_dot(w_ref[...], h_ref[...]))
    biased = scores + b_ref[...]
    neg = -jnp.inf

    grp = biased.reshape(N_GROUPS, per, tm)
    idx_in = lax.broadcasted_iota(i32, grp.shape, 1)
    m1 = jnp.max(grp, axis=1, keepdims=True)
    first = jnp.min(jnp.where(grp == m1, idx_in, per), axis=1, keepdims=True)
    m2 = jnp.max(jnp.where(idx_in == first, neg, grp), axis=1, keepdims=True)
    cur = m1 + m2

    g_idx = lax.broadcasted_iota(i32, cur.shape, 0)
    g_sel = jnp.zeros(cur.shape, f32)
    for _ in range(TOPK_GROUPS):
        mx = jnp.max(cur, axis=0, keepdims=True)
        pick = g_idx == jnp.min(jnp.where(cur == mx, g_idx, N_GROUPS), axis=0, keepdims=True)
        g_sel = jnp.where(pick, 1.0, g_sel)
        cur = jnp.where(pick, neg, cur)
    e_mask = jnp.broadcast_to(g_sel, (N_GROUPS, per, tm)).reshape(n_e, tm) > 0.0

    e_idx = lax.broadcasted_iota(i32, (n_e, tm), 0)
    cur = jnp.where(e_mask, biased, MASK_SCORE)
    ws, picks = [], []
    member = jnp.zeros((n_e, tm), f32)
    for r in range(TOP_K):
        mx = jnp.max(cur, axis=0, keepdims=True)
        first = jnp.min(jnp.where(cur == mx, e_idx, n_e), axis=0, keepdims=True)
        pick = e_idx == first
        e_ref[r:r + 1, :] = first
        ws.append(jnp.sum(jnp.where(pick, scores, 0.0), axis=0, keepdims=True))
        cur = jnp.where(pick, neg, cur)
        member = jnp.where(pick, 1.0, member)
        picks.append(pick)
    total = ws[0]
    for r in range(1, TOP_K):
        total = total + ws[r]
    for r in range(TOP_K):
        wt_ref[r:r + 1, :] = ws[r] / total * ROUTE_SCALE

    u_i = lax.broadcasted_iota(i32, (tm, tm), 0)
    t_i = lax.broadcasted_iota(i32, (tm, tm), 1)
    prefix = jnp.dot(member.astype(bf16), (u_i <= t_i).astype(bf16), preferred_element_type=f32)
    rank = run_ref[...] + prefix
    for r in range(TOP_K):
        rank_ref[r:r + 1, :] = jnp.sum(jnp.where(picks[r], rank, 0.0), axis=0, keepdims=True).astype(i32)
    run_ref[...] = run_ref[...] + jnp.sum(member, axis=1, keepdims=True)
    cnt_ref[...] = run_ref[...].astype(i32)


def moe_route(h, w_router_t, b_router):
    t, d = h.shape
    n_e = w_router_t.shape[0]
    tm = 512
    assert t % tm == 0
    out_spec = pl.BlockSpec((TOP_K, tm), lambda i: (0, i))
    return pl.pallas_call(
        _router_kernel,
        grid=(t // tm,),
        in_specs=[pl.BlockSpec((tm, d), lambda i: (i, 0)), pl.BlockSpec((n_e, d), lambda i: (0, 0)),
                  pl.BlockSpec((n_e, 1), lambda i: (0, 0))],
        out_specs=[out_spec, out_spec, out_spec, pl.BlockSpec((n_e, 1), lambda i: (0, 0))],
        out_shape=[jax.ShapeDtypeStruct((TOP_K, t), i32), jax.ShapeDtypeStruct((TOP_K, t), f32),
                   jax.ShapeDtypeStruct((TOP_K, t), i32), jax.ShapeDtypeStruct((n_e, 1), i32)],
        scratch_shapes=[pltpu.VMEM((n_e, 1), f32)],
        compiler_params=_params("arbitrary"),
        name="moe_route",
    )(h, w_router_t, b_router.reshape(n_e, 1).astype(f32))


GATHER_DEPTH = 3


def _expert_kernel(be_ref, n_used_ref, idx0_ref, idx1_ref, idx_ahead_ref, h_ref, wg_ref, wu_ref, wd_ref, y_ref,
                   xbuf, sem, x_s, wg_b, wu_b, wd_b):
    i = pl.program_id(0)
    last = pl.num_programs(0) - 1
    tm = xbuf.shape[1] // PACK_ROWS
    slot = i % GATHER_DEPTH
    ahead = (i + GATHER_DEPTH - 1) % GATHER_DEPTH

    def gather_start(idx_ref, dst_slot):
        for r in range(tm):
            src_row = pl.multiple_of(idx_ref[0, 0, r] * PACK_ROWS, PACK_ROWS)
            pltpu.make_async_copy(h_ref.at[pl.ds(src_row, PACK_ROWS)],
                                  xbuf.at[dst_slot, pl.ds(r * PACK_ROWS, PACK_ROWS)],
                                  sem.at[dst_slot]).start(priority=r % 2)

    def gather_wait(dst_slot):
        pltpu.make_async_copy(h_ref.at[pl.ds(0, tm * PACK_ROWS)], xbuf.at[dst_slot], sem.at[dst_slot]).wait()

    @pl.when(i == 0)
    def _():
        gather_start(idx0_ref, 0)
        gather_start(idx1_ref, 1)

    @pl.when((i == 0) | (be_ref[i] != be_ref[jnp.maximum(i - 1, 0)]))
    def _():
        wg_b[...] = wg_ref[0, 0].astype(bf16)
        wu_b[...] = wu_ref[0, 0].astype(bf16)
        wd_b[...] = wd_ref[0, 0].astype(bf16)

    gather_wait(slot)

    @pl.when(i < n_used_ref[0])
    def _():
        half = PACK_ROWS * HEAD_DIM
        for s in range(PACK_ROWS):
            lo, hi = _unpack_word(xbuf[slot, pl.ds(s, tm, stride=PACK_ROWS), :])
            x_s[:, s * HEAD_DIM:(s + 1) * HEAD_DIM] = lo.astype(bf16)
            x_s[:, half + s * HEAD_DIM:half + (s + 1) * HEAD_DIM] = hi.astype(bf16)
        gather_start(idx_ahead_ref, ahead)
        x = x_s[...]
        g = jnp.dot(x, wg_b[...], preferred_element_type=f32)
        u = jnp.dot(x, wu_b[...], preferred_element_type=f32)
        a = (_silu(g) * u).astype(bf16)
        y = jnp.dot(a, wd_b[...], preferred_element_type=f32)
        for s, word in enumerate(_pack_words(y)):
            y_ref[pl.ds(s, tm, stride=PACK_ROWS), :] = word

    @pl.when(i >= n_used_ref[0])
    def _():
        gather_start(idx_ahead_ref, ahead)
        y_ref[...] = jnp.zeros_like(y_ref)

    @pl.when(i == last)
    def _():
        for k in range(1, GATHER_DEPTH):
            gather_wait((i + k) % GATHER_DEPTH)


def moe_experts(h_packed, slot_tok, w_gate, w_up, w_down, layer, block_e, n_used):
    tm = slot_tok.shape[0] // block_e.shape[0]
    n_blocks = block_e.shape[0]
    assert n_blocks >= GATHER_DEPTH
    d, d_e = w_gate.shape[-2:]
    idx = slot_tok.reshape(n_blocks, 1, tm)

    def idx_spec(block_fn):
        return pl.BlockSpec((1, 1, tm), lambda i, be, nu: (block_fn(i), 0, 0), memory_space=pltpu.SMEM)

    def w_spec(shape):
        return pl.BlockSpec((1, 1) + shape, lambda i, be, nu: (layer, be[i], 0, 0))

    return pl.pallas_call(
        _expert_kernel,
        grid_spec=pltpu.PrefetchScalarGridSpec(
            num_scalar_prefetch=2,
            grid=(n_blocks,),
            in_specs=[idx_spec(lambda i: 0), idx_spec(lambda i: 1),
                      idx_spec(lambda i: jnp.minimum(i + GATHER_DEPTH - 1, n_blocks - 1)),
                      pl.BlockSpec(memory_space=pl.ANY),
                      w_spec((d, d_e)), w_spec((d, d_e)), w_spec((d_e, d))],
            out_specs=pl.BlockSpec((tm * PACK_ROWS, HEAD_DIM), lambda i, be, nu: (i, 0)),
            scratch_shapes=[pltpu.VMEM((GATHER_DEPTH, tm * PACK_ROWS, HEAD_DIM), jnp.uint32),
                            pltpu.SemaphoreType.DMA((GATHER_DEPTH,)), pltpu.VMEM((tm, d), bf16),
                            pltpu.VMEM((d, d_e), bf16), pltpu.VMEM((d, d_e), bf16), pltpu.VMEM((d_e, d), bf16)],
        ),
        out_shape=jax.ShapeDtypeStruct((n_blocks * tm * PACK_ROWS, HEAD_DIM), jnp.uint32),
        compiler_params=_params("arbitrary"),
        name="moe_experts",
    )(block_e, n_used, idx, idx, idx, h_packed, w_gate, w_up, w_down)


def _combine_kernel(idx_ref, idx_next_ref, w_ref, y_ref, o_ref, buf, sem):
    i = pl.program_id(0)
    last = pl.num_programs(0) - 1
    n_k = buf.shape[1]
    rows = buf.shape[2] // PACK_ROWS
    slot = i % 2

    def gather_start(src_idx_ref, dst_slot):
        for r in range(rows):
            for kk in range(n_k):
                src_row = pl.multiple_of(src_idx_ref[0, kk, r] * PACK_ROWS, PACK_ROWS)
                pltpu.make_async_copy(y_ref.at[pl.ds(src_row, PACK_ROWS)],
                                      buf.at[dst_slot, kk, pl.ds(r * PACK_ROWS, PACK_ROWS)],
                                      sem.at[dst_slot]).start(priority=(r * n_k + kk) % 2)

    def gather_wait(dst_slot):
        for kk in range(n_k):
            pltpu.make_async_copy(y_ref.at[pl.ds(0, rows * PACK_ROWS)], buf.at[dst_slot, kk],
                                  sem.at[dst_slot]).wait()

    @pl.when(i == 0)
    def _():
        gather_start(idx_ref, 0)

    gather_start(idx_next_ref, 1 - slot)
    gather_wait(slot)
    w = w_ref[...]
    half = PACK_ROWS * HEAD_DIM
    for s in range(PACK_ROWS):
        acc_lo = acc_hi = None
        for kk in range(n_k):
            lo, hi = _unpack_word(buf[slot, kk, pl.ds(s, rows, stride=PACK_ROWS), :])
            w_k = w[:, kk:kk + 1]
            acc_lo = lo * w_k if kk == 0 else acc_lo + lo * w_k
            acc_hi = hi * w_k if kk == 0 else acc_hi + hi * w_k
        o_ref[:, s * HEAD_DIM:(s + 1) * HEAD_DIM] = acc_lo
        o_ref[:, half + s * HEAD_DIM:half + (s + 1) * HEAD_DIM] = acc_hi

    @pl.when(i == last)
    def _():
        gather_wait(1 - slot)


def moe_combine(y, dest, top_w):
    n_k, t = dest.shape
    d = 2 * PACK_ROWS * HEAD_DIM
    rows = 128
    n_tiles = t // rows
    idx = dest.reshape(n_k, n_tiles, rows).transpose(1, 0, 2)

    def idx_spec(tile_fn):
        return pl.BlockSpec((1, n_k, rows), lambda i: (tile_fn(i), 0, 0), memory_space=pltpu.SMEM)

    return pl.pallas_call(
        _combine_kernel,
        grid=(n_tiles,),
        in_specs=[idx_spec(lambda i: i), idx_spec(lambda i: jnp.minimum(i + 1, n_tiles - 1)),
                  pl.BlockSpec((rows, n_k), lambda i: (i, 0)),
                  pl.BlockSpec(memory_space=pl.ANY)],
        out_specs=pl.BlockSpec((rows, d), lambda i: (i, 0)),
        out_shape=jax.ShapeDtypeStruct((t, d), f32),
        scratch_shapes=[pltpu.VMEM((2, n_k, rows * PACK_ROWS, HEAD_DIM), jnp.uint32),
                        pltpu.SemaphoreType.DMA((2,))],
        compiler_params=_params("arbitrary"),
        name="moe_combine",
    )(idx, idx, top_w.T, y)


def _moe_plan(top_e, rank, counts, tm):
    n_k, t = top_e.shape
    n_experts = counts.shape[0]
    n_blocks = -(-(n_k * t) // tm) + n_experts
    padded = (counts + tm - 1) // tm * tm
    pad_end = jnp.cumsum(padded)
    pad_start = pad_end - padded
    onehot = top_e[:, :, None] == jnp.arange(n_experts, dtype=i32)
    dest = jnp.sum(jnp.where(onehot, pad_start, 0), axis=-1) + rank - 1
    n_used = pad_end[-1] // tm
    blk = jnp.minimum(jnp.arange(n_blocks, dtype=i32), n_used - 1)
    block_e = jnp.sum((pad_end[None, :] <= (blk * tm)[:, None]).astype(i32), axis=1)
    block_e = jnp.minimum(block_e, n_experts - 1)
    tok = jnp.broadcast_to(jnp.arange(t, dtype=i32)[None, :], (n_k, t))
    sorted_tok = jnp.sort((top_e * t + tok).reshape(-1)) % t
    of_block = block_e[:, None] == jnp.arange(n_experts, dtype=i32)
    grp_start = jnp.cumsum(counts) - counts

    def per_block(table):
        return jnp.sum(jnp.where(of_block, table, 0), axis=1, keepdims=True)

    j = jnp.arange(n_blocks * tm, dtype=i32).reshape(n_blocks, tm) - per_block(pad_start)
    valid = (j >= 0) & (j < per_block(counts))
    src = jnp.clip(per_block(grp_start) + j, 0, n_k * t - 1)
    slot_tok = jnp.where(valid, jnp.take(sorted_tok, src.reshape(-1)).reshape(n_blocks, tm), 0)
    return dest.astype(i32), slot_tok.reshape(-1), block_e, n_used.astype(i32).reshape(1)


def _rope_perm():
    quarter = HEAD_DIM // 4
    return jnp.concatenate([jnp.arange(q * quarter, (q + 1) * quarter, dtype=i32) for q in (0, 2, 1, 3)])


def _rope_tables(n_ctx, n_lat):
    t = jnp.arange(n_lat)
    pos = jnp.stack([t // GRID_W, t % GRID_W], axis=-1).astype(f32)
    n_freq = HEAD_DIM // 4
    inv_freq = ROPE_THETA ** (-jnp.arange(n_freq, dtype=f32) / n_freq)
    ang = pos[:, :, None] * inv_freq
    cos, sin = jnp.cos(ang), jnp.sin(ang)
    cos_t = jnp.concatenate([cos[:, 0], cos[:, 1], cos[:, 0], cos[:, 1]], axis=-1)
    sin_t = jnp.concatenate([-sin[:, 0], -sin[:, 1], sin[:, 0], sin[:, 1]], axis=-1)
    cos_t = jnp.concatenate([jnp.ones((n_ctx, HEAD_DIM), f32), cos_t], axis=0)
    sin_t = jnp.concatenate([jnp.zeros((n_ctx, HEAD_DIM), f32), sin_t], axis=0)
    return cos_t, sin_t


def kernel(x, c, ctx, c_ctx, w_ada, b_ada, norm_mix, norm_ffn, w_in, q_norm_a, k_norm_a, hgrn_lb_logits, hgrn_out_norm, q_norm_c, k_norm_c, lambda_q1, lambda_k1, lambda_q2, lambda_k2, diff_sub_norm, w_branch, w_out, w_router, b_router, w_exp_gate, w_exp_up, w_exp_down, w_sh_gate, w_sh_up, w_sh_down):
    n_batch, n_lat, d = x.shape
    n_ctx = ctx.shape[1]
    depth = w_ada.shape[0]
    rows_b = n_ctx + n_lat
    t = n_batch * rows_b
    branch_w = d // 2
    a_heads = branch_w // HEAD_DIM
    a_kv = a_heads // 4
    b_heads = branch_w // HEAD_DIM
    c_heads = branch_w // (2 * HEAD_DIM)
    n_experts = w_router.shape[-1]
    tm = MM_TILE_M if rows_b % MM_TILE_M == 0 else ROW_TILE
    tiles_per_batch = rows_b // tm
    assert n_ctx % ROW_TILE == 0 and n_lat % ROW_TILE == 0 and rows_b % SCAN_TILE == 0

    cos_t, sin_t = _rope_tables(n_ctx, n_lat)
    p_lb = jax.nn.softmax(hgrn_lb_logits.astype(f32), axis=0)
    lower_bounds = jnp.cumsum(p_lb, axis=0) - p_lb[0]
    cond = jnp.zeros((8, d), f32).at[:n_batch].set(c).at[n_batch].set(c_ctx)

    xa = jnp.concatenate([ctx, x], axis=1).reshape(t, d)

    def mod_specs(tn):
        ctx_spec = pl.BlockSpec((1, 1, tn), lambda i, j: (n_batch, 0, j))
        b_spec = pl.BlockSpec((1, 1, tn), lambda i, j: (i // tiles_per_batch, 0, j))
        return ctx_spec, b_spec

    splits = (a_heads * HEAD_DIM, a_kv * HEAD_DIM, a_kv * HEAD_DIM,
              branch_w, branch_w, branch_w, branch_w, branch_w,
              branch_w, branch_w, branch_w, 3 * d)
    offs = [0]
    for s in splits:
        offs.append(offs[-1] + s)

    def cols(w, *ids):
        return jnp.concatenate([w[:, offs[i]:offs[i + 1]] for i in ids], axis=1).astype(bf16)

    n_qk_heads = a_heads + a_kv + 4 * c_heads
    qk_tn = (n_qk_heads // 2) * HEAD_DIM if n_qk_heads % 2 == 0 else n_qk_heads * HEAD_DIM
    sm_scale = HEAD_DIM ** -0.5 * LOG2E

    for l in range(depth):
        lam_init = 0.8 - 0.6 * math.exp(-0.3 * l)
        mod = fused_matmul([(cond, w_ada)], [(b_ada[l].reshape(1, -1), pl.BlockSpec((1, 1024), lambda i, j: (0, j)))],
                           _ep_bias, jax.ShapeDtypeStruct((8, 6 * d), f32), _tile_spec(8, 1024), 8, 1024,
                           a_fn=_silu, name="ada_ln", layer=l)
        mod = mod[:n_batch + 1].reshape(n_batch + 1, 6, 1, d)
        mods = [mod[:, k] for k in range(6)]

        (h,) = norm_modulate(xa, norm_mix[l], mods[0], mods[1], rows_b, n_ctx)
        w_l = w_in[l]
        perm = _rope_perm()
        gains = jnp.concatenate([jnp.tile(q_norm_a[l][perm] * sm_scale, a_heads),
                                 jnp.tile(k_norm_a[l][perm], a_kv),
                                 jnp.tile(q_norm_c[l][perm] * sm_scale, 2 * c_heads),
                                 jnp.tile(k_norm_c[l][perm], 2 * c_heads)])
        w_qk = cols(w_l, 0, 1, 8, 9)
        w_qk = w_qk.reshape(d, n_qk_heads, HEAD_DIM)[:, :, perm].reshape(d, n_qk_heads * HEAD_DIM)
        qk = fused_matmul(
            [(h, w_qk)],
            [(gains.reshape(1, -1).astype(f32), pl.BlockSpec((1, qk_tn), lambda i, j: (0, j))),
             (cos_t, pl.BlockSpec((tm, HEAD_DIM), lambda i, j: (i % tiles_per_batch, 0))),
             (sin_t, pl.BlockSpec((tm, HEAD_DIM), lambda i, j: (i % tiles_per_batch, 0)))],
            _ep_qk, jax.ShapeDtypeStruct((n_qk_heads, t, HEAD_DIM), bf16),
            pl.BlockSpec((qk_tn // HEAD_DIM, tm, HEAD_DIM), lambda i, j: (j, i, 0)), tm, qk_tn, name="in_proj_qk")
        w_va = jnp.pad(cols(w_l, 2).reshape(d, a_kv, HEAD_DIM), ((0, 0), (0, 0), (0, HEAD_DIM)))
        ones_cols = jnp.tile(jnp.concatenate([jnp.zeros((HEAD_DIM,), f32), jnp.ones((HEAD_DIM,), f32)]), a_kv)
        va_n = 2 * a_kv * HEAD_DIM
        va = fused_matmul([(h, w_va.reshape(d, va_n))],
                          [(ones_cols.reshape(1, va_n), pl.BlockSpec((1, va_n), lambda i, j: (0, 0)))],
                          _ep_bias_store(bf16), jax.ShapeDtypeStruct((t, va_n), bf16),
                          _tile_spec(tm, va_n), tm, va_n, name="in_proj_va")
        vc = fused_matmul([(h, cols(w_l, 10))], [], _ep_store(bf16), jax.ShapeDtypeStruct((t, branch_w), bf16),
                          _tile_spec(tm, 1024), tm, 1024, name="in_proj_vc")
        qio = fused_matmul([(h, cols(w_l, 3, 4, 7))], [], _ep_store(bf16), jax.ShapeDtypeStruct((t, 3 * branch_w), bf16),
                           _tile_spec(tm, 1024), tm, 1024, name="in_proj_hgrn")
        zfb = fused_matmul([(h, cols(w_l, 5, 6))], [], _ep_store(f32), jax.ShapeDtypeStruct((t, 2 * branch_w), f32),
                           _tile_spec(tm, 1024), tm, 1024, name="in_proj_forget")
        gates = fused_matmul([(h, cols(w_l, 11))], [], _ep_sigmoid, jax.ShapeDtypeStruct((t, 3 * d), bf16),
                             _tile_spec(tm, 1024), tm, 1024, name="in_proj_gates")

        dims = dict(n_batch=n_batch, rows_per_batch=rows_b, n_ctx=n_ctx)
        o_a = attention_a(qk, va, a_heads=a_heads, a_kv_heads=a_kv, **dims)
        lam_params = jnp.stack([lambda_q1[l], lambda_k1[l], lambda_q2[l], lambda_k2[l]]).astype(f32)
        o_c = attention_c(qk, vc, lam_params, diff_sub_norm[l], head0_q=a_heads + a_kv,
                          head0_k=a_heads + a_kv + 2 * c_heads, c_heads=c_heads, lam_init=lam_init, **dims)
        o_f, o_r = hgrn_scan(qio, zfb, lower_bounds[l], n_batch=n_batch, rows_per_batch=rows_b, b_heads=b_heads)
        o_b = hgrn_readout(o_f, o_r, qio, hgrn_out_norm[l], b_heads)

        wb = w_branch[l].astype(bf16)
        n_col = d // 512
        y = fused_matmul(
            [(o_a, wb[0]), (o_b, wb[1]), (o_c, wb[2])],
            [(gates, pl.BlockSpec((tm, 512), lambda i, j, r=r: (i, r * n_col + j))) for r in range(3)],
            _ep_merge, jax.ShapeDtypeStruct((t, d), bf16), _tile_spec(tm, 512), tm, 512, name="merge")
        ctx_spec, b_spec = mod_specs(512)
        xa = fused_matmul(
            [(y, w_out[l].astype(bf16))],
            [(xa, _tile_spec(tm, 512)), (mods[2], ctx_spec), (mods[2], b_spec)],
            _make_ep_residual(tm, rows_b, n_ctx, False), jax.ShapeDtypeStruct((t, d), f32),
            _tile_spec(tm, 512), tm, 512, name="out_proj")

        h2, h2_packed = norm_modulate(xa, norm_ffn[l], mods[3], mods[4], rows_b, n_ctx, packed_rows=True)
        top_e, top_w, rank, counts = moe_route(h2, w_router[l].T.astype(bf16), b_router[l])
        dest, slot_tok, block_e, n_used = _moe_plan(top_e, rank, counts.reshape(-1), MOE_TILE)
        ys = moe_experts(h2_packed, slot_tok, w_exp_gate, w_exp_up, w_exp_down, l, block_e, n_used)
        routed = moe_combine(ys, dest, top_w)
        d_sh = w_sh_gate.shape[-1]
        act = fused_matmul([(h2, w_sh_gate[l].astype(bf16)), (h2, w_sh_up[l].astype(bf16))], [], _ep_glu,
                           jax.ShapeDtypeStruct((t, d_sh), bf16), _tile_spec(tm, d_sh), tm, d_sh, name="shared_glu")
        xa = fused_matmul(
            [(act, w_sh_down[l].astype(bf16))],
            [(xa, _tile_spec(tm, 512)), (mods[5], ctx_spec), (mods[5], b_spec), (routed, _tile_spec(tm, 512))],
            _make_ep_residual(tm, rows_b, n_ctx, True), jax.ShapeDtypeStruct((t, d), f32),
            _tile_spec(tm, 512), tm, 512, name="moe_out")

    return xa.reshape(n_batch, rows_b, d)[:, n_ctx:]
```

```python
import functools
import math

import jax
import jax.numpy as jnp
from jax import lax
from jax.experimental import pallas as pl
from jax.experimental.pallas import tpu as pltpu

f32 = jnp.float32
bf16 = jnp.bfloat16
i32 = jnp.int32

HEAD_DIM = 128
GRID_W = 64
ROPE_THETA = 10000.0
EPS = 1e-6
F_MIN = 1e-6
SCAN_CHUNK = 64
SCAN_SUB = 8
N_GROUPS = 8
TOPK_GROUPS = 4
TOP_K = 8
ROUTE_SCALE = 2.5
MASK_SCORE = -1e9
LOG2E = 1.4426950408889634

VMEM_LIMIT_BYTES = 48 * 1024 * 1024
ROW_TILE = 256
MM_TILE_M = 768
MOE_TILE = 256
MXU_COLS = 256
SCAN_TILE = 256


def _params(*semantics):
    return pltpu.CompilerParams(dimension_semantics=semantics, vmem_limit_bytes=VMEM_LIMIT_BYTES)


def _silu(x):
    return x * jax.nn.sigmoid(x)


def _nt_dot(a, b):
    return lax.dot_general(a, b, (((1,), (1,)), ((), ())), preferred_element_type=f32)


def _tn_dot(a, b):
    return lax.dot_general(a, b, (((0,), (0,)), ((), ())), preferred_element_type=f32)


def _row_is_ctx(tile_index, tile_rows, rows_per_batch, n_ctx):
    row0 = (tile_index % (rows_per_batch // tile_rows)) * tile_rows
    rid = row0 + lax.broadcasted_iota(i32, (tile_rows, 1), 0)
    return rid < n_ctx


def _fm_kernel(*refs, n_pairs, n_extra, a_fn, epilogue):
    pair_refs = refs[:2 * n_pairs]
    extra = refs[2 * n_pairs:2 * n_pairs + n_extra]
    outs = refs[2 * n_pairs + n_extra:]
    lhs = []
    for p in range(n_pairs):
        a = pair_refs[2 * p][...]
        if a_fn is not None:
            a = a_fn(a)
        lhs.append(a.astype(bf16))
    tn = pair_refs[1].shape[-1]

    def dots(cs):
        accs = []
        for p in range(n_pairs):
            w_ref = pair_refs[2 * p + 1]
            w = w_ref[0, :, cs] if len(w_ref.shape) == 3 else w_ref[:, cs]
            accs.append(jnp.dot(lhs[p], w.astype(bf16), preferred_element_type=f32))
        return accs

    chunks = [slice(c0, min(c0 + MXU_COLS, tn)) for c0 in range(0, tn, MXU_COLS)]
    pending = dots(chunks[0])
    for ci in range(1, len(chunks)):
        accs = dots(chunks[ci])
        epilogue(pending, extra, outs, chunks[ci - 1])
        pending = accs
    epilogue(pending, extra, outs, chunks[-1])


def fused_matmul(pairs, extras, epilogue, out_shapes, out_specs, tm, tn, a_fn=None, name=None, layer=None):
    m = pairs[0][0].shape[0]
    n = pairs[0][1].shape[-1]
    assert m % tm == 0 and n % tn == 0, (m, tm, n, tn)
    in_specs, args = [], []
    for a, w in pairs:
        k = a.shape[1]
        assert w.shape[-2] == k
        if layer is None:
            w_spec = pl.BlockSpec((k, tn), lambda i, j: (0, j))
        else:
            w_spec = pl.BlockSpec((1, k, tn), lambda i, j: (layer, 0, j))
        in_specs += [pl.BlockSpec((tm, k), lambda i, j: (i, 0)), w_spec]
        args += [a, w]
    for arr, spec in extras:
        in_specs.append(spec)
        args.append(arr)
    return pl.pallas_call(
        functools.partial(_fm_kernel, n_pairs=len(pairs), n_extra=len(extras), a_fn=a_fn, epilogue=epilogue),
        grid=(m // tm, n // tn),
        in_specs=in_specs,
        out_specs=out_specs,
        out_shape=out_shapes,
        compiler_params=_params("parallel", "arbitrary"),
        name=name,
    )(*args)


def _tile_spec(tm, tn):
    return pl.BlockSpec((tm, tn), lambda i, j: (i, j))


def _ep_store(dtype):
    def ep(accs, extra, outs, cs):
        outs[0][:, cs] = accs[0].astype(dtype)
    return ep


def _ep_bias(accs, extra, outs, cs):
    outs[0][:, cs] = accs[0] + extra[0][:, cs]


def _ep_bias_store(dtype):
    def ep(accs, extra, outs, cs):
        outs[0][:, cs] = (accs[0] + extra[0][:, cs]).astype(dtype)
    return ep


def _ep_sigmoid(accs, extra, outs, cs):
    outs[0][:, cs] = (0.5 * jnp.tanh(0.5 * accs[0]) + 0.5).astype(outs[0].dtype)


def _ep_qk(accs, extra, outs, cs):
    gain_ref, cos_ref, sin_ref = extra
    acc = accs[0]
    cos = cos_ref[...]
    sin = sin_ref[...]
    for c in range(acc.shape[1] // HEAD_DIM):
        head = cs.start // HEAD_DIM + c
        x = acc[:, c * HEAD_DIM:(c + 1) * HEAD_DIM]
        y = x * lax.rsqrt(jnp.mean(x * x, axis=-1, keepdims=True) + EPS)
        y = y * gain_ref[:, head * HEAD_DIM:(head + 1) * HEAD_DIM]
        outs[0][head] = (y * cos + pltpu.roll(y, HEAD_DIM // 2, 1) * sin).astype(outs[0].dtype)


def _ep_merge(accs, extra, outs, cs):
    y = extra[0][:, cs].astype(f32) * accs[0]
    y = y + extra[1][:, cs].astype(f32) * accs[1]
    y = y + extra[2][:, cs].astype(f32) * accs[2]
    outs[0][:, cs] = y.astype(outs[0].dtype)


def _ep_glu(accs, extra, outs, cs):
    outs[0][:, cs] = (_silu(accs[0]) * accs[1]).astype(outs[0].dtype)


def _make_ep_residual(tm, rows_per_batch, n_ctx, with_routed):
    def ep(accs, extra, outs, cs):
        x_ref, mod_ctx_ref, mod_b_ref = extra[:3]
        y = accs[0]
        if with_routed:
            y = y + extra[3][:, cs]
        is_ctx = _row_is_ctx(pl.program_id(0), tm, rows_per_batch, n_ctx)
        mod = jnp.where(is_ctx, mod_ctx_ref[0, :, cs], mod_b_ref[0, :, cs])
        outs[0][:, cs] = x_ref[:, cs] + mod * y
    return ep


PACK_ROWS = 8


def _pack_words(x):
    bits = lax.bitcast_convert_type(x.astype(bf16).astype(f32), jnp.uint32)
    half = bits.shape[1] // 2
    return [bits[:, half + s * HEAD_DIM:half + (s + 1) * HEAD_DIM] | (bits[:, s * HEAD_DIM:(s + 1) * HEAD_DIM] >> 16)
            for s in range(PACK_ROWS)]


def _unpack_word(word):
    return (lax.bitcast_convert_type(word << 16, f32),
            lax.bitcast_convert_type(word & jnp.uint32(0xFFFF0000), f32))


def _norm_kernel(x_ref, w_ref, shc_ref, shb_ref, scc_ref, scb_ref, o_ref, *packed, tile, rows_per_batch, n_ctx):
    x = x_ref[...]
    y = x * lax.rsqrt(jnp.mean(x * x, axis=-1, keepdims=True) + EPS) * w_ref[...]
    is_ctx = _row_is_ctx(pl.program_id(0), tile, rows_per_batch, n_ctx)
    shift = jnp.where(is_ctx, shc_ref[0], shb_ref[0])
    scale = jnp.where(is_ctx, scc_ref[0], scb_ref[0])
    hb = (y * (1.0 + scale) + shift).astype(bf16)
    o_ref[...] = hb
    if packed:
        for s, word in enumerate(_pack_words(hb)):
            packed[0][pl.ds(s, tile, stride=PACK_ROWS), :] = word


def norm_modulate(x, w, shift, scale, rows_per_batch, n_ctx, packed_rows=False):
    t, d = x.shape
    tile = ROW_TILE
    n_b = shift.shape[0] - 1
    tiles_per_batch = rows_per_batch // tile
    ctx_spec = pl.BlockSpec((1, 1, d), lambda i: (n_b, 0, 0))
    b_spec = pl.BlockSpec((1, 1, d), lambda i: (i // tiles_per_batch, 0, 0))
    row_spec = pl.BlockSpec((tile, d), lambda i: (i, 0))
    out_specs, out_shape = [row_spec], [jax.ShapeDtypeStruct((t, d), bf16)]
    if packed_rows:
        assert d == 2 * PACK_ROWS * HEAD_DIM
        out_specs.append(pl.BlockSpec((tile * PACK_ROWS, HEAD_DIM), lambda i: (i, 0)))
        out_shape.append(jax.ShapeDtypeStruct((t * PACK_ROWS, HEAD_DIM), jnp.uint32))
    return pl.pallas_call(
        functools.partial(_norm_kernel, tile=tile, rows_per_batch=rows_per_batch, n_ctx=n_ctx),
        grid=(t // tile,),
        in_specs=[row_spec, pl.BlockSpec((1, d), lambda i: (0, 0)), ctx_spec, b_spec, ctx_spec, b_spec],
        out_specs=out_specs,
        out_shape=out_shape,
        compiler_params=_params("parallel"),
        name="norm_modulate",
    )(x, w.reshape(1, d), shift, shift, scale, scale)


def _attend(streams, kv_len, tk):
    n = kv_len // tk
    state = [None] * len(streams)
    s_next = [_nt_dot(q, k_at(0)) for q, k_at, _ in streams]
    for c in range(n):
        s_cur = s_next
        if c + 1 < n:
            s_next = [_nt_dot(q, k_at((c + 1) * tk)) for q, k_at, _ in streams]
        for i, (_, _, v_at) in enumerate(streams):
            s = s_cur[i]
            s_max = jnp.max(s, axis=1, keepdims=True)
            if c == 0:
                m = s_max
                p = jnp.exp2(s - m)
                l = jnp.sum(p, axis=1, keepdims=True)
                acc = jnp.dot(p.astype(bf16), v_at(c * tk), preferred_element_type=f32)
            else:
                m_old, l, acc = state[i]
                m = jnp.maximum(m_old, s_max)
                alpha = jnp.exp2(m_old - m)
                p = jnp.exp2(s - m)
                l = alpha * l + jnp.sum(p, axis=1, keepdims=True)
                acc = alpha * acc + jnp.dot(p.astype(bf16), v_at(c * tk), preferred_element_type=f32)
            state[i] = (m, l, acc)
    return [acc / l for _, l, acc in state]


def _attend_sum_in_v(q, k_at, v_at, kv_len, tk, d):
    n = kv_len // tk
    m = acc = None
    s_next = _nt_dot(q, k_at(0))
    for c in range(n):
        s = s_next
        if c + 1 < n:
            s_next = _nt_dot(q, k_at((c + 1) * tk))
        s_max = jnp.max(s, axis=1, keepdims=True)
        if c == 0:
            m = s_max
            acc = jnp.dot(jnp.exp2((s - m).astype(bf16)), v_at(0), preferred_element_type=f32)
        else:
            m_new = jnp.maximum(m, s_max)
            alpha = jnp.exp2(m - m_new)
            acc = alpha * acc + jnp.dot(jnp.exp2((s - m_new).astype(bf16)), v_at(c * tk),
                                        preferred_element_type=f32)
            m = m_new
    return acc[:, :d] / acc[:, d:]


def _kv_chunk(kv_len):
    for tk in (768, 1408, 512, 256, 128):
        if kv_len % tk == 0:
            return tk
    raise ValueError(kv_len)


def _ctx_or_all(n_ctx_tiles, n_ctx, kv_all, run):
    is_ctx = pl.program_id(2) < n_ctx_tiles

    @pl.when(is_ctx)
    def _():
        run(n_ctx)

    @pl.when(jnp.logical_not(is_ctx))
    def _():
        run(kv_all)


def _attn_a_kernel(q_ref, k_ref, v_ref, o_ref, *, n_ctx):
    g, tq, d = q_ref.shape

    def run(kv_len):
        tk = _kv_chunk(kv_len)

        def k_at(off):
            return k_ref[0, off:off + tk, :]

        def v_at(off):
            return v_ref[off:off + tk, :]

        o = _attend_sum_in_v(q_ref[...].reshape(g * tq, d), k_at, v_at, kv_len, tk, d)
        for h in range(g):
            o_ref[:, h * d:(h + 1) * d] = o[h * tq:(h + 1) * tq].astype(o_ref.dtype)

    _ctx_or_all(n_ctx // tq, n_ctx, k_ref.shape[1], run)


def _attn_c_kernel(lam_ref, dn_ref, q_ref, k_ref, v_ref, o_ref, *, n_ctx, lam_init):
    tq = q_ref.shape[1]

    def run(kv_len):
        tk = _kv_chunk(kv_len)
        def v_at(off):
            return v_ref[off:off + tk, :]

        (o1,) = _attend([(q_ref[0], lambda off: k_ref[0, off:off + tk, :], v_at)], kv_len, tk)
        (o2,) = _attend([(q_ref[1], lambda off: k_ref[1, off:off + tk, :], v_at)], kv_len, tk)
        lam_p = lam_ref[...]
        lam = (jnp.exp(jnp.sum(lam_p[0:1] * lam_p[1:2], axis=-1, keepdims=True))
               - jnp.exp(jnp.sum(lam_p[2:3] * lam_p[3:4], axis=-1, keepdims=True)) + lam_init)
        o = o1 - lam * o2
        y = o * lax.rsqrt(jnp.mean(o * o, axis=-1, keepdims=True) + EPS) * dn_ref[...]
        o_ref[...] = (y * (1.0 - lam_init)).astype(o_ref.dtype)

    _ctx_or_all(n_ctx // tq, n_ctx, k_ref.shape[1], run)


def attention_a(qk, va, *, n_batch, rows_per_batch, n_ctx, a_heads, a_kv_heads):
    t = qk.shape[1]
    group = a_heads // a_kv_heads
    tq = ROW_TILE
    n_q = rows_per_batch // tq
    return pl.pallas_call(
        functools.partial(_attn_a_kernel, n_ctx=n_ctx),
        grid=(n_batch, a_kv_heads, n_q),
        in_specs=[
            pl.BlockSpec((group, tq, HEAD_DIM), lambda b, h, i: (h, b * n_q + i, 0)),
            pl.BlockSpec((1, rows_per_batch, HEAD_DIM), lambda b, h, i: (a_heads + h, b, 0)),
            pl.BlockSpec((rows_per_batch, 2 * HEAD_DIM), lambda b, h, i: (b, h)),
        ],
        out_specs=pl.BlockSpec((tq, group * HEAD_DIM), lambda b, h, i: (b * n_q + i, h)),
        out_shape=jax.ShapeDtypeStruct((t, a_heads * HEAD_DIM), bf16),
        compiler_params=_params("parallel", "parallel", "arbitrary"),
        name="attention_a",
    )(qk, qk, va)


def attention_c(qk, vc, lam_params, diff_norm, *, n_batch, rows_per_batch, n_ctx, head0_q, head0_k,
                c_heads, lam_init):
    t = qk.shape[1]
    dv = 2 * HEAD_DIM
    tq = ROW_TILE
    n_q = rows_per_batch // tq
    return pl.pallas_call(
        functools.partial(_attn_c_kernel, n_ctx=n_ctx, lam_init=lam_init),
        grid=(n_batch, c_heads, n_q),
        in_specs=[
            pl.BlockSpec((4, HEAD_DIM), lambda b, h, i: (0, 0)),
            pl.BlockSpec((1, dv), lambda b, h, i: (0, 0)),
            pl.BlockSpec((2, tq, HEAD_DIM), lambda b, h, i: (head0_q // 2 + h, b * n_q + i, 0)),
            pl.BlockSpec((2, rows_per_batch, HEAD_DIM), lambda b, h, i: (head0_k // 2 + h, b, 0)),
            pl.BlockSpec((rows_per_batch, dv), lambda b, h, i: (b, h)),
        ],
        out_specs=pl.BlockSpec((tq, dv), lambda b, h, i: (b * n_q + i, h)),
        out_shape=jax.ShapeDtypeStruct((t, c_heads * dv), bf16),
        compiler_params=_params("parallel", "parallel", "arbitrary"),
        name="attention_c",
    )(lam_params, diff_norm.reshape(1, dv), qk, qk, vc)


def _hgrn_tables(rev):
    c = SCAN_CHUNK
    half = SCAN_SUB // 2
    t_i = lax.broadcasted_iota(i32, (c, c), 0)
    u_i = lax.broadcasted_iota(i32, (c, c), 1)
    blk8 = t_i & ~(SCAN_SUB - 1)
    blk4 = t_i & ~(half - 1)
    if rev:
        rows_upto = [t_i, blk4 + half - 1, blk8 + half, blk8]
        sums = jnp.concatenate([(u_i >= r).astype(f32) for r in rows_upto], axis=0)
    else:
        rows_upto = [t_i, blk4, blk8 + half - 1, blk8 + SCAN_SUB - 1]
        sums = jnp.concatenate([(u_i <= r).astype(f32) for r in rows_upto], axis=0)
    causal = (u_i >= t_i) if rev else (u_i <= t_i)
    same8 = (u_i & ~(SCAN_SUB - 1)) == blk8
    same4 = (u_i & ~(half - 1)) == blk4
    return sums, same8 & jnp.logical_not(same4) & causal, same4 & causal


def _hgrn_gates(qp, z, lb):
    q = _silu(qp.astype(f32))
    one_m = 1.0 - lb
    g = jnp.log(jnp.maximum(lb + one_m * jax.nn.sigmoid(z), F_MIN))
    k = one_m * jax.nn.sigmoid(-z)
    return q, k, g


def _hgrn_local(q, k, b_all, v, tables, rev):
    c = SCAN_CHUNK
    nb = c // SCAN_SUB
    _, across, inner = tables
    b, rho4, rho_mid, rho_edge = (b_all[i * c:(i + 1) * c] for i in range(4))
    end = 0 if rev else c - 1
    b_end = b[end:end + 1, :]
    q_in = (q * jnp.exp(b)).astype(bf16)
    d_st = _tn_dot(v, (k * jnp.exp(b_end - b)).astype(bf16))

    def edge_row(j):
        return j * SCAN_SUB if rev else j * SCAN_SUB + SCAN_SUB - 1

    row = lax.broadcasted_iota(i32, (c, HEAD_DIM), 0)
    k_til = k * jnp.exp(rho_edge - b)
    qs, ks = [], []
    for j in (range(1, nb) if rev else range(nb - 1)):
        rho = b[edge_row(j):edge_row(j) + 1, :]
        q_rows = (row < j * SCAN_SUB) if rev else (row >= (j + 1) * SCAN_SUB)
        qs.append((q * jnp.exp(jnp.where(q_rows, b - rho, -jnp.inf))).astype(bf16))
        k_rows = (row >= j * SCAN_SUB) & (row < (j + 1) * SCAN_SUB)
        ks.append(jnp.where(k_rows, k_til, 0.0).astype(bf16))
    att = _nt_dot(jnp.concatenate(qs, axis=1), jnp.concatenate(ks, axis=1))

    att_mid = _nt_dot((q * jnp.exp(jnp.minimum(b - rho_mid, 0.0))).astype(bf16),
                      (k * jnp.exp(jnp.minimum(rho_mid - b, 0.0))).astype(bf16))
    att_in = _nt_dot((q * jnp.exp(b - rho4)).astype(bf16), (k * jnp.exp(rho4 - b)).astype(bf16))
    att = att + jnp.where(across, att_mid, 0.0) + jnp.where(inner, att_in, 0.0)
    return jnp.dot(att.astype(bf16), v, preferred_element_type=f32), q_in, d_st, jnp.exp(b_end)


def _hgrn_kernel(lb_ref, qf_ref, vf_ref, zf_ref, qb_ref, vb_ref, zb_ref, of_ref, ob_ref, sf_ref, sb_ref):
    @pl.when(pl.program_id(2) == 0)
    def _():
        sf_ref[...] = jnp.zeros_like(sf_ref)
        sb_ref[...] = jnp.zeros_like(sb_ref)

    lb = lb_ref[...]
    n_chunks = qf_ref.shape[0] // SCAN_CHUNK
    work = []
    for c in range(n_chunks):
        work.append((slice(c * SCAN_CHUNK, (c + 1) * SCAN_CHUNK), False))
        work.append((slice((n_chunks - 1 - c) * SCAN_CHUNK, (n_chunks - c) * SCAN_CHUNK), True))
    refs = {False: (qf_ref, vf_ref, zf_ref, of_ref), True: (qb_ref, vb_ref, zb_ref, ob_ref)}
    tables = {False: _hgrn_tables(False), True: _hgrn_tables(True)}

    gates = [_hgrn_gates(refs[rev][0][rows, :], refs[rev][2][rows, :], lb) for rows, rev in work]
    sums = [jnp.dot(tables[rev][0], g, precision=lax.Precision.HIGHEST, preferred_element_type=f32)
            for (_, rev), (_, _, g) in zip(work, gates)]
    local = [_hgrn_local(q, k, b_all, refs[rev][1][rows, :], tables[rev], rev)
             for (rows, rev), (q, k, _), b_all in zip(work, gates, sums)]
    state = {False: sf_ref[...], True: sb_ref[...]}
    for (rows, rev), (o_in, q_in, d_st, decay) in zip(work, local):
        st = state[rev]
        refs[rev][3][rows, :] = o_in + _nt_dot(q_in, st.astype(bf16))
        state[rev] = st * decay + d_st
    sf_ref[...] = state[False]
    sb_ref[...] = state[True]


def hgrn_scan(qio, zfb, lower_bound, *, n_batch, rows_per_batch, b_heads):
    t = qio.shape[0]
    tile = SCAN_TILE
    n_tiles = rows_per_batch // tile

    def fwd(b, h, j):
        return b * n_tiles + j

    def bwd(b, h, j):
        return b * n_tiles + jnp.where(j == 0, 0, n_tiles - j)

    def spec(tile_fn, col0):
        return pl.BlockSpec((tile, HEAD_DIM), lambda b, h, j: (tile_fn(b, h, j), col0 + h))

    return pl.pallas_call(
        _hgrn_kernel,
        grid=(n_batch, b_heads, n_tiles),
        in_specs=[
            pl.BlockSpec((1, HEAD_DIM), lambda b, h, j: (0, h)),
            spec(fwd, 0), spec(fwd, b_heads), spec(fwd, 0),
            spec(bwd, 0), spec(bwd, b_heads), spec(bwd, b_heads),
        ],
        out_specs=[spec(fwd, 0), spec(bwd, 0)],
        out_shape=[jax.ShapeDtypeStruct((t, b_heads * HEAD_DIM), f32)] * 2,
        scratch_shapes=[pltpu.VMEM((HEAD_DIM, HEAD_DIM), f32)] * 2,
        compiler_params=_params("parallel", "parallel", "arbitrary"),
        name="hgrn_scan",
    )(lower_bound.reshape(1, -1), qio, qio, zfb, qio, qio, zfb)


def _hgrn_out_kernel(of_ref, ob_ref, og_ref, w_ref, o_ref):
    o = of_ref[...] + ob_ref[...]
    w = w_ref[...]
    for c in range(o.shape[1] // HEAD_DIM):
        sl = slice(c * HEAD_DIM, (c + 1) * HEAD_DIM)
        x = o[:, sl]
        y = x * lax.rsqrt(jnp.mean(x * x, axis=-1, keepdims=True) + EPS) * w
        o_ref[:, sl] = (y * _silu(og_ref[:, sl].astype(f32))).astype(o_ref.dtype)


def hgrn_readout(o_f, o_b, qio, hgrn_norm, b_heads):
    t, w = o_f.shape
    tile = ROW_TILE
    row_spec = pl.BlockSpec((tile, w), lambda i: (i, 0))
    return pl.pallas_call(
        _hgrn_out_kernel,
        grid=(t // tile,),
        in_specs=[row_spec, row_spec, pl.BlockSpec((tile, w), lambda i: (i, 2)),
                  pl.BlockSpec((1, HEAD_DIM), lambda i: (0, 0))],
        out_specs=row_spec,
        out_shape=jax.ShapeDtypeStruct((t, w), bf16),
        compiler_params=_params("parallel"),
        name="hgrn_readout",
    )(o_f, o_b, qio, hgrn_norm.reshape(1, HEAD_DIM))


def _router_kernel(h_ref, w_ref, b_ref, e_ref, wt_ref, rank_ref, cnt_ref, run_ref):
    n_e = w_ref.shape[0]
    tm = h_ref.shape[0]
    per = n_e // N_GROUPS

    @pl.when(pl.program_id(0) == 0)
    def _():
        run_ref[...] = jnp.zeros_like(run_ref)

    scores = jax.nn.sigmoid(_nt_dot(w_ref[...], h_ref[...]))
    biased = scores + b_ref[...]
    neg = -jnp.inf

    grp = biased.reshape(N_GROUPS, per, tm)
    idx_in = lax.broadcasted_iota(i32, grp.shape, 1)
    m1 = jnp.max(grp, axis=1, keepdims=True)
    first = jnp.min(jnp.where(grp == m1, idx_in, per), axis=1, keepdims=True)
    m2 = jnp.max(jnp.where(idx_in == first, neg, grp), axis=1, keepdims=True)
    cur = m1 + m2

    g_idx = lax.broadcasted_iota(i32, cur.shape, 0)
    g_sel = jnp.zeros(cur.shape, f32)
    for _ in range(TOPK_GROUPS):
        mx = jnp.max(cur, axis=0, keepdims=True)
        pick = g_idx == jnp.min(jnp.where(cur == mx, g_idx, N_GROUPS), axis=0, keepdims=True)
        g_sel = jnp.where(pick, 1.0, g_sel)
        cur = jnp.where(pick, neg, cur)
    e_mask = jnp.broadcast_to(g_sel, (N_GROUPS, per, tm)).reshape(n_e, tm) > 0.0

    e_idx = lax.broadcasted_iota(i32, (n_e, tm), 0)
    cur = jnp.where(e_mask, biased, MASK_SCORE)
    ws, picks = [], []
    member = jnp.zeros((n_e, tm), f32)
    for r in range(TOP_K):
        mx = jnp.max(cur, axis=0, keepdims=True)
        first = jnp.min(jnp.where(cur == mx, e_idx, n_e), axis=0, keepdims=True)
        pick = e_idx == first
        e_ref[r:r + 1, :] = first
        ws.append(jnp.sum(jnp.where(pick, scores, 0.0), axis=0, keepdims=True))
        cur = jnp.where(pick, neg, cur)
        member = jnp.where(pick, 1.0, member)
        picks.append(pick)
    total = ws[0]
    for r in range(1, TOP_K):
        total = total + ws[r]
    for r in range(TOP_K):
        wt_ref[r:r + 1, :] = ws[r] / total * ROUTE_SCALE

    u_i = lax.broadcasted_iota(i32, (tm, tm), 0)
    t_i = lax.broadcasted_iota(i32, (tm, tm), 1)
    prefix = jnp.dot(member.astype(bf16), (u_i <= t_i).astype(bf16), preferred_element_type=f32)
    rank = run_ref[...] + prefix
    for r in range(TOP_K):
        rank_ref[r:r + 1, :] = jnp.sum(jnp.where(picks[r], rank, 0.0), axis=0, keepdims=True).astype(i32)
    run_ref[...] = run_ref[...] + jnp.sum(member, axis=1, keepdims=True)
    cnt_ref[...] = run_ref[...].astype(i32)


def moe_route(h, w_router_t, b_router):
    t, d = h.shape
    n_e = w_router_t.shape[0]
    tm = 512
    assert t % tm == 0
    out_spec = pl.BlockSpec((TOP_K, tm), lambda i: (0, i))
    return pl.pallas_call(
        _router_kernel,
        grid=(t // tm,),
        in_specs=[pl.BlockSpec((tm, d), lambda i: (i, 0)), pl.BlockSpec((n_e, d), lambda i: (0, 0)),
                  pl.BlockSpec((n_e, 1), lambda i: (0, 0))],
        out_specs=[out_spec, out_spec, out_spec, pl.BlockSpec((n_e, 1), lambda i: (0, 0))],
        out_shape=[jax.ShapeDtypeStruct((TOP_K, t), i32), jax.ShapeDtypeStruct((TOP_K, t), f32),
                   jax.ShapeDtypeStruct((TOP_K, t), i32), jax.ShapeDtypeStruct((n_e, 1), i32)],
        scratch_shapes=[pltpu.VMEM((n_e, 1), f32)],
        compiler_params=_params("arbitrary"),
        name="moe_route",
    )(h, w_router_t, b_router.reshape(n_e, 1).astype(f32))


GATHER_DEPTH = 3


def _expert_kernel(be_ref, idx0_ref, idx1_ref, idx_ahead_ref, h_ref, wg_ref, wu_ref, wd_ref, y_ref,
                   xbuf, sem, x_s, wg_b, wu_b, wd_b):
    i = pl.program_id(0)
    last = pl.num_programs(0) - 1
    tm = xbuf.shape[1] // PACK_ROWS
    slot = i % GATHER_DEPTH
    ahead = (i + GATHER_DEPTH - 1) % GATHER_DEPTH

    def gather_start(idx_ref, dst_slot):
        for r in range(tm):
            src_row = pl.multiple_of(idx_ref[0, 0, r] * PACK_ROWS, PACK_ROWS)
            pltpu.make_async_copy(h_ref.at[pl.ds(src_row, PACK_ROWS)],
                                  xbuf.at[dst_slot, pl.ds(r * PACK_ROWS, PACK_ROWS)],
                                  sem.at[dst_slot]).start(priority=r % 2)

    def gather_wait(dst_slot):
        pltpu.make_async_copy(h_ref.at[pl.ds(0, tm * PACK_ROWS)], xbuf.at[dst_slot], sem.at[dst_slot]).wait()

    @pl.when(i == 0)
    def _():
        gather_start(idx0_ref, 0)
        gather_start(idx1_ref, 1)

    @pl.when((i == 0) | (be_ref[i] != be_ref[jnp.maximum(i - 1, 0)]))
    def _():
        wg_b[...] = wg_ref[0, 0].astype(bf16)
        wu_b[...] = wu_ref[0, 0].astype(bf16)
        wd_b[...] = wd_ref[0, 0].astype(bf16)

    gather_wait(slot)
    half = PACK_ROWS * HEAD_DIM
    for s in range(PACK_ROWS):
        lo, hi = _unpack_word(xbuf[slot, pl.ds(s, tm, stride=PACK_ROWS), :])
        x_s[:, s * HEAD_DIM:(s + 1) * HEAD_DIM] = lo.astype(bf16)
        x_s[:, half + s * HEAD_DIM:half + (s + 1) * HEAD_DIM] = hi.astype(bf16)
    gather_start(idx_ahead_ref, ahead)
    x = x_s[...]
    g = jnp.dot(x, wg_b[...], preferred_element_type=f32)
    u = jnp.dot(x, wu_b[...], preferred_element_type=f32)
    a = (_silu(g) * u).astype(bf16)
    y = jnp.dot(a, wd_b[...], preferred_element_type=f32)
    for s, word in enumerate(_pack_words(y)):
        y_ref[pl.ds(s, tm, stride=PACK_ROWS), :] = word

    @pl.when(i == last)
    def _():
        for k in range(1, GATHER_DEPTH):
            gather_wait((i + k) % GATHER_DEPTH)


def moe_experts(h_packed, slot_tok, w_gate, w_up, w_down, layer, block_e):
    tm = slot_tok.shape[0] // block_e.shape[0]
    n_blocks = block_e.shape[0]
    assert n_blocks >= GATHER_DEPTH
    d, d_e = w_gate.shape[-2:]
    idx = slot_tok.reshape(n_blocks, 1, tm)

    def idx_spec(block_fn):
        return pl.BlockSpec((1, 1, tm), lambda i, be: (block_fn(i), 0, 0), memory_space=pltpu.SMEM)

    def w_spec(shape):
        return pl.BlockSpec((1, 1) + shape, lambda i, be: (layer, be[i], 0, 0))

    return pl.pallas_call(
        _expert_kernel,
        grid_spec=pltpu.PrefetchScalarGridSpec(
            num_scalar_prefetch=1,
            grid=(n_blocks,),
            in_specs=[idx_spec(lambda i: 0), idx_spec(lambda i: 1),
                      idx_spec(lambda i: jnp.minimum(i + GATHER_DEPTH - 1, n_blocks - 1)),
                      pl.BlockSpec(memory_space=pl.ANY),
                      w_spec((d, d_e)), w_spec((d, d_e)), w_spec((d_e, d))],
            out_specs=pl.BlockSpec((tm * PACK_ROWS, HEAD_DIM), lambda i, be: (i, 0)),
            scratch_shapes=[pltpu.VMEM((GATHER_DEPTH, tm * PACK_ROWS, HEAD_DIM), jnp.uint32),
                            pltpu.SemaphoreType.DMA((GATHER_DEPTH,)), pltpu.VMEM((tm, d), bf16),
                            pltpu.VMEM((d, d_e), bf16), pltpu.VMEM((d, d_e), bf16), pltpu.VMEM((d_e, d), bf16)],
        ),
        out_shape=jax.ShapeDtypeStruct((n_blocks * tm * PACK_ROWS, HEAD_DIM), jnp.uint32),
        compiler_params=_params("arbitrary"),
        name="moe_experts",
    )(block_e, idx, idx, idx, h_packed, w_gate, w_up, w_down)


def _combine_kernel(idx_ref, idx_next_ref, w_ref, y_ref, h_ref, wg_ref, wu_ref, wd_ref, x_ref, modc_ref, modb_ref,
                    o_ref, buf, sem, *, rows_per_batch, n_ctx):
    i = pl.program_id(0)
    last = pl.num_programs(0) - 1
    n_k = buf.shape[1]
    rows = buf.shape[2] // PACK_ROWS
    slot = i % 2

    def gather_start(src_idx_ref, dst_slot):
        for r in range(rows):
            for kk in range(n_k):
                src_row = pl.multiple_of(src_idx_ref[0, kk, r] * PACK_ROWS, PACK_ROWS)
                pltpu.make_async_copy(y_ref.at[pl.ds(src_row, PACK_ROWS)],
                                      buf.at[dst_slot, kk, pl.ds(r * PACK_ROWS, PACK_ROWS)],
                                      sem.at[dst_slot]).start(priority=(r * n_k + kk) % 2)

    def gather_wait(dst_slot):
        for kk in range(n_k):
            pltpu.make_async_copy(y_ref.at[pl.ds(0, rows * PACK_ROWS)], buf.at[dst_slot, kk],
                                  sem.at[dst_slot]).wait()

    @pl.when(i == 0)
    def _():
        gather_start(idx_ref, 0)

    gather_start(idx_next_ref, 1 - slot)
    h = h_ref[...]
    a = (_silu(jnp.dot(h, wg_ref[...], preferred_element_type=f32))
         * jnp.dot(h, wu_ref[...], preferred_element_type=f32)).astype(bf16)
    shared = jnp.dot(a, wd_ref[...], preferred_element_type=f32)
    mod = jnp.where(_row_is_ctx(i, rows, rows_per_batch, n_ctx), modc_ref[0], modb_ref[0])
    gather_wait(slot)
    w = w_ref[...]
    half = PACK_ROWS * HEAD_DIM
    for s in range(PACK_ROWS):
        acc_lo = acc_hi = None
        for kk in range(n_k):
            lo, hi = _unpack_word(buf[slot, kk, pl.ds(s, rows, stride=PACK_ROWS), :])
            w_k = w[:, kk:kk + 1]
            acc_lo = lo * w_k if kk == 0 else acc_lo + lo * w_k
            acc_hi = hi * w_k if kk == 0 else acc_hi + hi * w_k
        for cs, acc in ((slice(s * HEAD_DIM, (s + 1) * HEAD_DIM), acc_lo),
                        (slice(half + s * HEAD_DIM, half + (s + 1) * HEAD_DIM), acc_hi)):
            o_ref[:, cs] = x_ref[:, cs] + mod[:, cs] * (shared[:, cs] + acc)

    @pl.when(i == last)
    def _():
        gather_wait(1 - slot)


def moe_combine(y, dest, top_w, h, w_sg, w_su, w_sd, x, mod, rows_per_batch, n_ctx):
    n_k, t = dest.shape
    d = 2 * PACK_ROWS * HEAD_DIM
    d_sh = w_sg.shape[1]
    rows = 128
    n_tiles = t // rows
    n_b = mod.shape[0] - 1
    tiles_per_batch = rows_per_batch // rows
    idx = dest.reshape(n_k, n_tiles, rows).transpose(1, 0, 2)
    row_spec = pl.BlockSpec((rows, d), lambda i: (i, 0))

    def idx_spec(tile_fn):
        return pl.BlockSpec((1, n_k, rows), lambda i: (tile_fn(i), 0, 0), memory_space=pltpu.SMEM)

    return pl.pallas_call(
        functools.partial(_combine_kernel, rows_per_batch=rows_per_batch, n_ctx=n_ctx),
        grid=(n_tiles,),
        in_specs=[idx_spec(lambda i: i), idx_spec(lambda i: jnp.minimum(i + 1, n_tiles - 1)),
                  pl.BlockSpec((rows, n_k), lambda i: (i, 0)),
                  pl.BlockSpec(memory_space=pl.ANY),
                  row_spec,
                  pl.BlockSpec((d, d_sh), lambda i: (0, 0)), pl.BlockSpec((d, d_sh), lambda i: (0, 0)),
                  pl.BlockSpec((d_sh, d), lambda i: (0, 0)),
                  row_spec,
                  pl.BlockSpec((1, 1, d), lambda i: (n_b, 0, 0)),
                  pl.BlockSpec((1, 1, d), lambda i: (i // tiles_per_batch, 0, 0))],
        out_specs=row_spec,
        out_shape=jax.ShapeDtypeStruct((t, d), f32),
        scratch_shapes=[pltpu.VMEM((2, n_k, rows * PACK_ROWS, HEAD_DIM), jnp.uint32),
                        pltpu.SemaphoreType.DMA((2,))],
        compiler_params=_params("arbitrary"),
        name="moe_combine",
    )(idx, idx, top_w.T, y, h, w_sg, w_su, w_sd, x, mod, mod)


def _moe_plan(top_e, rank, counts, tm):
    n_k, t = top_e.shape
    n_experts = counts.shape[0]
    n_blocks = -(-(n_k * t) // tm) + n_experts
    padded = (counts + tm - 1) // tm * tm
    pad_end = jnp.cumsum(padded)
    pad_start = pad_end - padded
    onehot = top_e[:, :, None] == jnp.arange(n_experts, dtype=i32)
    dest = jnp.sum(jnp.where(onehot, pad_start, 0), axis=-1) + rank - 1
    n_used = pad_end[-1] // tm
    blk = jnp.minimum(jnp.arange(n_blocks, dtype=i32), n_used - 1)
    block_e = jnp.sum((pad_end[None, :] <= (blk * tm)[:, None]).astype(i32), axis=1)
    block_e = jnp.minimum(block_e, n_experts - 1)
    tok = jnp.broadcast_to(jnp.arange(t, dtype=i32)[None, :], (n_k, t))
    sorted_tok = jnp.sort((top_e * t + tok).reshape(-1)) % t
    of_block = block_e[:, None] == jnp.arange(n_experts, dtype=i32)
    grp_start = jnp.cumsum(counts) - counts

    def per_block(table):
        return jnp.sum(jnp.where(of_block, table, 0), axis=1, keepdims=True)

    j = jnp.arange(n_blocks * tm, dtype=i32).reshape(n_blocks, tm) - per_block(pad_start)
    valid = (j >= 0) & (j < per_block(counts))
    src = jnp.clip(per_block(grp_start) + j, 0, n_k * t - 1)
    slot_tok = jnp.where(valid, jnp.take(sorted_tok, src.reshape(-1)).reshape(n_blocks, tm), 0)
    return dest.astype(i32), slot_tok.reshape(-1), block_e


def _rope_perm():
    quarter = HEAD_DIM // 4
    return jnp.concatenate([jnp.arange(q * quarter, (q + 1) * quarter, dtype=i32) for q in (0, 2, 1, 3)])


def _rope_tables(n_ctx, n_lat):
    t = jnp.arange(n_lat)
    pos = jnp.stack([t // GRID_W, t % GRID_W], axis=-1).astype(f32)
    n_freq = HEAD_DIM // 4
    inv_freq = ROPE_THETA ** (-jnp.arange(n_freq, dtype=f32) / n_freq)
    ang = pos[:, :, None] * inv_freq
    cos, sin = jnp.cos(ang), jnp.sin(ang)
    cos_t = jnp.concatenate([cos[:, 0], cos[:, 1], cos[:, 0], cos[:, 1]], axis=-1)
    sin_t = jnp.concatenate([-sin[:, 0], -sin[:, 1], sin[:, 0], sin[:, 1]], axis=-1)
    cos_t = jnp.concatenate([jnp.ones((n_ctx, HEAD_DIM), f32), cos_t], axis=0)
    sin_t = jnp.concatenate([jnp.zeros((n_ctx, HEAD_DIM), f32), sin_t], axis=0)
    return cos_t, sin_t


def kernel(x, c, ctx, c_ctx, w_ada, b_ada, norm_mix, norm_ffn, w_in, q_norm_a, k_norm_a, hgrn_lb_logits, hgrn_out_norm, q_norm_c, k_norm_c, lambda_q1, lambda_k1, lambda_q2, lambda_k2, diff_sub_norm, w_branch, w_out, w_router, b_router, w_exp_gate, w_exp_up, w_exp_down, w_sh_gate, w_sh_up, w_sh_down):
    n_batch, n_lat, d = x.shape
    n_ctx = ctx.shape[1]
    depth = w_ada.shape[0]
    rows_b = n_ctx + n_lat
    t = n_batch * rows_b
    branch_w = d // 2
    a_heads = branch_w // HEAD_DIM
    a_kv = a_heads // 4
    b_heads = branch_w // HEAD_DIM
    c_heads = branch_w // (2 * HEAD_DIM)
    n_experts = w_router.shape[-1]
    tm = MM_TILE_M if rows_b % MM_TILE_M == 0 else ROW_TILE
    tiles_per_batch = rows_b // tm
    assert n_ctx % ROW_TILE == 0 and n_lat % ROW_TILE == 0 and rows_b % SCAN_TILE == 0

    cos_t, sin_t = _rope_tables(n_ctx, n_lat)
    p_lb = jax.nn.softmax(hgrn_lb_logits.astype(f32), axis=0)
    lower_bounds = jnp.cumsum(p_lb, axis=0) - p_lb[0]
    cond = jnp.zeros((8, d), f32).at[:n_batch].set(c).at[n_batch].set(c_ctx)

    xa = jnp.concatenate([ctx, x], axis=1).reshape(t, d)

    def mod_specs(tn):
        ctx_spec = pl.BlockSpec((1, 1, tn), lambda i, j: (n_batch, 0, j))
        b_spec = pl.BlockSpec((1, 1, tn), lambda i, j: (i // tiles_per_batch, 0, j))
        return ctx_spec, b_spec

    splits = (a_heads * HEAD_DIM, a_kv * HEAD_DIM, a_kv * HEAD_DIM,
              branch_w, branch_w, branch_w, branch_w, branch_w,
              branch_w, branch_w, branch_w, 3 * d)
    offs = [0]
    for s in splits:
        offs.append(offs[-1] + s)

    def cols(w, *ids):
        return jnp.concatenate([w[:, offs[i]:offs[i + 1]] for i in ids], axis=1).astype(bf16)

    n_qk_heads = a_heads + a_kv + 4 * c_heads
    qk_tn = (n_qk_heads // 2) * HEAD_DIM if n_qk_heads % 2 == 0 else n_qk_heads * HEAD_DIM
    sm_scale = HEAD_DIM ** -0.5 * LOG2E

    for l in range(depth):
        lam_init = 0.8 - 0.6 * math.exp(-0.3 * l)
        mod = fused_matmul([(cond, w_ada)], [(b_ada[l].reshape(1, -1), pl.BlockSpec((1, 1024), lambda i, j: (0, j)))],
                           _ep_bias, jax.ShapeDtypeStruct((8, 6 * d), f32), _tile_spec(8, 1024), 8, 1024,
                           a_fn=_silu, name="ada_ln", layer=l)
        mod = mod[:n_batch + 1].reshape(n_batch + 1, 6, 1, d)
        mods = [mod[:, k] for k in range(6)]

        (h,) = norm_modulate(xa, norm_mix[l], mods[0], mods[1], rows_b, n_ctx)
        w_l = w_in[l]
        perm = _rope_perm()
        gains = jnp.concatenate([jnp.tile(q_norm_a[l][perm] * sm_scale, a_heads),
                                 jnp.tile(k_norm_a[l][perm], a_kv),
                                 jnp.tile(q_norm_c[l][perm] * sm_scale, 2 * c_heads),
                                 jnp.tile(k_norm_c[l][perm], 2 * c_heads)])
        w_qk = cols(w_l, 0, 1, 8, 9)
        w_qk = w_qk.reshape(d, n_qk_heads, HEAD_DIM)[:, :, perm].reshape(d, n_qk_heads * HEAD_DIM)
        qk = fused_matmul(
            [(h, w_qk)],
            [(gains.reshape(1, -1).astype(f32), pl.BlockSpec((1, qk_tn), lambda i, j: (0, j))),
             (cos_t, pl.BlockSpec((tm, HEAD_DIM), lambda i, j: (i % tiles_per_batch, 0))),
             (sin_t, pl.BlockSpec((tm, HEAD_DIM), lambda i, j: (i % tiles_per_batch, 0)))],
            _ep_qk, jax.ShapeDtypeStruct((n_qk_heads, t, HEAD_DIM), bf16),
            pl.BlockSpec((qk_tn // HEAD_DIM, tm, HEAD_DIM), lambda i, j: (j, i, 0)), tm, qk_tn, name="in_proj_qk")
        w_va = jnp.pad(cols(w_l, 2).reshape(d, a_kv, HEAD_DIM), ((0, 0), (0, 0), (0, HEAD_DIM)))
        ones_cols = jnp.tile(jnp.concatenate([jnp.zeros((HEAD_DIM,), f32), jnp.ones((HEAD_DIM,), f32)]), a_kv)
        va_n = 2 * a_kv * HEAD_DIM
        va = fused_matmul([(h, w_va.reshape(d, va_n))],
                          [(ones_cols.reshape(1, va_n), pl.BlockSpec((1, va_n), lambda i, j: (0, 0)))],
                          _ep_bias_store(bf16), jax.ShapeDtypeStruct((t, va_n), bf16),
                          _tile_spec(tm, va_n), tm, va_n, name="in_proj_va")
        vc = fused_matmul([(h, cols(w_l, 10))], [], _ep_store(bf16), jax.ShapeDtypeStruct((t, branch_w), bf16),
                          _tile_spec(tm, 1024), tm, 1024, name="in_proj_vc")
        qio = fused_matmul([(h, cols(w_l, 3, 4, 7))], [], _ep_store(bf16), jax.ShapeDtypeStruct((t, 3 * branch_w), bf16),
                           _tile_spec(tm, 1024), tm, 1024, name="in_proj_hgrn")
        zfb = fused_matmul([(h, cols(w_l, 5, 6))], [], _ep_store(f32), jax.ShapeDtypeStruct((t, 2 * branch_w), f32),
                           _tile_spec(tm, 1024), tm, 1024, name="in_proj_forget")
        gates = fused_matmul([(h, cols(w_l, 11))], [], _ep_sigmoid, jax.ShapeDtypeStruct((t, 3 * d), bf16),
                             _tile_spec(tm, 1024), tm, 1024, name="in_proj_gates")

        dims = dict(n_batch=n_batch, rows_per_batch=rows_b, n_ctx=n_ctx)
        o_a = attention_a(qk, va, a_heads=a_heads, a_kv_heads=a_kv, **dims)
        lam_params = jnp.stack([lambda_q1[l], lambda_k1[l], lambda_q2[l], lambda_k2[l]]).astype(f32)
        o_c = attention_c(qk, vc, lam_params, diff_sub_norm[l], head0_q=a_heads + a_kv,
                          head0_k=a_heads + a_kv + 2 * c_heads, c_heads=c_heads, lam_init=lam_init, **dims)
        o_f, o_r = hgrn_scan(qio, zfb, lower_bounds[l], n_batch=n_batch, rows_per_batch=rows_b, b_heads=b_heads)
        o_b = hgrn_readout(o_f, o_r, qio, hgrn_out_norm[l], b_heads)

        wb = w_branch[l].astype(bf16)
        n_col = d // 512
        y = fused_matmul(
            [(o_a, wb[0]), (o_b, wb[1]), (o_c, wb[2])],
            [(gates, pl.BlockSpec((tm, 512), lambda i, j, r=r: (i, r * n_col + j))) for r in range(3)],
            _ep_merge, jax.ShapeDtypeStruct((t, d), bf16), _tile_spec(tm, 512), tm, 512, name="merge")
        ctx_spec, b_spec = mod_specs(512)
        xa = fused_matmul(
            [(y, w_out[l].astype(bf16))],
            [(xa, _tile_spec(tm, 512)), (mods[2], ctx_spec), (mods[2], b_spec)],
            _make_ep_residual(tm, rows_b, n_ctx, False), jax.ShapeDtypeStruct((t, d), f32),
            _tile_spec(tm, 512), tm, 512, name="out_proj")

        h2, h2_packed = norm_modulate(xa, norm_ffn[l], mods[3], mods[4], rows_b, n_ctx, packed_rows=True)
        top_e, top_w, rank, counts = moe_route(h2, w_router[l].T.astype(bf16), b_router[l])
        dest, slot_tok, block_e = _moe_plan(top_e, rank, counts.reshape(-1), MOE_TILE)
        ys = moe_experts(h2_packed, slot_tok, w_exp_gate, w_exp_up, w_exp_down, l, block_e)
        xa = moe_combine(ys, dest, top_w, h2, w_sh_gate[l].astype(bf16), w_sh_up[l].astype(bf16),
                         w_sh_down[l].astype(bf16), xa, mods[5], rows_b, n_ctx)

    return xa.reshape(n_batch, rows_b, d)[:, n_ctx:]
```
